```python
import math
import jax
import jax.numpy as jnp
from jax import lax
import numpy as np

D_MODEL = 2048
BATCH = 32
SEQ = 256
DEPTH = 4
DEC_BATCH = 2
DEC_SEQ = 2048
PAST_LEN = 256

GRID_W = 64
MIX_WIDTH = D_MODEL
W_A = MIX_WIDTH // 4
W_B = MIX_WIDTH // 2
W_C = MIX_WIDTH - W_A - W_B
DA_QK = 64
DA_V = 2 * DA_QK
DA_HEADS = W_A // DA_V
MLA_Q_RANK = 512
MLA_KV_RANK = 256
MLA_NOPE = 128
MLA_ROPE = 64
MLA_V = 128
MLA_HEADS = W_B // MLA_V
MLA_SCALE = (MLA_NOPE + MLA_ROPE) ** -0.5
NA_DIM = 128
NA_HEADS = W_C // NA_DIM
NA_KH = 8
NA_KW = 16
IN_SPLITS = [W_A, W_A, W_A, MLA_Q_RANK, MLA_KV_RANK, MLA_ROPE, W_C, W_C, W_C]
IN_OFFSETS = [sum(IN_SPLITS[:i + 1]) for i in range(len(IN_SPLITS) - 1)]
D_IN = sum(IN_SPLITS)
D_FF = 5504
N_EXPERTS = 8
TOP_K = 2
D_FF_EXPERT = 2816
N_DENSE = (DEPTH + 1) // 2
N_MOE = DEPTH // 2
ROPE_THETA = 10000.0
LN_EPS = 1e-5
RMS_EPS = 1e-6
NEG_INF = -1e30
QBLK = 128
DEEPNORM_ALPHA = (2.0 * DEPTH) ** 0.25
DEEPNORM_BETA = (8.0 * DEPTH) ** -0.25
F32 = jnp.float32

kernel_name = 'hybrid_diffusion_trunk_step'


def layer_norm(x, g, b):
    xf = x.astype(F32)
    mu = jnp.mean(xf, axis=-1, keepdims=True)
    var = jnp.mean(jnp.square(xf - mu), axis=-1, keepdims=True)
    return ((xf - mu) * lax.rsqrt(var + LN_EPS) * g + b).astype(x.dtype)


def rms_norm(x, g):
    xf = x.astype(F32)
    return (xf * lax.rsqrt(jnp.mean(xf * xf, axis=-1, keepdims=True) + RMS_EPS) * g).astype(x.dtype)


def axial_rope_tables(n_tokens, dim):
    t = jnp.arange(n_tokens)
    row = (t // GRID_W).astype(F32)
    col = (t % GRID_W).astype(F32)
    half = dim // 2
    inv_freq = ROPE_THETA ** (-jnp.arange(0, half, 2, dtype=F32) / half)
    ar = row[:, None] * inv_freq[None, :]
    ac = col[:, None] * inv_freq[None, :]
    ang = jnp.concatenate([ar, ar, ac, ac], axis=-1)
    return jnp.cos(ang), jnp.sin(ang)


def apply_axial_rope(x, cos, sin):
    d = x.shape[-1]
    xr = x.reshape(x.shape[:-1] + (2, 2, d // 4))
    rot = jnp.stack([-xr[..., 1, :], xr[..., 0, :]], axis=-2).reshape(x.shape)
    return (x.astype(F32) * cos + rot.astype(F32) * sin).astype(x.dtype)


def blockwise(f, *qs):
    b, s = qs[0].shape[:2]
    n = s // QBLK
    qb = tuple(jnp.moveaxis(q.reshape((b, n, QBLK) + q.shape[2:]), 1, 0) for q in qs)
    out = lax.map(lambda blk: f(*blk), qb)
    out = jnp.moveaxis(out, 0, 1)
    return out.reshape((b, s) + out.shape[3:])


def softmax_attn(q, k, v):
    s = jnp.einsum('bqhd,bkhd->bhqk', q, k).astype(F32) * (q.shape[-1] ** -0.5)
    p = jax.nn.softmax(s, axis=-1).astype(v.dtype)
    return jnp.einsum('bhqk,bkhd->bqhd', p, v)


def diff_attn(q, k, v, lam, lam_init, g_sub):
    s = jnp.einsum('bqhcd,bkhcd->bhcqk', q, k).astype(F32) * (DA_QK ** -0.5)
    p = jax.nn.softmax(s, axis=-1)
    p = (p[:, :, 0] - lam * p[:, :, 1]).astype(v.dtype)
    o = jnp.einsum('bhqk,bkhd->bqhd', p, v)
    return rms_norm(o, g_sub) * (1.0 - lam_init)


def mla_queries(cq_raw, g_q, w_uq):
    b, s, _ = cq_raw.shape
    q = (rms_norm(cq_raw, g_q) @ w_uq).reshape(b, s, MLA_HEADS, MLA_NOPE + MLA_ROPE)
    return q[..., :MLA_NOPE], q[..., MLA_NOPE:]


def mla_expand(ckv, w_ukv):
    b, t, _ = ckv.shape
    kv = (ckv @ w_ukv).reshape(b, t, MLA_HEADS, MLA_NOPE + MLA_V)
    return kv[..., :MLA_NOPE], kv[..., MLA_NOPE:]


def mla_attn(q_nope, q_rope, k_nope, k_rope, v):
    s = (jnp.einsum('bqhd,bkhd->bhqk', q_nope, k_nope)
         + jnp.einsum('bqhr,bkr->bhqk', q_rope, k_rope)).astype(F32) * MLA_SCALE
    p = jax.nn.softmax(s, axis=-1).astype(v.dtype)
    return jnp.einsum('bhqk,bkhd->bqhd', p, v)


def na_latent(q, k, v, k_ctx, v_ctx, rpb):
    b, s, h, d = q.shape
    rows_n = s // GRID_W
    kh = min(NA_KH, rows_n)
    kw = NA_KW
    scale = d ** -0.5
    qg = q.reshape(b, rows_n, GRID_W, h, d)
    kg = k.reshape(b, rows_n, GRID_W, h, d)
    vg = v.reshape(b, rows_n, GRID_W, h, d)
    r = jnp.arange(rows_n)
    row_start = jnp.clip(r - kh // 2, 0, rows_n - kh)
    rows = row_start[:, None] + jnp.arange(kh)[None, :]
    cq = jnp.arange(GRID_W)
    col_start = jnp.clip(cq - kw // 2, 0, GRID_W - kw)
    kc = jnp.arange(GRID_W)
    col_valid = (kc[None, :] >= col_start[:, None]) & (kc[None, :] < col_start[:, None] + kw)
    k_rows = kg[:, rows]
    v_rows = vg[:, rows]
    s_win = jnp.einsum('brqhd,brkwhd->bhrqkw', qg, k_rows).astype(F32) * scale
    roff = rows - r[:, None] + (NA_KH - 1)
    coff = jnp.clip(kc[None, :] - cq[:, None], -(NA_KW - 1), NA_KW - 1) + (NA_KW - 1)
    bias = rpb[:, roff[:, None, :, None], coff[None, :, None, :]]
    s_win = s_win + bias[None].astype(F32)
    s_win = jnp.where(col_valid[None, None, None, :, None, :], s_win, NEG_INF)
    s_ctx = jnp.einsum('brqhd,bthd->bhrqt', qg, k_ctx).astype(F32) * scale
    n_win = kh * GRID_W
    s_all = jnp.concatenate([s_win.reshape(b, h, rows_n, GRID_W, n_win), s_ctx], axis=-1)
    p = jax.nn.softmax(s_all, axis=-1).astype(v.dtype)
    p_win = p[..., :n_win].reshape(b, h, rows_n, GRID_W, kh, GRID_W)
    p_ctx = p[..., n_win:]
    o = (jnp.einsum('bhrqkw,brkwhd->brqhd', p_win, v_rows)
         + jnp.einsum('bhrqt,bthd->brqhd', p_ctx, v_ctx))
    return o.reshape(b, s, h, d)


def merge_heads(o_a, o_b, o_c, w_out):
    b, s = o_a.shape[:2]
    o = jnp.concatenate([o_a.reshape(b, s, W_A), o_b.reshape(b, s, W_B), o_c.reshape(b, s, W_C)], axis=-1)
    return o @ w_out


def mixer_context(h, w_in, w_out, lam, lam_init, da_subln, mla_gq, mla_gkv, mla_wuq, mla_wukv):
    b, n, _ = h.shape
    da_q, da_k, da_v, cq, ckv_raw, kr, na_q, na_k, na_v = jnp.split(h @ w_in, IN_OFFSETS, axis=-1)
    da_q = da_q.reshape(b, n, DA_HEADS, 2, DA_QK)
    da_k = da_k.reshape(b, n, DA_HEADS, 2, DA_QK)
    da_v = da_v.reshape(b, n, DA_HEADS, DA_V)
    o_a = blockwise(lambda q: diff_attn(q, da_k, da_v, lam, lam_init, da_subln), da_q)
    q_nope, q_rope = mla_queries(cq, mla_gq, mla_wuq)
    ckv = rms_norm(ckv_raw, mla_gkv)
    k_nope, v_b = mla_expand(ckv, mla_wukv)
    o_b = blockwise(lambda qn, qr: mla_attn(qn, qr, k_nope, kr, v_b), q_nope, q_rope)
    na_q = na_q.reshape(b, n, NA_HEADS, NA_DIM)
    na_k = na_k.reshape(b, n, NA_HEADS, NA_DIM)
    na_v = na_v.reshape(b, n, NA_HEADS, NA_DIM)
    o_c = blockwise(lambda q: softmax_attn(q, na_k, na_v), na_q)
    return merge_heads(o_a, o_b, o_c, w_out), (da_k, da_v, ckv, kr, na_k, na_v)


def mixer_latent(h, ck_da, cv_da, c_ckv, c_kr, ck_na, cv_na, rope_da, rope_mla,
                 w_in, w_out, lam, lam_init, da_subln, mla_gq, mla_gkv, mla_wuq, mla_wukv, na_rpb):
    b, s, _ = h.shape
    da_q, da_k, da_v, cq, ckv_raw, kr, na_q, na_k, na_v = jnp.split(h @ w_in, IN_OFFSETS, axis=-1)
    cos_a, sin_a = rope_da
    ca, sa = cos_a[:, None, None, :], sin_a[:, None, None, :]
    da_q = apply_axial_rope(da_q.reshape(b, s, DA_HEADS, 2, DA_QK), ca, sa)
    da_k = apply_axial_rope(da_k.reshape(b, s, DA_HEADS, 2, DA_QK), ca, sa)
    k_all = jnp.concatenate([da_k, ck_da], axis=1)
    v_all = jnp.concatenate([da_v.reshape(b, s, DA_HEADS, DA_V), cv_da], axis=1)
    o_a = blockwise(lambda q: diff_attn(q, k_all, v_all, lam, lam_init, da_subln), da_q)
    cos_m, sin_m = rope_mla
    q_nope, q_rope = mla_queries(cq, mla_gq, mla_wuq)
    q_rope = apply_axial_rope(q_rope, cos_m[:, None, :], sin_m[:, None, :])
    kr = apply_axial_rope(kr, cos_m, sin_m)
    ckv_all = jnp.concatenate([rms_norm(ckv_raw, mla_gkv), c_ckv], axis=1)
    kr_all = jnp.concatenate([kr, c_kr], axis=1)
    k_nope, v_b = mla_expand(ckv_all, mla_wukv)
    o_b = blockwise(lambda qn, qr: mla_attn(qn, qr, k_nope, kr_all, v_b), q_nope, q_rope)
    o_c = na_latent(na_q.reshape(b, s, NA_HEADS, NA_DIM), na_k.reshape(b, s, NA_HEADS, NA_DIM),
                    na_v.reshape(b, s, NA_HEADS, NA_DIM), ck_na, cv_na, na_rpb)
    return merge_heads(o_a, o_b, o_c, w_out)


def swiglu(h, w1, w3, w2):
    return (jax.nn.silu(h @ w1) * (h @ w3)) @ w2


def moe_ffn(h, w_router, w1, w3, w2):
    logits = (h @ w_router).astype(F32)
    top_v, top_i = lax.top_k(logits, TOP_K)
    gates = jax.nn.softmax(top_v, axis=-1)
    dense_gate = jnp.sum(jax.nn.one_hot(top_i, N_EXPERTS, dtype=F32) * gates[..., None], axis=-2).astype(h.dtype)
    out = jnp.zeros_like(h)
    for e in range(N_EXPERTS):
        out = out + dense_gate[..., e:e + 1] * swiglu(h, w1[e], w3[e], w2[e])
    return out


def ada_modulation(cond, w_ada, b_ada):
    m = (jax.nn.silu(cond) @ w_ada + b_ada).reshape(-1, 1, 6 * D_MODEL)
    return jnp.split(m, 6, axis=-1)


def modulate(x, shift, scale):
    return x * (1.0 + scale) + shift


def setup_inputs(seed: int = 0) -> dict:
    key = jax.random.key(seed)
    keys = jax.random.split(key, 40)
    counter = [0]

    def nrm(shape, scale=1.0):
        k = keys[counter[0]]
        counter[0] += 1
        return jax.random.normal(k, shape, jnp.float32) * scale

    def gain(shape):
        return 1.0 + nrm(shape, 0.02)

    D = D_MODEL
    inp = {}
    inp['x_prompt'] = nrm((BATCH, SEQ, D))
    inp['x_sample'] = nrm((DEC_BATCH, DEC_SEQ, D))
    inp['cache_da_k'] = nrm((DEC_BATCH, DEPTH, PAST_LEN, DA_HEADS, 2, DA_QK))
    inp['cache_da_v'] = nrm((DEC_BATCH, DEPTH, PAST_LEN, DA_HEADS, DA_V))
    inp['cache_mla_ckv'] = nrm((DEC_BATCH, DEPTH, PAST_LEN, MLA_KV_RANK))
    inp['cache_mla_krope'] = nrm((DEC_BATCH, DEPTH, PAST_LEN, MLA_ROPE))
    inp['cache_na_k'] = nrm((DEC_BATCH, DEPTH, PAST_LEN, NA_HEADS, NA_DIM))
    inp['cache_na_v'] = nrm((DEC_BATCH, DEPTH, PAST_LEN, NA_HEADS, NA_DIM))
    inp['c'] = nrm((DEC_BATCH, D))
    inp['c_ctx'] = nrm((D,))
    inp['w_ada'] = nrm((DEPTH, D, 6 * D), 0.5 * D ** -0.5)
    inp['b_ada'] = nrm((DEPTH, 6 * D), 0.02)
    inp['w_in'] = nrm((DEPTH, D, D_IN), D ** -0.5)
    inp['da_lq1'] = nrm((DEPTH, DA_QK), 0.1)
    inp['da_lk1'] = nrm((DEPTH, DA_QK), 0.1)
    inp['da_lq2'] = nrm((DEPTH, DA_QK), 0.1)
    inp['da_lk2'] = nrm((DEPTH, DA_QK), 0.1)
    inp['da_subln'] = gain((DEPTH, DA_V))
    inp['mla_gq'] = gain((DEPTH, MLA_Q_RANK))
    inp['mla_gkv'] = gain((DEPTH, MLA_KV_RANK))
    inp['mla_wuq'] = nrm((DEPTH, MLA_Q_RANK, MLA_HEADS * (MLA_NOPE + MLA_ROPE)), MLA_Q_RANK ** -0.5)
    inp['mla_wukv'] = nrm((DEPTH, MLA_KV_RANK, MLA_HEADS * (MLA_NOPE + MLA_V)), MLA_KV_RANK ** -0.5)
    inp['na_rpb'] = nrm((DEPTH, NA_HEADS, 2 * NA_KH - 1, 2 * NA_KW - 1), 0.02)
    inp['w_out'] = nrm((DEPTH, MIX_WIDTH, D), DEEPNORM_BETA * MIX_WIDTH ** -0.5)
    inp['ln1_g'] = gain((DEPTH, D))
    inp['ln1_b'] = nrm((DEPTH, D), 0.02)
    inp['ln2_g'] = gain((DEPTH, D))
    inp['ln2_b'] = nrm((DEPTH, D), 0.02)
    inp['ffn_w1'] = nrm((N_DENSE, D, D_FF), D ** -0.5)
    inp['ffn_w3'] = nrm((N_DENSE, D, D_FF), D ** -0.5)
    inp['ffn_w2'] = nrm((N_DENSE, D_FF, D), DEEPNORM_BETA * D_FF ** -0.5)
    inp['moe_router'] = nrm((N_MOE, D, N_EXPERTS), D ** -0.5)
    inp['moe_w1'] = nrm((N_MOE, N_EXPERTS, D, D_FF_EXPERT), D ** -0.5)
    inp['moe_w3'] = nrm((N_MOE, N_EXPERTS, D, D_FF_EXPERT), D ** -0.5)
    inp['moe_w2'] = nrm((N_MOE, N_EXPERTS, D_FF_EXPERT, D), DEEPNORM_BETA * D_FF_EXPERT ** -0.5)
    return inp


def reference(x_prompt, x_sample, cache_da_k, cache_da_v, cache_mla_ckv, cache_mla_krope,
              cache_na_k, cache_na_v, c, c_ctx, w_ada, b_ada, w_in, da_lq1, da_lk1, da_lq2, da_lk2,
              da_subln, mla_gq, mla_gkv, mla_wuq, mla_wukv, na_rpb, w_out, ln1_g, ln1_b, ln2_g, ln2_b,
              ffn_w1, ffn_w3, ffn_w2, moe_router, moe_w1, moe_w3, moe_w2):
    n_lat = x_sample.shape[1]
    rope_da = axial_rope_tables(n_lat, DA_QK)
    rope_mla = axial_rope_tables(n_lat, MLA_ROPE)
    y_p = x_prompt
    y_s = x_sample
    st_da_k, st_da_v, st_ckv, st_kr, st_na_k, st_na_v = [], [], [], [], [], []
    for l in range(DEPTH):
        lam_init = 0.8 - 0.6 * math.exp(-0.3 * l)
        lam = (jnp.exp(jnp.sum(da_lq1[l].astype(F32) * da_lk1[l].astype(F32)))
               - jnp.exp(jnp.sum(da_lq2[l].astype(F32) * da_lk2[l].astype(F32))) + lam_init)
        sh1_p, sc1_p, g1_p, sh2_p, sc2_p, g2_p = ada_modulation(c_ctx, w_ada[l], b_ada[l])
        sh1_s, sc1_s, g1_s, sh2_s, sc2_s, g2_s = ada_modulation(c, w_ada[l], b_ada[l])
        mix_p, (dk, dv, ckv, kr, nk, nv) = mixer_context(
            modulate(y_p, sh1_p, sc1_p), w_in[l], w_out[l], lam, lam_init, da_subln[l],
            mla_gq[l], mla_gkv[l], mla_wuq[l], mla_wukv[l])
        st_da_k.append(dk)
        st_da_v.append(dv)
        st_ckv.append(ckv)
        st_kr.append(kr)
        st_na_k.append(nk)
        st_na_v.append(nv)
        mix_s = mixer_latent(
            modulate(y_s, sh1_s, sc1_s), cache_da_k[:, l], cache_da_v[:, l], cache_mla_ckv[:, l],
            cache_mla_krope[:, l], cache_na_k[:, l], cache_na_v[:, l], rope_da, rope_mla,
            w_in[l], w_out[l], lam, lam_init, da_subln[l], mla_gq[l], mla_gkv[l], mla_wuq[l],
            mla_wukv[l], na_rpb[l])
        y_p = layer_norm(DEEPNORM_ALPHA * y_p + g1_p * mix_p, ln1_g[l], ln1_b[l])
        y_s = layer_norm(DEEPNORM_ALPHA * y_s + g1_s * mix_s, ln1_g[l], ln1_b[l])
        h_p = modulate(y_p, sh2_p, sc2_p)
        h_s = modulate(y_s, sh2_s, sc2_s)
        i = l // 2
        if l % 2 == 0:
            f_p = swiglu(h_p, ffn_w1[i], ffn_w3[i], ffn_w2[i])
            f_s = swiglu(h_s, ffn_w1[i], ffn_w3[i], ffn_w2[i])
        else:
            f_p = moe_ffn(h_p, moe_router[i], moe_w1[i], moe_w3[i], moe_w2[i])
            f_s = moe_ffn(h_s, moe_router[i], moe_w1[i], moe_w3[i], moe_w2[i])
        y_p = layer_norm(DEEPNORM_ALPHA * y_p + g2_p * f_p, ln2_g[l], ln2_b[l])
        y_s = layer_norm(DEEPNORM_ALPHA * y_s + g2_s * f_s, ln2_g[l], ln2_b[l])
    new_da_k = jnp.stack(st_da_k, axis=1)
    new_da_v = jnp.stack(st_da_v, axis=1)
    new_mla_ckv = jnp.stack(st_ckv, axis=1)
    new_mla_krope = jnp.stack(st_kr, axis=1)
    new_na_k = jnp.stack(st_na_k, axis=1)
    new_na_v = jnp.stack(st_na_v, axis=1)
    return (y_p, y_s, new_da_k, new_da_v, new_mla_ckv, new_mla_krope, new_na_k, new_na_v)
```

```python
import functools
import math

import numpy as np
import jax
import jax.numpy as jnp
from jax import lax
from jax.experimental import pallas as pl
from jax.experimental.pallas import tpu as pltpu

F32 = jnp.float32
BF16 = jnp.bfloat16

GRID_W = 64
DA_QK = 64
DA_V = 128
DA_HEADS = 4
MLA_Q_RANK = 512
MLA_KV_RANK = 256
MLA_NOPE = 128
MLA_ROPE = 64
MLA_V = 128
MLA_HEADS = 8
MLA_SCALE = (MLA_NOPE + MLA_ROPE) ** -0.5
NA_DIM = 128
NA_HEADS = 4
NA_KH = 8
NA_KW = 16
N_EXPERTS = 8
ROPE_THETA = 10000.0
LN_EPS = 1e-5
RMS_EPS = 1e-6
NEG_INF = -1e30
LANE = 128
COND_ROWS = 8
VMEM_LIMIT = 56 * 1024 * 1024

C_DAQ, C_DAK, C_DAV, C_CQ, C_CKV, C_NAQ, C_NAK, C_NAV = 0, 512, 1024, 1536, 2048, 2560, 3072, 3584
P_COLS = 4096
MLA_QK_PAD = 256


def _cparams(sem):
    return pltpu.CompilerParams(dimension_semantics=sem, vmem_limit_bytes=VMEM_LIMIT)


def _dot(a, b):
    return jnp.dot(a, b, preferred_element_type=F32)


def _dot_nt(a, b):
    return lax.dot_general(a, b, (((1,), (1,)), ((), ())), preferred_element_type=F32)


def _silu(x):
    return x * (1.0 / (1.0 + jnp.exp(-x)))


def _rms(x, g):
    return x * lax.rsqrt(jnp.mean(x * x, axis=-1, keepdims=True) + RMS_EPS) * g


def _rope(x, cos, sa, sb):
    return x * cos + pltpu.roll(x, LANE - 16, 1) * sa + pltpu.roll(x, 16, 1) * sb


def _ada_kernel(c_ref, w_ref, b_ref, o_ref):
    s = _silu(c_ref[...]).astype(BF16)
    o_ref[...] = _dot(s, w_ref[...].astype(BF16)) + b_ref[...]


def _ada(cond, w_ada, b_ada):
    n_layers, d, n = w_ada.shape
    tn = 1024
    return pl.pallas_call(
        _ada_kernel,
        out_shape=jax.ShapeDtypeStruct((n_layers, COND_ROWS, n), F32),
        grid=(n_layers, n // tn),
        in_specs=[pl.BlockSpec((COND_ROWS, d), lambda l, j: (0, 0)),
                  pl.BlockSpec((None, d, tn), lambda l, j: (l, 0, j)),
                  pl.BlockSpec((None, 1, tn), lambda l, j: (l, 0, j))],
        out_specs=pl.BlockSpec((None, COND_ROWS, tn), lambda l, j: (l, 0, j)),
        compiler_params=_cparams(("arbitrary", "arbitrary")),
    )(cond, w_ada, b_ada.reshape(n_layers, 1, n))


class _Tok:
    def __init__(self, mp, s_lat, n_lat_batches):
        self.mp, self.s_lat, self.nb = mp, s_lat, n_lat_batches
        self.m = mp + s_lat * n_lat_batches

    def rid(self, i, tm):
        r0 = i * tm
        return jnp.where(r0 < self.mp, 0, 1 + (r0 - self.mp) // self.s_lat)

    def mod_spec(self, layer, which, tm, d, grid_rank=1):
        if grid_rank == 1:
            return pl.BlockSpec((None, None, 1, d), lambda i: (layer, self.rid(i, tm), 0, which))
        return pl.BlockSpec((None, None, 1, d), lambda i, k: (layer, self.rid(i, tm), 0, which))


def _modulate_kernel(x_ref, sh_ref, sc_ref, o_ref):
    o_ref[...] = (x_ref[...] * (1.0 + sc_ref[...]) + sh_ref[...]).astype(BF16)


def _modulate(tok, y, mods, layer, tm):
    m, d = y.shape
    return pl.pallas_call(
        _modulate_kernel,
        out_shape=jax.ShapeDtypeStruct((m, d), BF16),
        grid=(m // tm,),
        in_specs=[pl.BlockSpec((tm, d), lambda i: (i, 0)),
                  tok.mod_spec(layer, 0, tm, d), tok.mod_spec(layer, 1, tm, d)],
        out_specs=pl.BlockSpec((tm, d), lambda i: (i, 0)),
        compiler_params=_cparams(("arbitrary",)),
    )(y, mods, mods)


def _mm_kernel(x_ref, w_ref, o_ref):
    o_ref[...] = _dot(x_ref[...], w_ref[...])


def _mm(x, w, tm, tn):
    m, k = x.shape
    n = w.shape[1]
    return pl.pallas_call(
        _mm_kernel,
        out_shape=jax.ShapeDtypeStruct((m, n), F32),
        grid=(n // tn, m // tm),
        in_specs=[pl.BlockSpec((tm, k), lambda j, i: (i, 0)),
                  pl.BlockSpec((k, tn), lambda j, i: (0, j))],
        out_specs=pl.BlockSpec((tm, tn), lambda j, i: (i, j)),
        compiler_params=_cparams(("arbitrary", "arbitrary")),
    )(x, w)


def _mla_prep_kernel(*refs, norm, with_q, rope, da_rope):
    it = iter(refs)
    ckv_ref = next(it)
    cq_ref = next(it) if with_q else None
    daq_ref = next(it) if da_rope else None
    dak_ref = next(it) if da_rope else None
    gkv_ref, wk_ref, wv_ref = next(it), next(it), next(it)
    gq_ref = next(it) if with_q else None
    wuq_ref = next(it) if with_q else None
    if rope:
        cm_ref, sam_ref, sbm_ref = next(it), next(it), next(it)
    if da_rope:
        cd_ref, sad_ref, sbd_ref = next(it), next(it), next(it)
    kcat_ref, vb_ref, ckvn_ref = next(it), next(it), next(it)
    qcat_ref = next(it) if with_q else None
    qa_ref = next(it) if da_rope else None
    ka_ref = next(it) if da_rope else None

    ckv = ckv_ref[:, :MLA_KV_RANK]
    ckvn = _rms(ckv, gkv_ref[...]) if norm else ckv
    ckvn_ref[...] = ckvn
    kr = ckv_ref[:, MLA_KV_RANK:MLA_KV_RANK + LANE]
    if rope:
        kr = _rope(kr, cm_ref[...], sam_ref[...], sbm_ref[...])
    kr = kr.astype(BF16)
    cb = ckvn.astype(BF16)
    kn = _dot(cb, wk_ref[...]).astype(BF16)
    vb_ref[...] = _dot(cb, wv_ref[...]).astype(BF16)
    for h in range(MLA_HEADS):
        kcat_ref[:, h * MLA_QK_PAD:h * MLA_QK_PAD + MLA_NOPE] = kn[:, h * MLA_NOPE:(h + 1) * MLA_NOPE]
        kcat_ref[:, h * MLA_QK_PAD + MLA_NOPE:(h + 1) * MLA_QK_PAD] = kr
    if with_q:
        cqn = _rms(cq_ref[...], gq_ref[...]).astype(BF16)
        q = _dot(cqn, wuq_ref[...])
        for h in range(MLA_HEADS):
            lo = h * MLA_QK_PAD
            qcat_ref[:, lo:lo + MLA_NOPE] = q[:, lo:lo + MLA_NOPE].astype(BF16)
            qr = q[:, lo + MLA_NOPE:lo + MLA_QK_PAD]
            if rope:
                qr = _rope(qr, cm_ref[...], sam_ref[...], sbm_ref[...])
            qcat_ref[:, lo + MLA_NOPE:lo + MLA_QK_PAD] = qr.astype(BF16)
    if da_rope:
        for h in range(DA_HEADS):
            sl = slice(h * LANE, (h + 1) * LANE)
            qa_ref[:, sl] = _rope(daq_ref[:, sl], cd_ref[...], sad_ref[...], sbd_ref[...]).astype(BF16)
            ka_ref[:, sl] = _rope(dak_ref[:, sl], cd_ref[...], sad_ref[...], sbd_ref[...]).astype(BF16)


def _mla_prep(src, row0, rows, tm, gkv, wk, wv, gq=None, wuq=None, rope_m=None, rope_d=None,
              norm=True, s_lat=None):
    with_q = wuq is not None
    rope = rope_m is not None
    da_rope = rope_d is not None
    b0 = row0 // tm
    n = rows // tm
    ckv_blk = C_CKV // 512 if src.shape[1] == P_COLS else 0
    const = lambda shape: pl.BlockSpec(shape, lambda i: (0,) * len(shape))
    ins, specs = [src], [pl.BlockSpec((tm, 512), lambda i: (b0 + i, ckv_blk))]
    if with_q:
        ins.append(src)
        specs.append(pl.BlockSpec((tm, 512), lambda i: (b0 + i, C_CQ // 512)))
    if da_rope:
        ins += [src, src]
        specs += [pl.BlockSpec((tm, 512), lambda i: (b0 + i, C_DAQ // 512)),
                  pl.BlockSpec((tm, 512), lambda i: (b0 + i, C_DAK // 512))]
    ins += [gkv, wk, wv]
    specs += [const(gkv.shape), const(wk.shape), const(wv.shape)]
    if with_q:
        ins += [gq, wuq]
        specs += [const(gq.shape), const(wuq.shape)]
    nt = (s_lat // tm) if rope else 1
    if rope:
        ins += list(rope_m)
        specs += [pl.BlockSpec((tm, LANE), lambda i: (i % nt, 0))] * 3
    if da_rope:
        ins += list(rope_d)
        specs += [pl.BlockSpec((tm, LANE), lambda i: (i % nt, 0))] * 3
    outs = [jax.ShapeDtypeStruct((rows, MLA_HEADS * MLA_QK_PAD), BF16),
            jax.ShapeDtypeStruct((rows, MLA_HEADS * MLA_V), BF16),
            jax.ShapeDtypeStruct((rows, MLA_KV_RANK), F32)]
    if with_q:
        outs.append(jax.ShapeDtypeStruct((rows, MLA_HEADS * MLA_QK_PAD), BF16))
    if da_rope:
        outs += [jax.ShapeDtypeStruct((rows, DA_HEADS * LANE), BF16)] * 2
    ospecs = [pl.BlockSpec((tm, o.shape[1]), lambda i: (i, 0)) for o in outs]
    return pl.pallas_call(
        functools.partial(_mla_prep_kernel, norm=norm, with_q=with_q, rope=rope, da_rope=da_rope),
        out_shape=outs, grid=(n,), in_specs=specs, out_specs=ospecs,
        compiler_params=_cparams(("arbitrary",)),
    )(*ins)


def _softmax2(s, s2):
    m = jnp.max(s, axis=-1, keepdims=True)
    if s2 is not None:
        m = jnp.maximum(m, jnp.max(s2, axis=-1, keepdims=True))
    e = jnp.exp(s - m)
    den = jnp.sum(e, axis=-1, keepdims=True)
    e2 = None
    if s2 is not None:
        e2 = jnp.exp(s2 - m)
        den = den + jnp.sum(e2, axis=-1, keepdims=True)
    inv = 1.0 / den
    return e * inv, (None if e2 is None else e2 * inv)


def _attn_kernel(*refs, heads, dk, dv, scale, has_ctx):
    if has_ctx:
        q_ref, k_ref, v_ref, kc_ref, vc_ref, o_ref = refs
    else:
        q_ref, k_ref, v_ref, o_ref = refs
    for h in range(heads):
        q = q_ref[:, h * dk:(h + 1) * dk].astype(BF16)
        s = _dot_nt(q, k_ref[:, h * dk:(h + 1) * dk].astype(BF16)) * scale
        s2 = None
        if has_ctx:
            s2 = _dot_nt(q, kc_ref[:, h * dk:(h + 1) * dk].astype(BF16)) * scale
        p, p2 = _softmax2(s, s2)
        o = _dot(p.astype(BF16), v_ref[:, h * dv:(h + 1) * dv].astype(BF16))
        if has_ctx:
            o = o + _dot(p2.astype(BF16), vc_ref[:, h * dv:(h + 1) * dv].astype(BF16))
        o_ref[:, h * dv:(h + 1) * dv] = o.astype(BF16)


def _attn(q, k, v, *, nb, sq, sk, tq, heads, hp, dk, dv, scale, qrow0=0, krow0=0, qcol0=0, kcol0=0, vcol0=0,
          kc=None, vc=None, skc=0):
    nq = sq // tq
    ng = heads // hp
    wq, wv = hp * dk, hp * dv
    has_ctx = kc is not None
    ins = [q, k, v]
    specs = [pl.BlockSpec((tq, wq), lambda b, g, i: (qrow0 // tq + b * nq + i, qcol0 // wq + g)),
             pl.BlockSpec((sk, wq), lambda b, g, i: (krow0 // sk + b, kcol0 // wq + g)),
             pl.BlockSpec((sk, wv), lambda b, g, i: (krow0 // sk + b, vcol0 // wv + g))]
    if has_ctx:
        ins += [kc, vc]
        specs += [pl.BlockSpec((skc, wq), lambda b, g, i: (b, g)),
                  pl.BlockSpec((skc, wv), lambda b, g, i: (b, g))]
    return pl.pallas_call(
        functools.partial(_attn_kernel, heads=hp, dk=dk, dv=dv, scale=scale, has_ctx=has_ctx),
        out_shape=jax.ShapeDtypeStruct((nb * sq, heads * dv), BF16),
        grid=(nb, ng, nq), in_specs=specs,
        out_specs=pl.BlockSpec((tq, wv), lambda b, g, i: (b * nq + i, g)),
        compiler_params=_cparams(("arbitrary", "arbitrary", "arbitrary")),
    )(*ins)


def _da_kernel(*refs, has_ctx, lam_init):
    if has_ctx:
        q_ref, k_ref, v_ref, kc_ref, vc_ref, lq1, lk1, lq2, lk2, g_ref, o_ref = refs
    else:
        q_ref, k_ref, v_ref, lq1, lk1, lq2, lk2, g_ref, o_ref = refs
    lam = (jnp.exp(jnp.sum(lq1[...] * lk1[...], axis=-1, keepdims=True))
           - jnp.exp(jnp.sum(lq2[...] * lk2[...], axis=-1, keepdims=True)) + lam_init)
    first = lax.broadcasted_iota(jnp.int32, (1, LANE), 1) < DA_QK
    scale = DA_QK ** -0.5
    for h in range(DA_HEADS):
        sl = slice(h * LANE, (h + 1) * LANE)
        q = q_ref[:, sl].astype(F32) * scale
        qs = (jnp.where(first, q, 0.0).astype(BF16), jnp.where(first, 0.0, q).astype(BF16))
        k = k_ref[:, sl].astype(BF16)
        kc = kc_ref[:, sl].astype(BF16) if has_ctx else None
        ps = []
        for c in range(2):
            s = _dot_nt(qs[c], k)
            s2 = _dot_nt(qs[c], kc) if has_ctx else None
            ps.append(_softmax2(s, s2))
        p = (ps[0][0] - lam * ps[1][0]).astype(BF16)
        o = _dot(p, v_ref[:, sl].astype(BF16))
        if has_ctx:
            p2 = (ps[0][1] - lam * ps[1][1]).astype(BF16)
            o = o + _dot(p2, vc_ref[:, sl].astype(BF16))
        o = _rms(o, g_ref[...]) * (1.0 - lam_init)
        o_ref[:, sl] = o.astype(BF16)


def _da(q, k, v, lams, g, lam_init, *, nb, sq, sk, tq, qrow0=0, krow0=0, qcol0=0, kcol0=0, vcol0=0,
        vrow0=None, kc=None, vc=None, skc=0):
    nq = sq // tq
    w = DA_HEADS * LANE
    vrow0 = krow0 if vrow0 is None else vrow0
    has_ctx = kc is not None
    ins = [q, k, v]
    specs = [pl.BlockSpec((tq, w), lambda b, i: (qrow0 // tq + b * nq + i, qcol0 // w)),
             pl.BlockSpec((sk, w), lambda b, i: (krow0 // sk + b, kcol0 // w)),
             pl.BlockSpec((sk, w), lambda b, i: (vrow0 // sk + b, vcol0 // w))]
    if has_ctx:
        ins += [kc, vc]
        specs += [pl.BlockSpec((None, skc, w), lambda b, i: (b, 0, 0))] * 2
    ins += list(lams) + [g]
    specs += [pl.BlockSpec((1, DA_QK), lambda b, i: (0, 0))] * 4 + [pl.BlockSpec((1, DA_V), lambda b, i: (0, 0))]
    return pl.pallas_call(
        functools.partial(_da_kernel, has_ctx=has_ctx, lam_init=lam_init),
        out_shape=jax.ShapeDtypeStruct((nb * sq, w), BF16),
        grid=(nb, nq), in_specs=specs,
        out_specs=pl.BlockSpec((tq, w), lambda b, i: (b * nq + i, 0)),
        compiler_params=_cparams(("arbitrary", "arbitrary")),
    )(*ins)


def _na_kernel(q_ref, k_ref, v_ref, kc_ref, vc_ref, bias_ref, o_ref, *, rows_n, kh):
    r = pl.program_id(1)
    rs = jnp.clip(r - kh // 2, 0, rows_n - kh)
    start = pl.multiple_of(rs * GRID_W, GRID_W)
    nwin = kh * GRID_W
    scale = NA_DIM ** -0.5
    for h in range(NA_HEADS):
        sl = slice(h * NA_DIM, (h + 1) * NA_DIM)
        q = q_ref[:, sl].astype(BF16)
        kw = k_ref[pl.ds(start, nwin), sl].astype(BF16)
        vw = v_ref[pl.ds(start, nwin), sl].astype(BF16)
        s = _dot_nt(q, kw) * scale + bias_ref[h]
        s2 = _dot_nt(q, kc_ref[:, sl].astype(BF16)) * scale
        p, p2 = _softmax2(s, s2)
        o = _dot(p.astype(BF16), vw) + _dot(p2.astype(BF16), vc_ref[:, sl].astype(BF16))
        o_ref[:, sl] = o.astype(BF16)


def _na_latent(proj, row0, nb, s_lat, kc, vc, bias):
    rows_n = s_lat // GRID_W
    kh = min(NA_KH, rows_n)
    w = NA_HEADS * NA_DIM
    skc = kc.shape[1]

    def variant(r):
        return r - jnp.clip(r - kh // 2, 0, rows_n - kh)

    return pl.pallas_call(
        functools.partial(_na_kernel, rows_n=rows_n, kh=kh),
        out_shape=jax.ShapeDtypeStruct((nb * s_lat, w), BF16),
        grid=(nb, rows_n),
        in_specs=[pl.BlockSpec((GRID_W, w), lambda b, r: (row0 // GRID_W + b * rows_n + r, C_NAQ // w)),
                  pl.BlockSpec((s_lat, w), lambda b, r: (row0 // s_lat + b, C_NAK // w)),
                  pl.BlockSpec((s_lat, w), lambda b, r: (row0 // s_lat + b, C_NAV // w)),
                  pl.BlockSpec((None, skc, w), lambda b, r: (b, 0, 0)),
                  pl.BlockSpec((None, skc, w), lambda b, r: (b, 0, 0)),
                  pl.BlockSpec((NA_HEADS, None, GRID_W, kh * GRID_W), lambda b, r: (0, variant(r), 0, 0))],
        out_specs=pl.BlockSpec((GRID_W, w), lambda b, r: (b * rows_n + r, 0)),
        compiler_params=_cparams(("arbitrary", "arbitrary")),
    )(proj, proj, proj, kc, vc, bias)


def _na_bias_tables(rpb, rows_n):
    kh = min(NA_KH, rows_n)
    qc = np.arange(GRID_W)[:, None]
    kc = np.arange(GRID_W)[None, :]
    col_start = np.clip(qc - NA_KW // 2, 0, GRID_W - NA_KW)
    valid = (kc >= col_start) & (kc < col_start + NA_KW)
    coff = np.clip(kc - qc, -(NA_KW - 1), NA_KW - 1) + (NA_KW - 1)
    onehot = (coff.reshape(-1)[None, :] == np.arange(2 * NA_KW - 1)[:, None]).astype(np.float32)
    t = jnp.einsum('hrc,cx->hrx', rpb.astype(F32), jnp.asarray(onehot), precision=lax.Precision.HIGHEST)
    t = t.reshape(NA_HEADS, 2 * NA_KH - 1, GRID_W, GRID_W)
    t = jnp.where(jnp.asarray(valid)[None, None], t, NEG_INF)
    strips = []
    for v in range(kh):
        lo = NA_KH - 1 - v
        s = t[:, lo:lo + kh]
        strips.append(jnp.transpose(s, (0, 2, 1, 3)).reshape(NA_HEADS, GRID_W, kh * GRID_W))
    return jnp.stack(strips, axis=1)


def _ln_epilogue(z, g_ref, b_ref):
    mu = jnp.mean(z, axis=-1, keepdims=True)
    zc = z - mu
    var = jnp.mean(zc * zc, axis=-1, keepdims=True)
    return zc * lax.rsqrt(var + LN_EPS) * g_ref[...] + b_ref[...]


def _router_gates(h, wr_ref):
    logits = _dot(h, wr_ref[...])
    lane = lax.broadcasted_iota(jnp.int32, logits.shape, 1)
    lg = jnp.where(lane < N_EXPERTS, logits, -jnp.inf)
    m1 = jnp.max(lg, axis=-1, keepdims=True)
    i1 = jnp.min(jnp.where(lg == m1, lane, LANE), axis=-1, keepdims=True)
    lg2 = jnp.where(lane == i1, -jnp.inf, lg)
    m2 = jnp.max(lg2, axis=-1, keepdims=True)
    i2 = jnp.min(jnp.where(lg2 == m2, lane, LANE), axis=-1, keepdims=True)
    e2 = jnp.exp(m2 - m1)
    inv = 1.0 / (1.0 + e2)
    return jnp.where(lane == i1, inv, 0.0) + jnp.where(lane == i2, e2 * inv, 0.0)


def _mm_ln_kernel(*refs, n_x, k_total, tk, alpha, with_h, with_router, mask_k):
    it = iter(refs)
    x_refs = [next(it) for _ in range(n_x)]
    w_ref, y_ref, gate_ref, lng_ref, lnb_ref = next(it), next(it), next(it), next(it), next(it)
    sh_ref = next(it) if with_h else None
    sc_ref = next(it) if with_h else None
    wr_ref = next(it) if with_router else None
    yo_ref = next(it)
    h_ref = next(it) if with_h else None
    go_ref = next(it) if with_router else None
    acc_ref = next(it)
    k = pl.program_id(1)
    nk = pl.num_programs(1)
    x = x_refs[0][...] if n_x == 1 else jnp.concatenate([r[...] for r in x_refs], axis=1)
    w = w_ref[...]
    if mask_k:
        lim = k_total - k * tk
        x = jnp.where(lax.broadcasted_iota(jnp.int32, x.shape, 1) < lim, x, jnp.zeros_like(x))
        w = jnp.where(lax.broadcasted_iota(jnp.int32, w.shape, 0) < lim, w, jnp.zeros_like(w))
    part = _dot(x, w.astype(BF16))

    @pl.when(k == 0)
    def _():
        acc_ref[...] = part

    @pl.when(k > 0)
    def _():
        acc_ref[...] += part

    @pl.when(k == nk - 1)
    def _():
        z = alpha * y_ref[...] + gate_ref[...] * acc_ref[...]
        y = _ln_epilogue(z, lng_ref, lnb_ref)
        yo_ref[...] = y
        if with_h:
            h = (y * (1.0 + sc_ref[...]) + sh_ref[...]).astype(BF16)
            h_ref[...] = h
            if with_router:
                go_ref[...] = _router_gates(h, wr_ref)


def _mm_ln(tok, xs, w, y, mods, layer, gate_which, ln_g, ln_b, alpha, tm, tk, h_mod=None, w_router=None):
    m, d = y.shape
    k_total = w.shape[0]
    nk = pl.cdiv(k_total, tk)
    with_h = h_mod is not None
    with_router = w_router is not None
    ins, specs = [], []
    for x in xs:
        wx = x.shape[1] if len(xs) > 1 else tk
        ins.append(x)
        specs.append(pl.BlockSpec((tm, wx), lambda i, k: (i, k)))
    ins += [w, y, mods, ln_g, ln_b]
    specs += [pl.BlockSpec((tk, d), lambda i, k: (k, 0)),
              pl.BlockSpec((tm, d), lambda i, k: (i, 0)),
              tok.mod_spec(layer, gate_which, tm, d, 2),
              pl.BlockSpec((1, d), lambda i, k: (0, 0)),
              pl.BlockSpec((1, d), lambda i, k: (0, 0))]
    outs = [jax.ShapeDtypeStruct((m, d), F32)]
    ospecs = [pl.BlockSpec((tm, d), lambda i, k: (i, 0))]
    if with_h:
        hl, hsh, hsc = h_mod
        ins += [mods, mods]
        specs += [tok.mod_spec(hl, hsh, tm, d, 2), tok.mod_spec(hl, hsc, tm, d, 2)]
        outs.append(jax.ShapeDtypeStruct((m, d), BF16))
        ospecs.append(pl.BlockSpec((tm, d), lambda i, k: (i, 0)))
    if with_router:
        ins.append(w_router)
        specs.append(pl.BlockSpec(w_router.shape, lambda i, k: (0, 0)))
        outs.append(jax.ShapeDtypeStruct((m, LANE), F32))
        ospecs.append(pl.BlockSpec((tm, LANE), lambda i, k: (i, 0)))
    res = pl.pallas_call(
        functools.partial(_mm_ln_kernel, n_x=len(xs), k_total=k_total, tk=tk, alpha=alpha, with_h=with_h,
                          with_router=with_router, mask_k=(k_total % tk != 0)),
        out_shape=outs, grid=(m // tm, nk), in_specs=specs, out_specs=ospecs,
        scratch_shapes=[pltpu.VMEM((tm, d), F32)],
        compiler_params=_cparams(("arbitrary", "arbitrary")),
    )(*ins)
    return res


def _ffn_up_kernel(*refs, with_gate):
    if with_gate:
        x_ref, w1_ref, w3_ref, g_ref, o_ref = refs
    else:
        x_ref, w1_ref, w3_ref, o_ref = refs
    x = x_ref[...]
    a = _silu(_dot(x, w1_ref[...].astype(BF16))) * _dot(x, w3_ref[...].astype(BF16))
    if with_gate:
        a = a * g_ref[...]
    o_ref[...] = a.astype(BF16)


def _ffn_up(x, w1, w3, tm, tf, gates=None):
    m, d = x.shape
    if w1.ndim == 2:
        f = w1.shape[1]
        nf = pl.cdiv(f, tf)
        wspec = pl.BlockSpec((d, tf), lambda j, i: (0, j))
        ins, specs, width = [x, w1, w3], [pl.BlockSpec((tm, d), lambda j, i: (i, 0)), wspec, wspec], f
    else:
        e, _, fe = w1.shape
        npe = fe // tf
        nf = e * npe
        wspec = pl.BlockSpec((None, d, tf), lambda j, i: (j // npe, 0, j % npe))
        ins = [x, w1, w3, gates]
        specs = [pl.BlockSpec((tm, d), lambda j, i: (i, 0)), wspec, wspec,
                 pl.BlockSpec((None, tm, 1), lambda j, i: (j // npe, i, 0))]
        width = e * fe
    return pl.pallas_call(
        functools.partial(_ffn_up_kernel, with_gate=gates is not None),
        out_shape=jax.ShapeDtypeStruct((m, width), BF16),
        grid=(nf, m // tm), in_specs=specs,
        out_specs=pl.BlockSpec((tm, tf), lambda j, i: (i, j)),
        compiler_params=_cparams(("arbitrary", "arbitrary")),
    )(*ins)


def _rope_tables(n_tokens, dim, pad_to):
    t = jnp.arange(n_tokens)
    row = (t // GRID_W).astype(F32)
    col = (t % GRID_W).astype(F32)
    half = dim // 2
    inv_freq = ROPE_THETA ** (-jnp.arange(0, half, 2, dtype=F32) / half)
    ar = row[:, None] * inv_freq[None, :]
    ac = col[:, None] * inv_freq[None, :]
    ang = jnp.concatenate([ar, ar, ac, ac], axis=-1)
    cos, sin = jnp.cos(ang), jnp.sin(ang)
    lo = (np.arange(dim) % (dim // 2)) < dim // 4
    sa = jnp.where(lo[None, :], -sin, 0.0)
    sb = jnp.where(lo[None, :], 0.0, sin)
    return cos, sa, sb


def _pad_lanes(x, width, fill):
    return jnp.concatenate([x, jnp.full((x.shape[0], width - x.shape[1]), fill, x.dtype)], axis=1)


def kernel(x_prompt, x_sample, cache_da_k, cache_da_v, cache_mla_ckv, cache_mla_krope, cache_na_k, cache_na_v, c, c_ctx, w_ada, b_ada, w_in, da_lq1, da_lk1, da_lq2, da_lk2, da_subln, mla_gq, mla_gkv, mla_wuq, mla_wukv, na_rpb, w_out, ln1_g, ln1_b, ln2_g, ln2_b, ffn_w1, ffn_w3, ffn_w2, moe_router, moe_w1, moe_w3, moe_w2):
    nbp, seq, d = x_prompt.shape
    nbs, s_lat, _ = x_sample.shape
    depth = w_in.shape[0]
    past = cache_da_k.shape[2]
    mp, ms = nbp * seq, nbs * s_lat
    m = mp + ms
    tok = _Tok(mp, s_lat, nbs)
    tm = 512
    assert mp % s_lat == 0 and s_lat % tm == 0 and mp % tm == 0 and seq % LANE == 0 and nbs + 1 <= COND_ROWS
    alpha = (2.0 * depth) ** 0.25
    rows_n = s_lat // GRID_W

    cond = jnp.concatenate([c_ctx[None], c, jnp.zeros((COND_ROWS - 1 - nbs, d), F32)], axis=0)
    mods = _ada(cond, w_ada, b_ada).reshape(depth, COND_ROWS, 1, 6 * d)

    cos, sa, sb = _rope_tables(s_lat, DA_QK, LANE)
    rope_d = tuple(jnp.tile(t, (1, 2)) for t in (cos, sa, sb))
    rope_m = (_pad_lanes(cos, LANE, 1.0), _pad_lanes(sa, LANE, 0.0), _pad_lanes(sb, LANE, 0.0))

    y = jnp.concatenate([x_prompt.reshape(mp, d), x_sample.reshape(ms, d)], axis=0)
    h = _modulate(tok, y, mods, 0, tm)

    st = [[] for _ in range(6)]
    for l in range(depth):
        lam_init = 0.8 - 0.6 * math.exp(-0.3 * l)
        wl = w_in[l]
        zpad = jnp.zeros((d, C_NAQ - (C_CKV + MLA_KV_RANK + MLA_ROPE)), F32)
        w_in_p = jnp.concatenate([wl[:, :C_CKV + MLA_KV_RANK + MLA_ROPE], zpad,
                                  wl[:, C_CKV + MLA_KV_RANK + MLA_ROPE:]], axis=1).astype(BF16)
        wuq = mla_wuq[l].reshape(MLA_Q_RANK, MLA_HEADS, MLA_NOPE + MLA_ROPE)
        wuq_p = jnp.concatenate([wuq, jnp.zeros((MLA_Q_RANK, MLA_HEADS, MLA_QK_PAD - MLA_NOPE - MLA_ROPE), F32)],
                                axis=2).reshape(MLA_Q_RANK, MLA_HEADS * MLA_QK_PAD).astype(BF16)
        wukv = mla_wukv[l].reshape(MLA_KV_RANK, MLA_HEADS, MLA_NOPE + MLA_V)
        wk = wukv[:, :, :MLA_NOPE].reshape(MLA_KV_RANK, MLA_HEADS * MLA_NOPE).astype(BF16)
        wv = wukv[:, :, MLA_NOPE:].reshape(MLA_KV_RANK, MLA_HEADS * MLA_V).astype(BF16)
        gq = mla_gq[l].reshape(1, -1)
        gkv = mla_gkv[l].reshape(1, -1)
        lams = (da_lq1[l].reshape(1, -1), da_lk1[l].reshape(1, -1), da_lq2[l].reshape(1, -1), da_lk2[l].reshape(1, -1))
        gsub = da_subln[l].reshape(1, -1)

        proj = _mm(h, w_in_p, 1024 if m % 1024 == 0 else tm, 1024)

        kcat_p, vb_p, ckvn_p, qcat_p = _mla_prep(proj, 0, mp, tm, gkv, wk, wv, gq, wuq_p)
        oa_p = _da(proj, proj, proj, lams, gsub, lam_init, nb=nbp, sq=seq, sk=seq, tq=seq,
                   qcol0=C_DAQ, kcol0=C_DAK, vcol0=C_DAV)
        ob_p = _attn(qcat_p, kcat_p, vb_p, nb=nbp, sq=seq, sk=seq, tq=seq, heads=MLA_HEADS, hp=MLA_HEADS,
                     dk=MLA_QK_PAD, dv=MLA_V, scale=MLA_SCALE)
        oc_p = _attn(proj, proj, proj, nb=nbp, sq=seq, sk=seq, tq=seq, heads=NA_HEADS, hp=NA_HEADS,
                     dk=NA_DIM, dv=NA_DIM, scale=NA_DIM ** -0.5, qcol0=C_NAQ, kcol0=C_NAK, vcol0=C_NAV)

        kcat_s, vb_s, _, qcat_s, qa_s, ka_s = _mla_prep(proj, mp, ms, tm, gkv, wk, wv, gq, wuq_p,
                                                      rope_m=rope_m, rope_d=rope_d, s_lat=s_lat)
        cache_pack = jnp.concatenate(
            [cache_mla_ckv[:, l].reshape(nbs * past, MLA_KV_RANK), cache_mla_krope[:, l].reshape(nbs * past, MLA_ROPE),
             jnp.zeros((nbs * past, 512 - MLA_KV_RANK - MLA_ROPE), F32)], axis=1)
        kcat_c, vb_c, _ = _mla_prep(cache_pack, 0, nbs * past, past, gkv, wk, wv, norm=False)
        tq = 512 if s_lat % 512 == 0 else s_lat
        oa_s = _da(qa_s, ka_s, proj, lams, gsub, lam_init, nb=nbs, sq=s_lat, sk=s_lat, tq=tq,
                   vrow0=mp, vcol0=C_DAV,
                   kc=cache_da_k[:, l].reshape(nbs, past, DA_HEADS * LANE),
                   vc=cache_da_v[:, l].reshape(nbs, past, DA_HEADS * DA_V), skc=past)
        ob_s = _attn(qcat_s, kcat_s, vb_s, nb=nbs, sq=s_lat, sk=s_lat, tq=tq, heads=MLA_HEADS, hp=1,
                     dk=MLA_QK_PAD, dv=MLA_V, scale=MLA_SCALE, kc=kcat_c, vc=vb_c, skc=past)
        bias = _na_bias_tables(na_rpb[l], rows_n)
        oc_s = _na_latent(proj, mp, nbs, s_lat,
                          cache_na_k[:, l].reshape(nbs, past, NA_HEADS * NA_DIM),
                          cache_na_v[:, l].reshape(nbs, past, NA_HEADS * NA_DIM), bias)

        oa = jnp.concatenate([oa_p, oa_s], axis=0)
        ob = jnp.concatenate([ob_p, ob_s], axis=0)
        oc = jnp.concatenate([oc_p, oc_s], axis=0)

        i = l // 2
        moe = (l % 2 == 1)
        wr = None
        if moe:
            wr = _pad_lanes(moe_router[i], LANE, 0.0).astype(BF16)
        res = _mm_ln(tok, [oa, ob, oc], w_out[l].astype(BF16), y, mods, l, 2, ln1_g[l].reshape(1, d),
                     ln1_b[l].reshape(1, d), alpha, tm, d, h_mod=(l, 3, 4), w_router=wr)
        y, h2 = res[0], res[1]

        if not moe:
            a = _ffn_up(h2, ffn_w1[i], ffn_w3[i], 1024 if m % 1024 == 0 else tm, 512)
            w2 = ffn_w2[i]
        else:
            gates = jnp.transpose(res[2][:, :N_EXPERTS])[:, :, None]
            a = _ffn_up(h2, moe_w1[i], moe_w3[i], 1024 if m % 1024 == 0 else tm, 256, gates=gates)
            w2 = moe_w2[i].reshape(-1, d)
        nxt = (l + 1, 0, 1) if l + 1 < depth else None
        res = _mm_ln(tok, [a], w2, y, mods, l, 5, ln2_g[l].reshape(1, d), ln2_b[l].reshape(1, d), alpha,
                     tm, 512, h_mod=nxt)
        y = res[0]
        if nxt is not None:
            h = res[1]

        pp = proj[:mp]
        st[0].append(pp[:, C_DAK:C_DAK + 512].reshape(nbp, seq, DA_HEADS, 2, DA_QK))
        st[1].append(pp[:, C_DAV:C_DAV + 512].reshape(nbp, seq, DA_HEADS, DA_V))
        st[2].append(ckvn_p.reshape(nbp, seq, MLA_KV_RANK))
        st[3].append(pp[:, C_CKV + MLA_KV_RANK:C_CKV + MLA_KV_RANK + MLA_ROPE].reshape(nbp, seq, MLA_ROPE))
        st[4].append(pp[:, C_NAK:C_NAK + 512].reshape(nbp, seq, NA_HEADS, NA_DIM))
        st[5].append(pp[:, C_NAV:C_NAV + 512].reshape(nbp, seq, NA_HEADS, NA_DIM))

    y_p = y[:mp].reshape(nbp, seq, d)
    y_s = y[mp:].reshape(nbs, s_lat, d)
    return (y_p, y_s) + tuple(jnp.stack(s, axis=1) for s in st)
```

```python
import functools
import math

import numpy as np
import jax
import jax.numpy as jnp
from jax import lax
from jax.experimental import pallas as pl
from jax.experimental.pallas import tpu as pltpu

F32 = jnp.float32
BF16 = jnp.bfloat16

GRID_W = 64
DA_QK = 64
DA_V = 128
DA_HEADS = 4
MLA_Q_RANK = 512
MLA_KV_RANK = 256
MLA_NOPE = 128
MLA_ROPE = 64
MLA_V = 128
MLA_HEADS = 8
MLA_SCALE = (MLA_NOPE + MLA_ROPE) ** -0.5
NA_DIM = 128
NA_HEADS = 4
NA_KH = 8
NA_KW = 16
N_EXPERTS = 8
ROPE_THETA = 10000.0
LN_EPS = 1e-5
RMS_EPS = 1e-6
NEG_INF = -1e30
LANE = 128
COND_ROWS = 8
VMEM_LIMIT = 56 * 1024 * 1024

C_DAQ, C_DAK, C_DAV, C_CQ, C_CKV, C_NAQ, C_NAK, C_NAV = 0, 512, 1024, 1536, 2048, 2560, 3072, 3584
P_COLS = 4096
MLA_QK_PAD = 256


def _cparams(sem):
    return pltpu.CompilerParams(dimension_semantics=sem, vmem_limit_bytes=VMEM_LIMIT)


def _dot(a, b):
    return jnp.dot(a, b, preferred_element_type=F32)


def _dot_nt(a, b):
    return lax.dot_general(a, b, (((1,), (1,)), ((), ())), preferred_element_type=F32)


def _silu(x):
    return x * (1.0 / (1.0 + jnp.exp(-x)))


def _rms(x, g):
    return x * lax.rsqrt(jnp.mean(x * x, axis=-1, keepdims=True) + RMS_EPS) * g


def _rope(x, cos, sa, sb):
    return x * cos + pltpu.roll(x, LANE - 16, 1) * sa + pltpu.roll(x, 16, 1) * sb


def _ada_kernel(c_ref, w_ref, b_ref, o_ref):
    s = _silu(c_ref[...]).astype(BF16)
    o_ref[...] = _dot(s, w_ref[...].astype(BF16)) + b_ref[...]


def _ada(cond, w_ada, b_ada):
    n_layers, d, n = w_ada.shape
    tn = 1024
    return pl.pallas_call(
        _ada_kernel,
        out_shape=jax.ShapeDtypeStruct((n_layers, COND_ROWS, n), F32),
        grid=(n_layers, n // tn),
        in_specs=[pl.BlockSpec((COND_ROWS, d), lambda l, j: (0, 0)),
                  pl.BlockSpec((None, d, tn), lambda l, j: (l, 0, j)),
                  pl.BlockSpec((None, 1, tn), lambda l, j: (l, 0, j))],
        out_specs=pl.BlockSpec((None, COND_ROWS, tn), lambda l, j: (l, 0, j)),
        compiler_params=_cparams(("arbitrary", "arbitrary")), name="ada",
    )(cond, w_ada, b_ada.reshape(n_layers, 1, n))


class _Tok:
    def __init__(self, mp, s_lat, n_lat_batches):
        self.mp, self.s_lat, self.nb = mp, s_lat, n_lat_batches
        self.m = mp + s_lat * n_lat_batches

    def rid(self, i, tm):
        r0 = i * tm
        return jnp.where(r0 < self.mp, 0, 1 + (r0 - self.mp) // self.s_lat)

    def mod_spec(self, layer, which, tm, d, grid_rank=1):
        if grid_rank == 1:
            return pl.BlockSpec((None, None, 1, d), lambda i: (layer, self.rid(i, tm), 0, which))
        return pl.BlockSpec((None, None, 1, d), lambda i, k: (layer, self.rid(i, tm), 0, which))


def _modulate_kernel(x_ref, sh_ref, sc_ref, o_ref):
    o_ref[...] = (x_ref[...] * (1.0 + sc_ref[...]) + sh_ref[...]).astype(BF16)


def _modulate(tok, y, mods, layer, tm):
    m, d = y.shape
    return pl.pallas_call(
        _modulate_kernel,
        out_shape=jax.ShapeDtypeStruct((m, d), BF16),
        grid=(m // tm,),
        in_specs=[pl.BlockSpec((tm, d), lambda i: (i, 0)),
                  tok.mod_spec(layer, 0, tm, d), tok.mod_spec(layer, 1, tm, d)],
        out_specs=pl.BlockSpec((tm, d), lambda i: (i, 0)),
        compiler_params=_cparams(("arbitrary",)), name="modulate",
    )(y, mods, mods)


def _mm_kernel(x_ref, w_ref, o_ref):
    o_ref[...] = _dot(x_ref[...], w_ref[...])


def _mm(x, w, tm, tn):
    m, k = x.shape
    n = w.shape[1]
    return pl.pallas_call(
        _mm_kernel,
        out_shape=jax.ShapeDtypeStruct((m, n), F32),
        grid=(n // tn, m // tm),
        in_specs=[pl.BlockSpec((tm, k), lambda j, i: (i, 0)),
                  pl.BlockSpec((k, tn), lambda j, i: (0, j))],
        out_specs=pl.BlockSpec((tm, tn), lambda j, i: (i, j)),
        compiler_params=_cparams(("arbitrary", "arbitrary")), name="in_proj",
    )(x, w)


def _mla_prep_kernel(*refs, norm, with_q, rope, da_rope):
    it = iter(refs)
    ckv_ref = next(it)
    cq_ref = next(it) if with_q else None
    daq_ref = next(it) if da_rope else None
    dak_ref = next(it) if da_rope else None
    gkv_ref, wk_ref, wv_ref = next(it), next(it), next(it)
    gq_ref = next(it) if with_q else None
    wuq_ref = next(it) if with_q else None
    if rope:
        cm_ref, sam_ref, sbm_ref = next(it), next(it), next(it)
    if da_rope:
        cd_ref, sad_ref, sbd_ref = next(it), next(it), next(it)
    kcat_ref, vb_ref, ckvn_ref = next(it), next(it), next(it)
    qcat_ref = next(it) if with_q else None
    qa_ref = next(it) if da_rope else None
    ka_ref = next(it) if da_rope else None

    ckv = ckv_ref[:, :MLA_KV_RANK]
    ckvn = _rms(ckv, gkv_ref[...]) if norm else ckv
    ckvn_ref[...] = ckvn
    kr = ckv_ref[:, MLA_KV_RANK:MLA_KV_RANK + LANE]
    if rope:
        kr = _rope(kr, cm_ref[...], sam_ref[...], sbm_ref[...])
    kr = kr.astype(BF16)
    cb = ckvn.astype(BF16)
    kn = _dot(cb, wk_ref[...]).astype(BF16)
    vb_ref[...] = _dot(cb, wv_ref[...]).astype(BF16)
    for h in range(MLA_HEADS):
        kcat_ref[:, h * MLA_QK_PAD:h * MLA_QK_PAD + MLA_NOPE] = kn[:, h * MLA_NOPE:(h + 1) * MLA_NOPE]
        kcat_ref[:, h * MLA_QK_PAD + MLA_NOPE:(h + 1) * MLA_QK_PAD] = kr
    if with_q:
        cqn = _rms(cq_ref[...], gq_ref[...]).astype(BF16)
        q = _dot(cqn, wuq_ref[...])
        for h in range(MLA_HEADS):
            lo = h * MLA_QK_PAD
            qcat_ref[:, lo:lo + MLA_NOPE] = q[:, lo:lo + MLA_NOPE].astype(BF16)
            qr = q[:, lo + MLA_NOPE:lo + MLA_QK_PAD]
            if rope:
                qr = _rope(qr, cm_ref[...], sam_ref[...], sbm_ref[...])
            qcat_ref[:, lo + MLA_NOPE:lo + MLA_QK_PAD] = qr.astype(BF16)
    if da_rope:
        for h in range(DA_HEADS):
            sl = slice(h * LANE, (h + 1) * LANE)
            qa_ref[:, sl] = _rope(daq_ref[:, sl], cd_ref[...], sad_ref[...], sbd_ref[...]).astype(BF16)
            ka_ref[:, sl] = _rope(dak_ref[:, sl], cd_ref[...], sad_ref[...], sbd_ref[...]).astype(BF16)


def _mla_prep(src, row0, rows, tm, gkv, wk, wv, gq=None, wuq=None, rope_m=None, rope_d=None,
              norm=True, s_lat=None):
    with_q = wuq is not None
    rope = rope_m is not None
    da_rope = rope_d is not None
    b0 = row0 // tm
    n = rows // tm
    ckv_blk = C_CKV // 512 if src.shape[1] == P_COLS else 0
    const = lambda shape: pl.BlockSpec(shape, lambda i: (0,) * len(shape))
    ins, specs = [src], [pl.BlockSpec((tm, 512), lambda i: (b0 + i, ckv_blk))]
    if with_q:
        ins.append(src)
        specs.append(pl.BlockSpec((tm, 512), lambda i: (b0 + i, C_CQ // 512)))
    if da_rope:
        ins += [src, src]
        specs += [pl.BlockSpec((tm, 512), lambda i: (b0 + i, C_DAQ // 512)),
                  pl.BlockSpec((tm, 512), lambda i: (b0 + i, C_DAK // 512))]
    ins += [gkv, wk, wv]
    specs += [const(gkv.shape), const(wk.shape), const(wv.shape)]
    if with_q:
        ins += [gq, wuq]
        specs += [const(gq.shape), const(wuq.shape)]
    nt = (s_lat // tm) if rope else 1
    if rope:
        ins += list(rope_m)
        specs += [pl.BlockSpec((tm, LANE), lambda i: (i % nt, 0))] * 3
    if da_rope:
        ins += list(rope_d)
        specs += [pl.BlockSpec((tm, LANE), lambda i: (i % nt, 0))] * 3
    outs = [jax.ShapeDtypeStruct((rows, MLA_HEADS * MLA_QK_PAD), BF16),
            jax.ShapeDtypeStruct((rows, MLA_HEADS * MLA_V), BF16),
            jax.ShapeDtypeStruct((rows, MLA_KV_RANK), F32)]
    if with_q:
        outs.append(jax.ShapeDtypeStruct((rows, MLA_HEADS * MLA_QK_PAD), BF16))
    if da_rope:
        outs += [jax.ShapeDtypeStruct((rows, DA_HEADS * LANE), BF16)] * 2
    ospecs = [pl.BlockSpec((tm, o.shape[1]), lambda i: (i, 0)) for o in outs]
    return pl.pallas_call(
        functools.partial(_mla_prep_kernel, norm=norm, with_q=with_q, rope=rope, da_rope=da_rope),
        out_shape=outs, grid=(n,), in_specs=specs, out_specs=ospecs,
        compiler_params=_cparams(("arbitrary",)), name="mla_prep",
    )(*ins)


def _softmax2(s, s2):
    m = jnp.max(s, axis=-1, keepdims=True)
    if s2 is not None:
        m = jnp.maximum(m, jnp.max(s2, axis=-1, keepdims=True))
    e = jnp.exp(s - m)
    den = jnp.sum(e, axis=-1, keepdims=True)
    e2 = None
    if s2 is not None:
        e2 = jnp.exp(s2 - m)
        den = den + jnp.sum(e2, axis=-1, keepdims=True)
    inv = 1.0 / den
    return e * inv, (None if e2 is None else e2 * inv)


def _attn_kernel(*refs, heads, dk, dv, scale, has_ctx):
    if has_ctx:
        q_ref, k_ref, v_ref, kc_ref, vc_ref, o_ref = refs
    else:
        q_ref, k_ref, v_ref, o_ref = refs
    for h in range(heads):
        q = q_ref[:, h * dk:(h + 1) * dk].astype(BF16)
        s = _dot_nt(q, k_ref[:, h * dk:(h + 1) * dk].astype(BF16)) * scale
        s2 = None
        if has_ctx:
            s2 = _dot_nt(q, kc_ref[:, h * dk:(h + 1) * dk].astype(BF16)) * scale
        p, p2 = _softmax2(s, s2)
        o = _dot(p.astype(BF16), v_ref[:, h * dv:(h + 1) * dv].astype(BF16))
        if has_ctx:
            o = o + _dot(p2.astype(BF16), vc_ref[:, h * dv:(h + 1) * dv].astype(BF16))
        o_ref[:, h * dv:(h + 1) * dv] = o.astype(BF16)


def _attn(q, k, v, *, nb, sq, sk, tq, heads, hp, dk, dv, scale, qrow0=0, krow0=0, qcol0=0, kcol0=0, vcol0=0,
          kc=None, vc=None, skc=0):
    nq = sq // tq
    ng = heads // hp
    wq, wv = hp * dk, hp * dv
    has_ctx = kc is not None
    ins = [q, k, v]
    specs = [pl.BlockSpec((tq, wq), lambda b, g, i: (qrow0 // tq + b * nq + i, qcol0 // wq + g)),
             pl.BlockSpec((sk, wq), lambda b, g, i: (krow0 // sk + b, kcol0 // wq + g)),
             pl.BlockSpec((sk, wv), lambda b, g, i: (krow0 // sk + b, vcol0 // wv + g))]
    if has_ctx:
        ins += [kc, vc]
        specs += [pl.BlockSpec((skc, wq), lambda b, g, i: (b, g)),
                  pl.BlockSpec((skc, wv), lambda b, g, i: (b, g))]
    return pl.pallas_call(
        functools.partial(_attn_kernel, heads=hp, dk=dk, dv=dv, scale=scale, has_ctx=has_ctx),
        out_shape=jax.ShapeDtypeStruct((nb * sq, heads * dv), BF16),
        grid=(nb, ng, nq), in_specs=specs,
        out_specs=pl.BlockSpec((tq, wv), lambda b, g, i: (b * nq + i, g)),
        compiler_params=_cparams(("arbitrary", "arbitrary", "arbitrary")), name="attn_h%d" % heads,
    )(*ins)


def _da_kernel(*refs, has_ctx, lam_init):
    if has_ctx:
        q_ref, k_ref, v_ref, kc_ref, vc_ref, lq1, lk1, lq2, lk2, g_ref, o_ref = refs
    else:
        q_ref, k_ref, v_ref, lq1, lk1, lq2, lk2, g_ref, o_ref = refs
    lam = (jnp.exp(jnp.sum(lq1[...] * lk1[...], axis=-1, keepdims=True))
           - jnp.exp(jnp.sum(lq2[...] * lk2[...], axis=-1, keepdims=True)) + lam_init)
    first = lax.broadcasted_iota(jnp.int32, (1, LANE), 1) < DA_QK
    scale = DA_QK ** -0.5
    for h in range(DA_HEADS):
        sl = slice(h * LANE, (h + 1) * LANE)
        q = q_ref[:, sl].astype(F32) * scale
        qs = (jnp.where(first, q, 0.0).astype(BF16), jnp.where(first, 0.0, q).astype(BF16))
        k = k_ref[:, sl].astype(BF16)
        kc = kc_ref[:, sl].astype(BF16) if has_ctx else None
        ps = []
        for c in range(2):
            s = _dot_nt(qs[c], k)
            s2 = _dot_nt(qs[c], kc) if has_ctx else None
            ps.append(_softmax2(s, s2))
        p = (ps[0][0] - lam * ps[1][0]).astype(BF16)
        o = _dot(p, v_ref[:, sl].astype(BF16))
        if has_ctx:
            p2 = (ps[0][1] - lam * ps[1][1]).astype(BF16)
            o = o + _dot(p2, vc_ref[:, sl].astype(BF16))
        o = _rms(o, g_ref[...]) * (1.0 - lam_init)
        o_ref[:, sl] = o.astype(BF16)


def _da(q, k, v, lams, g, lam_init, *, nb, sq, sk, tq, qrow0=0, krow0=0, qcol0=0, kcol0=0, vcol0=0,
        vrow0=None, kc=None, vc=None, skc=0):
    nq = sq // tq
    w = DA_HEADS * LANE
    vrow0 = krow0 if vrow0 is None else vrow0
    has_ctx = kc is not None
    ins = [q, k, v]
    specs = [pl.BlockSpec((tq, w), lambda b, i: (qrow0 // tq + b * nq + i, qcol0 // w)),
             pl.BlockSpec((sk, w), lambda b, i: (krow0 // sk + b, kcol0 // w)),
             pl.BlockSpec((sk, w), lambda b, i: (vrow0 // sk + b, vcol0 // w))]
    if has_ctx:
        ins += [kc, vc]
        specs += [pl.BlockSpec((None, skc, w), lambda b, i: (b, 0, 0))] * 2
    ins += list(lams) + [g]
    specs += [pl.BlockSpec((1, DA_QK), lambda b, i: (0, 0))] * 4 + [pl.BlockSpec((1, DA_V), lambda b, i: (0, 0))]
    return pl.pallas_call(
        functools.partial(_da_kernel, has_ctx=has_ctx, lam_init=lam_init),
        out_shape=jax.ShapeDtypeStruct((nb * sq, w), BF16),
        grid=(nb, nq), in_specs=specs,
        out_specs=pl.BlockSpec((tq, w), lambda b, i: (b * nq + i, 0)),
        compiler_params=_cparams(("arbitrary", "arbitrary")), name="diff_attn",
    )(*ins)


def _na_kernel(q_ref, k_ref, v_ref, kc_ref, vc_ref, bias_ref, o_ref, *, rows_n, kh):
    r = pl.program_id(1)
    rs = jnp.clip(r - kh // 2, 0, rows_n - kh)
    start = pl.multiple_of(rs * GRID_W, GRID_W)
    nwin = kh * GRID_W
    scale = NA_DIM ** -0.5
    for h in range(NA_HEADS):
        sl = slice(h * NA_DIM, (h + 1) * NA_DIM)
        q = q_ref[:, sl].astype(BF16)
        kw = k_ref[pl.ds(start, nwin), sl].astype(BF16)
        vw = v_ref[pl.ds(start, nwin), sl].astype(BF16)
        s = _dot_nt(q, kw) * scale + bias_ref[h]
        s2 = _dot_nt(q, kc_ref[:, sl].astype(BF16)) * scale
        p, p2 = _softmax2(s, s2)
        o = _dot(p.astype(BF16), vw) + _dot(p2.astype(BF16), vc_ref[:, sl].astype(BF16))
        o_ref[:, sl] = o.astype(BF16)


def _na_latent(proj, row0, nb, s_lat, kc, vc, bias):
    rows_n = s_lat // GRID_W
    kh = min(NA_KH, rows_n)
    w = NA_HEADS * NA_DIM
    skc = kc.shape[1]

    def variant(r):
        return r - jnp.clip(r - kh // 2, 0, rows_n - kh)

    return pl.pallas_call(
        functools.partial(_na_kernel, rows_n=rows_n, kh=kh),
        out_shape=jax.ShapeDtypeStruct((nb * s_lat, w), BF16),
        grid=(nb, rows_n),
        in_specs=[pl.BlockSpec((GRID_W, w), lambda b, r: (row0 // GRID_W + b * rows_n + r, C_NAQ // w)),
                  pl.BlockSpec((s_lat, w), lambda b, r: (row0 // s_lat + b, C_NAK // w)),
                  pl.BlockSpec((s_lat, w), lambda b, r: (row0 // s_lat + b, C_NAV // w)),
                  pl.BlockSpec((None, skc, w), lambda b, r: (b, 0, 0)),
                  pl.BlockSpec((None, skc, w), lambda b, r: (b, 0, 0)),
                  pl.BlockSpec((NA_HEADS, None, GRID_W, kh * GRID_W), lambda b, r: (0, variant(r), 0, 0))],
        out_specs=pl.BlockSpec((GRID_W, w), lambda b, r: (b * rows_n + r, 0)),
        compiler_params=_cparams(("arbitrary", "arbitrary")), name="na_latent",
    )(proj, proj, proj, kc, vc, bias)


def _na_bias_tables(rpb, rows_n):
    kh = min(NA_KH, rows_n)
    qc = np.arange(GRID_W)[:, None]
    kc = np.arange(GRID_W)[None, :]
    col_start = np.clip(qc - NA_KW // 2, 0, GRID_W - NA_KW)
    valid = (kc >= col_start) & (kc < col_start + NA_KW)
    coff = np.clip(kc - qc, -(NA_KW - 1), NA_KW - 1) + (NA_KW - 1)
    onehot = (coff.reshape(-1)[None, :] == np.arange(2 * NA_KW - 1)[:, None]).astype(np.float32)
    t = jnp.einsum('hrc,cx->hrx', rpb.astype(F32), jnp.asarray(onehot), precision=lax.Precision.HIGHEST)
    t = t.reshape(NA_HEADS, 2 * NA_KH - 1, GRID_W, GRID_W)
    t = jnp.where(jnp.asarray(valid)[None, None], t, NEG_INF)
    strips = []
    for v in range(kh):
        lo = NA_KH - 1 - v
        s = t[:, lo:lo + kh]
        strips.append(jnp.transpose(s, (0, 2, 1, 3)).reshape(NA_HEADS, GRID_W, kh * GRID_W))
    return jnp.stack(strips, axis=1)


def _ln_epilogue(z, g_ref, b_ref):
    mu = jnp.mean(z, axis=-1, keepdims=True)
    zc = z - mu
    var = jnp.mean(zc * zc, axis=-1, keepdims=True)
    return zc * lax.rsqrt(var + LN_EPS) * g_ref[...] + b_ref[...]


def _router_info(h, wr_ref, cnt_ref):
    n = h.shape[0]
    logits = _dot(h, wr_ref[...])
    lane = lax.broadcasted_iota(jnp.int32, logits.shape, 1)
    lg = jnp.where(lane < N_EXPERTS, logits, -jnp.inf)
    m1 = jnp.max(lg, axis=-1, keepdims=True)
    i1 = jnp.min(jnp.where(lg == m1, lane, LANE), axis=-1, keepdims=True)
    lg2 = jnp.where(lane == i1, -jnp.inf, lg)
    m2 = jnp.max(lg2, axis=-1, keepdims=True)
    i2 = jnp.min(jnp.where(lg2 == m2, lane, LANE), axis=-1, keepdims=True)
    e2 = jnp.exp(m2 - m1)
    inv = 1.0 / (1.0 + e2)
    oh1, oh2 = lane == i1, lane == i2
    o1, o2 = jnp.where(oh1, 1.0, 0.0), jnp.where(oh2, 1.0, 0.0)
    below = lax.broadcasted_iota(jnp.int32, (n, n), 1) < lax.broadcasted_iota(jnp.int32, (n, n), 0)
    tri = jnp.where(below, 1.0, 0.0).astype(BF16)
    p1 = _dot(tri, o1.astype(BF16))
    p2 = _dot(tri, o2.astype(BF16))
    tot1 = jnp.sum(o1, axis=0, keepdims=True)
    tot2 = jnp.sum(o2, axis=0, keepdims=True)
    cnt = cnt_ref[...]
    rank0 = jnp.sum(jnp.where(oh1, cnt + p1, 0.0), axis=-1, keepdims=True)
    rank1 = jnp.sum(jnp.where(oh2, cnt + tot1 + p2, 0.0), axis=-1, keepdims=True)
    cnt_ref[...] = cnt + tot1 + tot2
    cols = (i1.astype(F32), i2.astype(F32), inv, e2 * inv, rank0, rank1)
    info = jnp.zeros(logits.shape, F32)
    for c, v in enumerate(cols):
        info = jnp.where(lane == c, v, info)
    return info


def _mm_ln_kernel(*refs, n_x, k_total, tk, alpha, with_h, with_router, mask_k):
    it = iter(refs)
    x_refs = [next(it) for _ in range(n_x)]
    w_ref, y_ref, gate_ref, lng_ref, lnb_ref = next(it), next(it), next(it), next(it), next(it)
    sh_ref = next(it) if with_h else None
    sc_ref = next(it) if with_h else None
    wr_ref = next(it) if with_router else None
    yo_ref = next(it)
    h_ref = next(it) if with_h else None
    go_ref = next(it) if with_router else None
    ca_ref = next(it) if with_router else None
    acc_ref = next(it)
    cnt_ref = next(it) if with_router else None
    k = pl.program_id(1)
    nk = pl.num_programs(1)
    if with_router:
        @pl.when((pl.program_id(0) == 0) & (k == 0))
        def _():
            cnt_ref[...] = jnp.zeros_like(cnt_ref)
    x = x_refs[0][...] if n_x == 1 else jnp.concatenate([r[...] for r in x_refs], axis=1)
    w = w_ref[...]
    if mask_k:
        lim = k_total - k * tk
        x = jnp.where(lax.broadcasted_iota(jnp.int32, x.shape, 1) < lim, x, jnp.zeros_like(x))
        w = jnp.where(lax.broadcasted_iota(jnp.int32, w.shape, 0) < lim, w, jnp.zeros_like(w))
    part = _dot(x, w.astype(BF16))

    @pl.when(k == 0)
    def _():
        acc_ref[...] = part

    @pl.when(k > 0)
    def _():
        acc_ref[...] += part

    @pl.when(k == nk - 1)
    def _():
        z = alpha * y_ref[...] + gate_ref[...] * acc_ref[...]
        y = _ln_epilogue(z, lng_ref, lnb_ref)
        yo_ref[...] = y
        if with_h:
            h = (y * (1.0 + sc_ref[...]) + sh_ref[...]).astype(BF16)
            h_ref[...] = h
            if with_router:
                go_ref[...] = _router_info(h, wr_ref, cnt_ref)
                ca_ref[...] = cnt_ref[...]


def _mm_ln(tok, xs, w, y, mods, layer, gate_which, ln_g, ln_b, alpha, tm, tk, h_mod=None, w_router=None,
           name="mm_ln"):
    m, d = y.shape
    k_total = w.shape[0]
    nk = pl.cdiv(k_total, tk)
    with_h = h_mod is not None
    with_router = w_router is not None
    ins, specs = [], []
    for x in xs:
        wx = x.shape[1] if len(xs) > 1 else tk
        ins.append(x)
        specs.append(pl.BlockSpec((tm, wx), lambda i, k: (i, k)))
    ins += [w, y, mods, ln_g, ln_b]
    specs += [pl.BlockSpec((tk, d), lambda i, k: (k, 0)),
              pl.BlockSpec((tm, d), lambda i, k: (i, 0)),
              tok.mod_spec(layer, gate_which, tm, d, 2),
              pl.BlockSpec((1, d), lambda i, k: (0, 0)),
              pl.BlockSpec((1, d), lambda i, k: (0, 0))]
    outs = [jax.ShapeDtypeStruct((m, d), F32)]
    ospecs = [pl.BlockSpec((tm, d), lambda i, k: (i, 0))]
    if with_h:
        hl, hsh, hsc = h_mod
        ins += [mods, mods]
        specs += [tok.mod_spec(hl, hsh, tm, d, 2), tok.mod_spec(hl, hsc, tm, d, 2)]
        outs.append(jax.ShapeDtypeStruct((m, d), BF16))
        ospecs.append(pl.BlockSpec((tm, d), lambda i, k: (i, 0)))
    if with_router:
        ins.append(w_router)
        specs.append(pl.BlockSpec(w_router.shape, lambda i, k: (0, 0)))
        outs += [jax.ShapeDtypeStruct((m, LANE), F32), jax.ShapeDtypeStruct((m // tm, 1, LANE), F32)]
        ospecs += [pl.BlockSpec((tm, LANE), lambda i, k: (i, 0)),
                   pl.BlockSpec((None, 1, LANE), lambda i, k: (i, 0, 0))]
    scratch = [pltpu.VMEM((tm, d), F32)]
    if with_router:
        scratch.append(pltpu.VMEM((1, LANE), F32))
    res = pl.pallas_call(
        functools.partial(_mm_ln_kernel, n_x=len(xs), k_total=k_total, tk=tk, alpha=alpha, with_h=with_h,
                          with_router=with_router, mask_k=(k_total % tk != 0)),
        out_shape=outs, grid=(m // tm, nk), in_specs=specs, out_specs=ospecs,
        scratch_shapes=scratch,
        compiler_params=_cparams(("arbitrary", "arbitrary")), name=name,
    )(*ins)
    return res


def _ffn_up_kernel(x_ref, w1_ref, w3_ref, o_ref):
    x = x_ref[...]
    a = _silu(_dot(x, w1_ref[...].astype(BF16))) * _dot(x, w3_ref[...].astype(BF16))
    o_ref[...] = a.astype(BF16)


def _ffn_up(x, w1, w3, tm, tf):
    m, d = x.shape
    f = w1.shape[1]
    wspec = pl.BlockSpec((d, tf), lambda j, i: (0, j))
    return pl.pallas_call(
        _ffn_up_kernel,
        out_shape=jax.ShapeDtypeStruct((m, f), BF16),
        grid=(pl.cdiv(f, tf), m // tm),
        in_specs=[pl.BlockSpec((tm, d), lambda j, i: (i, 0)), wspec, wspec],
        out_specs=pl.BlockSpec((tm, tf), lambda j, i: (i, j)),
        compiler_params=_cparams(("arbitrary", "arbitrary")), name="ffn_up",
    )(x, w1, w3)


MOE_ROWS = 512


def _moe_plan(info, c_after, m, mc):
    tr = MOE_ROWS
    e_n = N_EXPERTS
    n_chunks = m // mc
    n_tiles = (2 * m) // tr + e_n
    maxp = n_tiles + e_n * n_chunks
    i32 = jnp.int32
    i1, i2 = info[:, 0].astype(i32), info[:, 1].astype(i32)
    r0, r1 = info[:, 4].astype(i32), info[:, 5].astype(i32)
    ca = c_after[:, 0, :e_n].astype(i32)
    cb = jnp.concatenate([jnp.zeros((1, e_n), i32), ca[:-1]], axis=0)
    counts = ca[-1]
    padded = ((counts + tr - 1) // tr) * tr
    start = jnp.cumsum(padded) - padded
    eid = jnp.arange(e_n, dtype=i32)

    def pick(idx, table):
        return jnp.sum(jnp.where(idx[:, None] == eid[None, :], table[None, :], 0), axis=1)

    pos0 = pick(i1, start) + r0
    pos1 = pick(i2, start) + r1
    row0 = jnp.arange(n_tiles, dtype=i32) * tr
    te = jnp.minimum(jnp.sum((row0[:, None] >= (start + padded)[None, :]).astype(i32), axis=1), e_n - 1)
    tv = row0 < jnp.sum(padded)
    k0 = row0 - pick(te, start)
    k1 = jnp.minimum(k0 + tr, pick(te, counts))
    sel = (te[:, None] == eid[None, :])
    cb_t = jnp.sum(jnp.where(sel[:, None, :], cb[None], 0), axis=2)
    ca_t = jnp.sum(jnp.where(sel[:, None, :], ca[None], 0), axis=2)
    ov = tv[:, None] & (cb_t < k1[:, None]) & (ca_t > k0[:, None])
    first_chunk = (jnp.arange(n_chunks) == 0)[None, :]
    ov_g = ov | (~jnp.any(ov, axis=1, keepdims=True) & first_chunk)

    def pairs(mask2d, inner):
        flat = mask2d.reshape(-1)
        n = jnp.sum(flat.astype(i32))
        idx = jnp.nonzero(flat, size=maxp, fill_value=0)[0].astype(i32)
        p = jnp.arange(maxp, dtype=i32)
        valid = p < n
        idx = jnp.where(valid, idx, jnp.max(jnp.where(valid, idx, 0)))
        outer, inn = idx // inner, idx % inner
        prev = jnp.concatenate([jnp.full((1,), -1, i32), outer[:-1]])
        nxt = jnp.concatenate([outer[1:], jnp.full((1,), -1, i32)])
        first = valid & (outer != prev)
        last = valid & ((outer != nxt) | (p == n - 1))
        return outer, inn, first.astype(i32), last.astype(i32), valid.astype(i32)

    g_tile, g_chunk, g_first, _, g_valid = pairs(ov_g, n_chunks)
    c_chunk, c_tile, c_first, c_last, c_valid = pairs(ov.T, n_tiles)
    return dict(pos0=pos0, pos1=pos1, g1=info[:, 2], g2=info[:, 3], te=te, tv=tv.astype(i32),
                gather=(g_tile, g_chunk, g_first, g_valid),
                combine=(c_tile, c_chunk, c_first, c_last, c_valid), n_tiles=n_tiles, maxp=maxp)


def _moe_gather_kernel(pt, pc, pf, pv, h_ref, p0_ref, p1_ref, g0_ref, g1_ref, xs_ref, gr_ref):
    p = pl.program_id(0)
    tr, mc = xs_ref.shape[0], h_ref.shape[0]

    @pl.when(pf[p] == 1)
    def _():
        xs_ref[...] = jnp.zeros_like(xs_ref)
        gr_ref[...] = jnp.zeros_like(gr_ref)

    @pl.when(pv[p] == 1)
    def _():
        rows = pt[p] * tr + lax.broadcasted_iota(jnp.int32, (tr, mc), 0)
        m0 = p0_ref[...] == rows
        m1 = p1_ref[...] == rows
        sel = jnp.where(m0 | m1, 1.0, 0.0).astype(BF16)
        xs_ref[...] = (xs_ref[...].astype(F32) + _dot(sel, h_ref[...])).astype(BF16)
        gr_ref[...] += jnp.sum(jnp.where(m0, g0_ref[...], 0.0) + jnp.where(m1, g1_ref[...], 0.0),
                               axis=1, keepdims=True)


def _moe_gather(h, plan, mc):
    m, d = h.shape
    tr = MOE_ROWS
    rows = plan["n_tiles"] * tr
    row = lambda a: a.reshape(1, m)
    tok_spec = lambda: pl.BlockSpec((1, mc), lambda p, pt, pc, pf, pv: (0, pc[p]))
    return pl.pallas_call(
        _moe_gather_kernel,
        out_shape=[jax.ShapeDtypeStruct((rows, d), BF16), jax.ShapeDtypeStruct((rows, 1), F32)],
        grid_spec=pltpu.PrefetchScalarGridSpec(
            num_scalar_prefetch=4, grid=(plan["maxp"],),
            in_specs=[pl.BlockSpec((mc, d), lambda p, pt, pc, pf, pv: (pc[p], 0)),
                      tok_spec(), tok_spec(), tok_spec(), tok_spec()],
            out_specs=[pl.BlockSpec((tr, d), lambda p, pt, pc, pf, pv: (pt[p], 0)),
                       pl.BlockSpec((tr, 1), lambda p, pt, pc, pf, pv: (pt[p], 0))]),
        compiler_params=_cparams(("arbitrary",)), name="moe_gather",
    )(*plan["gather"], h, row(plan["pos0"]), row(plan["pos1"]), row(plan["g1"]), row(plan["g2"]))


def _moe_up_kernel(te, tv, x_ref, w1_ref, w3_ref, g_ref, o_ref):
    i = pl.program_id(1)

    @pl.when(tv[i] == 1)
    def _():
        x = x_ref[...]
        a = _silu(_dot(x, w1_ref[...].astype(BF16))) * _dot(x, w3_ref[...].astype(BF16))
        o_ref[...] = (a * g_ref[...]).astype(BF16)

    @pl.when(tv[i] == 0)
    def _():
        o_ref[...] = jnp.zeros_like(o_ref)


def _moe_up(xs, grow, w1, w3, plan, tf):
    rows, d = xs.shape
    tr = MOE_ROWS
    fe = w1.shape[2]
    wspec = pl.BlockSpec((None, d, tf), lambda j, i, te, tv: (te[i], 0, j))
    return pl.pallas_call(
        _moe_up_kernel,
        out_shape=jax.ShapeDtypeStruct((rows, fe), BF16),
        grid_spec=pltpu.PrefetchScalarGridSpec(
            num_scalar_prefetch=2, grid=(fe // tf, rows // tr),
            in_specs=[pl.BlockSpec((tr, d), lambda j, i, te, tv: (i, 0)), wspec, wspec,
                      pl.BlockSpec((tr, 1), lambda j, i, te, tv: (i, 0))],
            out_specs=pl.BlockSpec((tr, tf), lambda j, i, te, tv: (i, j))),
        compiler_params=_cparams(("arbitrary", "arbitrary")), name="moe_up",
    )(plan["te"], plan["tv"], xs, w1, w3, grow)


def _moe_down_kernel(te, tv, a_ref, w_ref, o_ref):
    i = pl.program_id(1)

    @pl.when(tv[i] == 1)
    def _():
        o_ref[...] = _dot(a_ref[...], w_ref[...].astype(BF16)).astype(BF16)

    @pl.when(tv[i] == 0)
    def _():
        o_ref[...] = jnp.zeros_like(o_ref)


def _moe_down(a, w2, plan, tn):
    rows, fe = a.shape
    tr = MOE_ROWS
    d = w2.shape[2]
    return pl.pallas_call(
        _moe_down_kernel,
        out_shape=jax.ShapeDtypeStruct((rows, d), BF16),
        grid_spec=pltpu.PrefetchScalarGridSpec(
            num_scalar_prefetch=2, grid=(d // tn, rows // tr),
            in_specs=[pl.BlockSpec((tr, fe), lambda n, i, te, tv: (i, 0)),
                      pl.BlockSpec((None, fe, tn), lambda n, i, te, tv: (te[i], 0, n))],
            out_specs=pl.BlockSpec((tr, tn), lambda n, i, te, tv: (i, n))),
        compiler_params=_cparams(("arbitrary", "arbitrary")), name="moe_down",
    )(plan["te"], plan["tv"], a, w2)


def _moe_combine_ln_kernel(ct, cc, cf, cl, cv, ys_ref, p0_ref, p1_ref, y_ref, gate_ref, lng_ref, lnb_ref,
                           *rest, alpha, with_h):
    if with_h:
        sh_ref, sc_ref, yo_ref, h_ref, acc_ref = rest
    else:
        yo_ref, acc_ref = rest
    p = pl.program_id(0)
    tr, mc = ys_ref.shape[0], y_ref.shape[0]

    @pl.when(cf[p] == 1)
    def _():
        acc_ref[...] = jnp.zeros_like(acc_ref)

    @pl.when(cv[p] == 1)
    def _():
        cols = ct[p] * tr + lax.broadcasted_iota(jnp.int32, (mc, tr), 1)
        sel = jnp.where((p0_ref[...] == cols) | (p1_ref[...] == cols), 1.0, 0.0).astype(BF16)
        acc_ref[...] += _dot(sel, ys_ref[...])

    @pl.when(cl[p] == 1)
    def _():
        z = alpha * y_ref[...] + gate_ref[...] * acc_ref[...]
        y = _ln_epilogue(z, lng_ref, lnb_ref)
        yo_ref[...] = y
        if with_h:
            h_ref[...] = (y * (1.0 + sc_ref[...]) + sh_ref[...]).astype(BF16)


def _moe_combine_ln(tok, ys, plan, y, mods, layer, gate_which, ln_g, ln_b, alpha, mc, h_mod=None):
    m, d = y.shape
    tr = MOE_ROWS
    with_h = h_mod is not None
    col = lambda a: a.reshape(m, 1)

    def mod(l, which):
        return pl.BlockSpec((None, None, 1, d),
                            lambda p, ct, cc, cf, cl, cv: (l, tok.rid(cc[p], mc), 0, which))

    chunk = lambda w: pl.BlockSpec((mc, w), lambda p, ct, cc, cf, cl, cv: (cc[p], 0))
    const = pl.BlockSpec((1, d), lambda p, ct, cc, cf, cl, cv: (0, 0))
    ins = [ys, col(plan["pos0"]), col(plan["pos1"]), y, mods, ln_g, ln_b]
    specs = [pl.BlockSpec((tr, d), lambda p, ct, cc, cf, cl, cv: (ct[p], 0)), chunk(1), chunk(1), chunk(d),
             mod(layer, gate_which), const, const]
    outs = [jax.ShapeDtypeStruct((m, d), F32)]
    ospecs = [chunk(d)]
    if with_h:
        hl, hsh, hsc = h_mod
        ins += [mods, mods]
        specs += [mod(hl, hsh), mod(hl, hsc)]
        outs.append(jax.ShapeDtypeStruct((m, d), BF16))
        ospecs.append(chunk(d))
    return pl.pallas_call(
        functools.partial(_moe_combine_ln_kernel, alpha=alpha, with_h=with_h),
        out_shape=outs,
        grid_spec=pltpu.PrefetchScalarGridSpec(
            num_scalar_prefetch=5, grid=(plan["maxp"],), in_specs=specs, out_specs=ospecs,
            scratch_shapes=[pltpu.VMEM((mc, d), F32)]),
        compiler_params=_cparams(("arbitrary",)), name="moe_combine_ln",
    )(*plan["combine"], *ins)


def _rope_tables(n_tokens, dim, pad_to):
    t = jnp.arange(n_tokens)
    row = (t // GRID_W).astype(F32)
    col = (t % GRID_W).astype(F32)
    half = dim // 2
    inv_freq = ROPE_THETA ** (-jnp.arange(0, half, 2, dtype=F32) / half)
    ar = row[:, None] * inv_freq[None, :]
    ac = col[:, None] * inv_freq[None, :]
    ang = jnp.concatenate([ar, ar, ac, ac], axis=-1)
    cos, sin = jnp.cos(ang), jnp.sin(ang)
    lo = (np.arange(dim) % (dim // 2)) < dim // 4
    sa = jnp.where(lo[None, :], -sin, 0.0)
    sb = jnp.where(lo[None, :], 0.0, sin)
    return cos, sa, sb


def _pad_lanes(x, width, fill):
    return jnp.concatenate([x, jnp.full((x.shape[0], width - x.shape[1]), fill, x.dtype)], axis=1)


def kernel(x_prompt, x_sample, cache_da_k, cache_da_v, cache_mla_ckv, cache_mla_krope, cache_na_k, cache_na_v, c, c_ctx, w_ada, b_ada, w_in, da_lq1, da_lk1, da_lq2, da_lk2, da_subln, mla_gq, mla_gkv, mla_wuq, mla_wukv, na_rpb, w_out, ln1_g, ln1_b, ln2_g, ln2_b, ffn_w1, ffn_w3, ffn_w2, moe_router, moe_w1, moe_w3, moe_w2):
    nbp, seq, d = x_prompt.shape
    nbs, s_lat, _ = x_sample.shape
    depth = w_in.shape[0]
    past = cache_da_k.shape[2]
    mp, ms = nbp * seq, nbs * s_lat
    m = mp + ms
    tok = _Tok(mp, s_lat, nbs)
    tm = 512
    assert mp % s_lat == 0 and s_lat % tm == 0 and mp % tm == 0 and seq % LANE == 0 and nbs + 1 <= COND_ROWS
    alpha = (2.0 * depth) ** 0.25
    rows_n = s_lat // GRID_W

    cond = jnp.concatenate([c_ctx[None], c, jnp.zeros((COND_ROWS - 1 - nbs, d), F32)], axis=0)
    mods = _ada(cond, w_ada, b_ada).reshape(depth, COND_ROWS, 1, 6 * d)

    cos, sa, sb = _rope_tables(s_lat, DA_QK, LANE)
    rope_d = tuple(jnp.tile(t, (1, 2)) for t in (cos, sa, sb))
    rope_m = (_pad_lanes(cos, LANE, 1.0), _pad_lanes(sa, LANE, 0.0), _pad_lanes(sb, LANE, 0.0))

    y = jnp.concatenate([x_prompt.reshape(mp, d), x_sample.reshape(ms, d)], axis=0)
    h = _modulate(tok, y, mods, 0, tm)

    st = [[] for _ in range(6)]
    for l in range(depth):
        lam_init = 0.8 - 0.6 * math.exp(-0.3 * l)
        wl = w_in[l]
        zpad = jnp.zeros((d, C_NAQ - (C_CKV + MLA_KV_RANK + MLA_ROPE)), F32)
        w_in_p = jnp.concatenate([wl[:, :C_CKV + MLA_KV_RANK + MLA_ROPE], zpad,
                                  wl[:, C_CKV + MLA_KV_RANK + MLA_ROPE:]], axis=1).astype(BF16)
        wuq = mla_wuq[l].reshape(MLA_Q_RANK, MLA_HEADS, MLA_NOPE + MLA_ROPE)
        wuq_p = jnp.concatenate([wuq, jnp.zeros((MLA_Q_RANK, MLA_HEADS, MLA_QK_PAD - MLA_NOPE - MLA_ROPE), F32)],
                                axis=2).reshape(MLA_Q_RANK, MLA_HEADS * MLA_QK_PAD).astype(BF16)
        wukv = mla_wukv[l].reshape(MLA_KV_RANK, MLA_HEADS, MLA_NOPE + MLA_V)
        wk = wukv[:, :, :MLA_NOPE].reshape(MLA_KV_RANK, MLA_HEADS * MLA_NOPE).astype(BF16)
        wv = wukv[:, :, MLA_NOPE:].reshape(MLA_KV_RANK, MLA_HEADS * MLA_V).astype(BF16)
        gq = mla_gq[l].reshape(1, -1)
        gkv = mla_gkv[l].reshape(1, -1)
        lams = (da_lq1[l].reshape(1, -1), da_lk1[l].reshape(1, -1), da_lq2[l].reshape(1, -1), da_lk2[l].reshape(1, -1))
        gsub = da_subln[l].reshape(1, -1)

        proj = _mm(h, w_in_p, 1024 if m % 1024 == 0 else tm, 1024)

        kcat_p, vb_p, ckvn_p, qcat_p = _mla_prep(proj, 0, mp, tm, gkv, wk, wv, gq, wuq_p)
        oa_p = _da(proj, proj, proj, lams, gsub, lam_init, nb=nbp, sq=seq, sk=seq, tq=seq,
                   qcol0=C_DAQ, kcol0=C_DAK, vcol0=C_DAV)
        ob_p = _attn(qcat_p, kcat_p, vb_p, nb=nbp, sq=seq, sk=seq, tq=seq, heads=MLA_HEADS, hp=MLA_HEADS,
                     dk=MLA_QK_PAD, dv=MLA_V, scale=MLA_SCALE)
        oc_p = _attn(proj, proj, proj, nb=nbp, sq=seq, sk=seq, tq=seq, heads=NA_HEADS, hp=NA_HEADS,
                     dk=NA_DIM, dv=NA_DIM, scale=NA_DIM ** -0.5, qcol0=C_NAQ, kcol0=C_NAK, vcol0=C_NAV)

        kcat_s, vb_s, _, qcat_s, qa_s, ka_s = _mla_prep(proj, mp, ms, tm, gkv, wk, wv, gq, wuq_p,
                                                      rope_m=rope_m, rope_d=rope_d, s_lat=s_lat)
        cache_pack = jnp.concatenate(
            [cache_mla_ckv[:, l].reshape(nbs * past, MLA_KV_RANK), cache_mla_krope[:, l].reshape(nbs * past, MLA_ROPE),
             jnp.zeros((nbs * past, 512 - MLA_KV_RANK - MLA_ROPE), F32)], axis=1)
        kcat_c, vb_c, _ = _mla_prep(cache_pack, 0, nbs * past, past, gkv, wk, wv, norm=False)
        tq = 512 if s_lat % 512 == 0 else s_lat
        oa_s = _da(qa_s, ka_s, proj, lams, gsub, lam_init, nb=nbs, sq=s_lat, sk=s_lat, tq=tq,
                   vrow0=mp, vcol0=C_DAV,
                   kc=cache_da_k[:, l].reshape(nbs, past, DA_HEADS * LANE),
                   vc=cache_da_v[:, l].reshape(nbs, past, DA_HEADS * DA_V), skc=past)
        ob_s = _attn(qcat_s, kcat_s, vb_s, nb=nbs, sq=s_lat, sk=s_lat, tq=tq, heads=MLA_HEADS, hp=1,
                     dk=MLA_QK_PAD, dv=MLA_V, scale=MLA_SCALE, kc=kcat_c, vc=vb_c, skc=past)
        bias = _na_bias_tables(na_rpb[l], rows_n)
        oc_s = _na_latent(proj, mp, nbs, s_lat,
                          cache_na_k[:, l].reshape(nbs, past, NA_HEADS * NA_DIM),
                          cache_na_v[:, l].reshape(nbs, past, NA_HEADS * NA_DIM), bias)

        oa = jnp.concatenate([oa_p, oa_s], axis=0)
        ob = jnp.concatenate([ob_p, ob_s], axis=0)
        oc = jnp.concatenate([oc_p, oc_s], axis=0)

        i = l // 2
        moe = (l % 2 == 1)
        wr = None
        if moe:
            wr = _pad_lanes(moe_router[i], LANE, 0.0).astype(BF16)
        res = _mm_ln(tok, [oa, ob, oc], w_out[l].astype(BF16), y, mods, l, 2, ln1_g[l].reshape(1, d),
                     ln1_b[l].reshape(1, d), alpha, tm, d, h_mod=(l, 3, 4), w_router=wr, name="out_proj_ln")
        y, h2 = res[0], res[1]

        nxt = (l + 1, 0, 1) if l + 1 < depth else None
        if not moe:
            a = _ffn_up(h2, ffn_w1[i], ffn_w3[i], 1024 if m % 1024 == 0 else tm, 512)
            res = _mm_ln(tok, [a], ffn_w2[i].astype(BF16), y, mods, l, 5, ln2_g[l].reshape(1, d),
                         ln2_b[l].reshape(1, d), alpha, tm, 512, h_mod=nxt, name="ffn_down_ln")
        else:
            plan = _moe_plan(res[2], res[3], m, tm)
            xs, grow = _moe_gather(h2, plan, tm)
            a = _moe_up(xs, grow, moe_w1[i], moe_w3[i], plan, 256)
            ys = _moe_down(a, moe_w2[i], plan, 512)
            res = _moe_combine_ln(tok, ys, plan, y, mods, l, 5, ln2_g[l].reshape(1, d), ln2_b[l].reshape(1, d),
                                  alpha, tm, h_mod=nxt)
        y = res[0]
        if nxt is not None:
            h = res[1]

        pp = proj[:mp]
        st[0].append(pp[:, C_DAK:C_DAK + 512].reshape(nbp, seq, DA_HEADS, 2, DA_QK))
        st[1].append(pp[:, C_DAV:C_DAV + 512].reshape(nbp, seq, DA_HEADS, DA_V))
        st[2].append(ckvn_p.reshape(nbp, seq, MLA_KV_RANK))
        st[3].append(pp[:, C_CKV + MLA_KV_RANK:C_CKV + MLA_KV_RANK + MLA_ROPE].reshape(nbp, seq, MLA_ROPE))
        st[4].append(pp[:, C_NAK:C_NAK + 512].reshape(nbp, seq, NA_HEADS, NA_DIM))
        st[5].append(pp[:, C_NAV:C_NAV + 512].reshape(nbp, seq, NA_HEADS, NA_DIM))

    y_p = y[:mp].reshape(nbp, seq, d)
    y_s = y[mp:].reshape(nbs, s_lat, d)
    return (y_p, y_s) + tuple(jnp.stack(s, axis=1) for s in st)
```

```python
import functools
import math

import numpy as np
import jax
import jax.numpy as jnp
from jax import lax
from jax.experimental import pallas as pl
from jax.experimental.pallas import tpu as pltpu

F32 = jnp.float32
BF16 = jnp.bfloat16

GRID_W = 64
DA_QK = 64
DA_V = 128
DA_HEADS = 4
MLA_Q_RANK = 512
MLA_KV_RANK = 256
MLA_NOPE = 128
MLA_ROPE = 64
MLA_V = 128
MLA_HEADS = 8
MLA_SCALE = (MLA_NOPE + MLA_ROPE) ** -0.5
NA_DIM = 128
NA_HEADS = 4
NA_KH = 8
NA_KW = 16
N_EXPERTS = 8
ROPE_THETA = 10000.0
LN_EPS = 1e-5
RMS_EPS = 1e-6
NEG_INF = -1e30
LOG2E = 1.4426950408889634
LANE = 128
COND_ROWS = 8
VMEM_LIMIT = 56 * 1024 * 1024

C_DAQ, C_DAK, C_DAV, C_CQ, C_CKV, C_NAQ, C_NAK, C_NAV = 0, 512, 1024, 1536, 2048, 2560, 3072, 3584
P_COLS = 4096
MLA_QK_PAD = 256
FFN_DOWN_TK = 1408


def _cparams(sem):
    return pltpu.CompilerParams(dimension_semantics=sem, vmem_limit_bytes=VMEM_LIMIT)


def _dot(a, b):
    return jnp.dot(a, b, preferred_element_type=F32)


def _dot_nt(a, b):
    return lax.dot_general(a, b, (((1,), (1,)), ((), ())), preferred_element_type=F32)


def _silu(x):
    return x * (1.0 / (1.0 + jnp.exp(-x)))


def _rms(x, g):
    return x * lax.rsqrt(jnp.mean(x * x, axis=-1, keepdims=True) + RMS_EPS) * g


def _rope(x, cos, sa, sb):
    return x * cos + pltpu.roll(x, LANE - 16, 1) * sa + pltpu.roll(x, 16, 1) * sb


def _ada_kernel(c_ref, w_ref, b_ref, o_ref):
    s = _silu(c_ref[...]).astype(BF16)
    o_ref[...] = _dot(s, w_ref[...].astype(BF16)) + b_ref[...]


def _ada(cond, w_ada, b_ada):
    n_layers, d, n = w_ada.shape
    tn = 1024
    return pl.pallas_call(
        _ada_kernel,
        out_shape=jax.ShapeDtypeStruct((n_layers, COND_ROWS, n), F32),
        grid=(n_layers, n // tn),
        in_specs=[pl.BlockSpec((COND_ROWS, d), lambda l, j: (0, 0)),
                  pl.BlockSpec((None, d, tn), lambda l, j: (l, 0, j)),
                  pl.BlockSpec((None, 1, tn), lambda l, j: (l, 0, j))],
        out_specs=pl.BlockSpec((None, COND_ROWS, tn), lambda l, j: (l, 0, j)),
        compiler_params=_cparams(("arbitrary", "arbitrary")), name="ada",
    )(cond, w_ada, b_ada.reshape(n_layers, 1, n))


class _Tok:
    def __init__(self, mp, s_lat, n_lat_batches):
        self.mp, self.s_lat, self.nb = mp, s_lat, n_lat_batches
        self.m = mp + s_lat * n_lat_batches

    def rid(self, i, tm):
        r0 = i * tm
        return jnp.where(r0 < self.mp, 0, 1 + (r0 - self.mp) // self.s_lat)

    def mod_spec(self, layer, which, tm, d, grid_rank=1):
        if grid_rank == 1:
            return pl.BlockSpec((None, None, 1, d), lambda i: (layer, self.rid(i, tm), 0, which))
        return pl.BlockSpec((None, None, 1, d), lambda i, k: (layer, self.rid(i, tm), 0, which))


def _modulate_kernel(x_ref, sh_ref, sc_ref, o_ref):
    o_ref[...] = (x_ref[...] * (1.0 + sc_ref[...]) + sh_ref[...]).astype(BF16)


def _modulate(tok, y, mods, layer, tm):
    m, d = y.shape
    return pl.pallas_call(
        _modulate_kernel,
        out_shape=jax.ShapeDtypeStruct((m, d), BF16),
        grid=(m // tm,),
        in_specs=[pl.BlockSpec((tm, d), lambda i: (i, 0)),
                  tok.mod_spec(layer, 0, tm, d), tok.mod_spec(layer, 1, tm, d)],
        out_specs=pl.BlockSpec((tm, d), lambda i: (i, 0)),
        compiler_params=_cparams(("arbitrary",)), name="modulate",
    )(y, mods, mods)


def _mm_kernel(x_ref, w_ref, o_ref):
    o_ref[...] = _dot(x_ref[...], w_ref[...])


def _layer_spec(arr, layer):
    zeros = (0,) * (arr.ndim - 1)
    return pl.BlockSpec((None,) + arr.shape[1:], lambda *_: (layer,) + zeros)


def _mm(x, w, layer, tm, tn):
    m, k = x.shape
    n = w.shape[2]
    return pl.pallas_call(
        _mm_kernel,
        out_shape=jax.ShapeDtypeStruct((m, n), F32),
        grid=(n // tn, m // tm),
        in_specs=[pl.BlockSpec((tm, k), lambda j, i: (i, 0)),
                  pl.BlockSpec((None, k, tn), lambda j, i: (layer, 0, j))],
        out_specs=pl.BlockSpec((tm, tn), lambda j, i: (i, j)),
        compiler_params=_cparams(("arbitrary", "arbitrary")), name="in_proj",
    )(x, w)


def _mla_prep_kernel(*refs, norm, with_q, rope, da_rope):
    it = iter(refs)
    ckv_ref = next(it)
    cq_ref = next(it) if with_q else None
    daq_ref = next(it) if da_rope else None
    dak_ref = next(it) if da_rope else None
    gkv_ref, wk_ref, wv_ref = next(it), next(it), next(it)
    gq_ref = next(it) if with_q else None
    wuq_ref = next(it) if with_q else None
    if rope:
        cm_ref, sam_ref, sbm_ref = next(it), next(it), next(it)
    if da_rope:
        cd_ref, sad_ref, sbd_ref = next(it), next(it), next(it)
    kcat_ref, vb_ref, ckvn_ref = next(it), next(it), next(it)
    qcat_ref = next(it) if with_q else None
    qa_ref = next(it) if da_rope else None
    ka_ref = next(it) if da_rope else None

    ckv = ckv_ref[:, :MLA_KV_RANK]
    ckvn = _rms(ckv, gkv_ref[...]) if norm else ckv
    ckvn_ref[...] = ckvn
    kr = ckv_ref[:, MLA_KV_RANK:MLA_KV_RANK + LANE]
    if rope:
        kr = _rope(kr, cm_ref[...], sam_ref[...], sbm_ref[...])
    kr = kr.astype(BF16)
    cb = ckvn.astype(BF16)
    kn = _dot(cb, wk_ref[...]).astype(BF16)
    vb_ref[...] = _dot(cb, wv_ref[...]).astype(BF16)
    for h in range(MLA_HEADS):
        kcat_ref[:, h * MLA_QK_PAD:h * MLA_QK_PAD + MLA_NOPE] = kn[:, h * MLA_NOPE:(h + 1) * MLA_NOPE]
        kcat_ref[:, h * MLA_QK_PAD + MLA_NOPE:(h + 1) * MLA_QK_PAD] = kr
    if with_q:
        cqn = _rms(cq_ref[...], gq_ref[...]).astype(BF16)
        q = _dot(cqn, wuq_ref[...])
        for h in range(MLA_HEADS):
            lo = h * MLA_QK_PAD
            qcat_ref[:, lo:lo + MLA_NOPE] = q[:, lo:lo + MLA_NOPE].astype(BF16)
            qr = q[:, lo + MLA_NOPE:lo + MLA_QK_PAD]
            if rope:
                qr = _rope(qr, cm_ref[...], sam_ref[...], sbm_ref[...])
            qcat_ref[:, lo + MLA_NOPE:lo + MLA_QK_PAD] = qr.astype(BF16)
    if da_rope:
        for h in range(DA_HEADS):
            sl = slice(h * LANE, (h + 1) * LANE)
            qa_ref[:, sl] = _rope(daq_ref[:, sl], cd_ref[...], sad_ref[...], sbd_ref[...]).astype(BF16)
            ka_ref[:, sl] = _rope(dak_ref[:, sl], cd_ref[...], sad_ref[...], sbd_ref[...]).astype(BF16)


def _mla_prep(src, layer, row0, rows, tm, gkv, wk, wv, gq=None, wuq=None, rope_m=None, rope_d=None,
              norm=True, s_lat=None, row_step=1):
    with_q = wuq is not None
    rope = rope_m is not None
    da_rope = rope_d is not None
    b0 = row0 // tm
    n = rows // tm
    ckv_blk = C_CKV // 512 if src.shape[1] == P_COLS else 0
    blk = lambda c: pl.BlockSpec((tm, 512), lambda i: (b0 + i * row_step, c))
    ins, specs = [src], [blk(ckv_blk)]
    if with_q:
        ins.append(src)
        specs.append(blk(C_CQ // 512))
    if da_rope:
        ins += [src, src]
        specs += [blk(C_DAQ // 512), blk(C_DAK // 512)]
    ins += [gkv, wk, wv]
    specs += [_layer_spec(gkv, layer), _layer_spec(wk, layer), _layer_spec(wv, layer)]
    if with_q:
        ins += [gq, wuq]
        specs += [_layer_spec(gq, layer), _layer_spec(wuq, layer)]
    nt = (s_lat // tm) if rope else 1
    if rope:
        ins += list(rope_m)
        specs += [pl.BlockSpec((tm, LANE), lambda i: (i % nt, 0))] * 3
    if da_rope:
        ins += list(rope_d)
        specs += [pl.BlockSpec((tm, LANE), lambda i: (i % nt, 0))] * 3
    outs = [jax.ShapeDtypeStruct((rows, MLA_HEADS * MLA_QK_PAD), BF16),
            jax.ShapeDtypeStruct((rows, MLA_HEADS * MLA_V), BF16),
            jax.ShapeDtypeStruct((rows, MLA_KV_RANK), F32)]
    if with_q:
        outs.append(jax.ShapeDtypeStruct((rows, MLA_HEADS * MLA_QK_PAD), BF16))
    if da_rope:
        outs += [jax.ShapeDtypeStruct((rows, DA_HEADS * LANE), BF16)] * 2
    ospecs = [pl.BlockSpec((tm, o.shape[1]), lambda i: (i, 0)) for o in outs]
    return pl.pallas_call(
        functools.partial(_mla_prep_kernel, norm=norm, with_q=with_q, rope=rope, da_rope=da_rope),
        out_shape=outs, grid=(n,), in_specs=specs, out_specs=ospecs,
        compiler_params=_cparams(("arbitrary",)), name="mla_prep",
    )(*ins)


def _softmax_parts(s, s2, scale):
    c = scale * LOG2E
    m = jnp.max(s, axis=-1, keepdims=True)
    if s2 is not None:
        m = jnp.maximum(m, jnp.max(s2, axis=-1, keepdims=True))
    e = jnp.exp2((s - m) * c)
    den = jnp.sum(e, axis=-1, keepdims=True)
    e2 = None
    if s2 is not None:
        e2 = jnp.exp2((s2 - m) * c)
        den = den + jnp.sum(e2, axis=-1, keepdims=True)
    return e, e2, 1.0 / den


def _attn_kernel(*refs, heads, dk, dv, scale, has_ctx):
    if has_ctx:
        q_ref, k_ref, v_ref, kc_ref, vc_ref, o_ref = refs
    else:
        q_ref, k_ref, v_ref, o_ref = refs
    for h in range(heads):
        q = q_ref[:, h * dk:(h + 1) * dk].astype(BF16)
        s = _dot_nt(q, k_ref[:, h * dk:(h + 1) * dk].astype(BF16))
        s2 = None
        if has_ctx:
            s2 = _dot_nt(q, kc_ref[:, h * dk:(h + 1) * dk].astype(BF16))
        e, e2, inv = _softmax_parts(s, s2, scale)
        o = _dot(e.astype(BF16), v_ref[:, h * dv:(h + 1) * dv].astype(BF16))
        if has_ctx:
            o = o + _dot(e2.astype(BF16), vc_ref[:, h * dv:(h + 1) * dv].astype(BF16))
        o_ref[:, h * dv:(h + 1) * dv] = (o * inv).astype(BF16)


def _attn(q, k, v, *, nb, sq, sk, tq, heads, hp, dk, dv, scale, qrow0=0, krow0=0, qcol0=0, kcol0=0, vcol0=0,
          kc=None, vc=None, skc=0):
    nq = sq // tq
    ng = heads // hp
    wq, wv = hp * dk, hp * dv
    has_ctx = kc is not None
    ins = [q, k, v]
    specs = [pl.BlockSpec((tq, wq), lambda b, g, i: (qrow0 // tq + b * nq + i, qcol0 // wq + g)),
             pl.BlockSpec((sk, wq), lambda b, g, i: (krow0 // sk + b, kcol0 // wq + g)),
             pl.BlockSpec((sk, wv), lambda b, g, i: (krow0 // sk + b, vcol0 // wv + g))]
    if has_ctx:
        ins += [kc, vc]
        specs += [pl.BlockSpec((skc, wq), lambda b, g, i: (b, g)),
                  pl.BlockSpec((skc, wv), lambda b, g, i: (b, g))]
    return pl.pallas_call(
        functools.partial(_attn_kernel, heads=hp, dk=dk, dv=dv, scale=scale, has_ctx=has_ctx),
        out_shape=jax.ShapeDtypeStruct((nb * sq, heads * dv), BF16),
        grid=(nb, ng, nq), in_specs=specs,
        out_specs=pl.BlockSpec((tq, wv), lambda b, g, i: (b * nq + i, g)),
        compiler_params=_cparams(("arbitrary", "arbitrary", "arbitrary")), name="attn_h%d" % heads,
    )(*ins)


def _da_kernel(*refs, has_ctx, lam_init):
    if has_ctx:
        q_ref, k_ref, v_ref, kc_ref, vc_ref, lq1, lk1, lq2, lk2, g_ref, o_ref = refs
    else:
        q_ref, k_ref, v_ref, lq1, lk1, lq2, lk2, g_ref, o_ref = refs
    lam = (jnp.exp(jnp.sum(lq1[...] * lk1[...], axis=-1, keepdims=True))
           - jnp.exp(jnp.sum(lq2[...] * lk2[...], axis=-1, keepdims=True)) + lam_init)
    first = lax.broadcasted_iota(jnp.int32, (1, LANE), 1) < DA_QK
    scale = DA_QK ** -0.5
    for h in range(DA_HEADS):
        sl = slice(h * LANE, (h + 1) * LANE)
        q = q_ref[:, sl].astype(F32) * scale
        qs = (jnp.where(first, q, 0.0).astype(BF16), jnp.where(first, 0.0, q).astype(BF16))
        k = k_ref[:, sl].astype(BF16)
        kc = kc_ref[:, sl].astype(BF16) if has_ctx else None
        v = v_ref[:, sl].astype(BF16)
        vc = vc_ref[:, sl].astype(BF16) if has_ctx else None
        os = []
        for c in range(2):
            s = _dot_nt(qs[c], k)
            s2 = _dot_nt(qs[c], kc) if has_ctx else None
            e, e2, inv = _softmax_parts(s, s2, 1.0)
            oc = _dot(e.astype(BF16), v)
            if has_ctx:
                oc = oc + _dot(e2.astype(BF16), vc)
            os.append(oc * inv)
        o = os[0] - lam * os[1]
        o = _rms(o, g_ref[...]) * (1.0 - lam_init)
        o_ref[:, sl] = o.astype(BF16)


def _da(q, k, v, lams, g, layer, lam_init, *, nb, sq, sk, tq, qrow0=0, krow0=0, qcol0=0, kcol0=0, vcol0=0,
        vrow0=None, kc=None, vc=None, skc=0):
    nq = sq // tq
    w = DA_HEADS * LANE
    vrow0 = krow0 if vrow0 is None else vrow0
    has_ctx = kc is not None
    ins = [q, k, v]
    specs = [pl.BlockSpec((tq, w), lambda b, i: (qrow0 // tq + b * nq + i, qcol0 // w)),
             pl.BlockSpec((sk, w), lambda b, i: (krow0 // sk + b, kcol0 // w)),
             pl.BlockSpec((sk, w), lambda b, i: (vrow0 // sk + b, vcol0 // w))]
    if has_ctx:
        ins += [kc, vc]
        specs += [pl.BlockSpec((None, None, skc, w), lambda b, i: (b, layer, 0, 0))] * 2
    ins += list(lams) + [g]
    specs += [_layer_spec(a, layer) for a in ins[-5:]]
    return pl.pallas_call(
        functools.partial(_da_kernel, has_ctx=has_ctx, lam_init=lam_init),
        out_shape=jax.ShapeDtypeStruct((nb * sq, w), BF16),
        grid=(nb, nq), in_specs=specs,
        out_specs=pl.BlockSpec((tq, w), lambda b, i: (b * nq + i, 0)),
        compiler_params=_cparams(("arbitrary", "arbitrary")), name="diff_attn",
    )(*ins)


def _na_kernel(q_ref, k_ref, v_ref, kc_ref, vc_ref, bias_ref, o_ref, *, rows_n, kh):
    r = pl.program_id(1)
    rs = jnp.clip(r - kh // 2, 0, rows_n - kh)
    start = pl.multiple_of(rs * GRID_W, GRID_W)
    nwin = kh * GRID_W
    scale = NA_DIM ** -0.5
    for h in range(NA_HEADS):
        sl = slice(h * NA_DIM, (h + 1) * NA_DIM)
        q = q_ref[:, sl].astype(BF16)
        kw = k_ref[pl.ds(start, nwin), sl].astype(BF16)
        vw = v_ref[pl.ds(start, nwin), sl].astype(BF16)
        s = _dot_nt(q, kw) * scale + bias_ref[h]
        s2 = _dot_nt(q, kc_ref[:, sl].astype(BF16)) * scale
        e, e2, inv = _softmax_parts(s, s2, 1.0)
        o = _dot(e.astype(BF16), vw) + _dot(e2.astype(BF16), vc_ref[:, sl].astype(BF16))
        o_ref[:, sl] = (o * inv).astype(BF16)


def _na_latent(proj, layer, row0, nb, s_lat, kc, vc, bias):
    rows_n = s_lat // GRID_W
    kh = min(NA_KH, rows_n)
    w = NA_HEADS * NA_DIM
    skc = kc.shape[2]

    def variant(r):
        return r - jnp.clip(r - kh // 2, 0, rows_n - kh)

    return pl.pallas_call(
        functools.partial(_na_kernel, rows_n=rows_n, kh=kh),
        out_shape=jax.ShapeDtypeStruct((nb * s_lat, w), BF16),
        grid=(nb, rows_n),
        in_specs=[pl.BlockSpec((GRID_W, w), lambda b, r: (row0 // GRID_W + b * rows_n + r, C_NAQ // w)),
                  pl.BlockSpec((s_lat, w), lambda b, r: (row0 // s_lat + b, C_NAK // w)),
                  pl.BlockSpec((s_lat, w), lambda b, r: (row0 // s_lat + b, C_NAV // w)),
                  pl.BlockSpec((None, None, skc, w), lambda b, r: (b, layer, 0, 0)),
                  pl.BlockSpec((None, None, skc, w), lambda b, r: (b, layer, 0, 0)),
                  pl.BlockSpec((None, NA_HEADS, None, GRID_W, kh * GRID_W),
                               lambda b, r: (layer, 0, variant(r), 0, 0))],
        out_specs=pl.BlockSpec((GRID_W, w), lambda b, r: (b * rows_n + r, 0)),
        compiler_params=_cparams(("arbitrary", "arbitrary")), name="na_latent",
    )(proj, proj, proj, kc, vc, bias)


def _na_bias_tables(rpb, rows_n):
    kh = min(NA_KH, rows_n)
    qc = np.arange(GRID_W)[:, None]
    kc = np.arange(GRID_W)[None, :]
    col_start = np.clip(qc - NA_KW // 2, 0, GRID_W - NA_KW)
    valid = (kc >= col_start) & (kc < col_start + NA_KW)
    coff = np.clip(kc - qc, -(NA_KW - 1), NA_KW - 1) + (NA_KW - 1)
    onehot = (coff.reshape(-1)[None, :] == np.arange(2 * NA_KW - 1)[:, None]).astype(np.float32)
    n_l = rpb.shape[0]
    t = jnp.einsum('lhrc,cx->lhrx', rpb.astype(F32), jnp.asarray(onehot), precision=lax.Precision.HIGHEST)
    t = t.reshape(n_l, NA_HEADS, 2 * NA_KH - 1, GRID_W, GRID_W)
    t = jnp.where(jnp.asarray(valid)[None, None, None], t, NEG_INF)
    strips = []
    for v in range(kh):
        lo = NA_KH - 1 - v
        s = t[:, :, lo:lo + kh]
        strips.append(jnp.transpose(s, (0, 1, 3, 2, 4)).reshape(n_l, NA_HEADS, GRID_W, kh * GRID_W))
    return jnp.stack(strips, axis=2)


def _ln_epilogue(z, g_ref, b_ref):
    mu = jnp.mean(z, axis=-1, keepdims=True)
    zc = z - mu
    var = jnp.mean(zc * zc, axis=-1, keepdims=True)
    return zc * lax.rsqrt(var + LN_EPS) * g_ref[...] + b_ref[...]


def _router_info(h, wr_ref, cnt_ref):
    n = h.shape[0]
    logits = _dot(h, wr_ref[...])
    lane = lax.broadcasted_iota(jnp.int32, logits.shape, 1)
    lg = jnp.where(lane < N_EXPERTS, logits, -jnp.inf)
    m1 = jnp.max(lg, axis=-1, keepdims=True)
    i1 = jnp.min(jnp.where(lg == m1, lane, LANE), axis=-1, keepdims=True)
    lg2 = jnp.where(lane == i1, -jnp.inf, lg)
    m2 = jnp.max(lg2, axis=-1, keepdims=True)
    i2 = jnp.min(jnp.where(lg2 == m2, lane, LANE), axis=-1, keepdims=True)
    e2 = jnp.exp(m2 - m1)
    inv = 1.0 / (1.0 + e2)
    oh1, oh2 = lane == i1, lane == i2
    o1, o2 = jnp.where(oh1, 1.0, 0.0), jnp.where(oh2, 1.0, 0.0)
    below = lax.broadcasted_iota(jnp.int32, (n, n), 1) < lax.broadcasted_iota(jnp.int32, (n, n), 0)
    tri = jnp.where(below, 1.0, 0.0).astype(BF16)
    p1 = _dot(tri, o1.astype(BF16))
    p2 = _dot(tri, o2.astype(BF16))
    tot1 = jnp.sum(o1, axis=0, keepdims=True)
    tot2 = jnp.sum(o2, axis=0, keepdims=True)
    cnt = cnt_ref[...]
    rank0 = jnp.sum(jnp.where(oh1, cnt + p1, 0.0), axis=-1, keepdims=True)
    rank1 = jnp.sum(jnp.where(oh2, cnt + tot1 + p2, 0.0), axis=-1, keepdims=True)
    cnt_ref[...] = cnt + tot1 + tot2
    cols = (i1.astype(F32), i2.astype(F32), inv, e2 * inv, rank0, rank1)
    info = jnp.zeros(logits.shape, F32)
    for c, v in enumerate(cols):
        info = jnp.where(lane == c, v, info)
    return info


def _mm_ln_kernel(*refs, n_x, k_total, tk, alpha, with_h, with_router, mask_k):
    it = iter(refs)
    x_refs = [next(it) for _ in range(n_x)]
    w_ref, y_ref, gate_ref, lng_ref, lnb_ref = next(it), next(it), next(it), next(it), next(it)
    sh_ref = next(it) if with_h else None
    sc_ref = next(it) if with_h else None
    wr_ref = next(it) if with_router else None
    yo_ref = next(it)
    h_ref = next(it) if with_h else None
    go_ref = next(it) if with_router else None
    ca_ref = next(it) if with_router else None
    acc_ref = next(it)
    cnt_ref = next(it) if with_router else None
    k = pl.program_id(1)
    nk = pl.num_programs(1)
    if with_router:
        @pl.when((pl.program_id(0) == 0) & (k == 0))
        def _():
            cnt_ref[...] = jnp.zeros_like(cnt_ref)
    x = x_refs[0][...] if n_x == 1 else jnp.concatenate([r[...] for r in x_refs], axis=1)
    w = w_ref[...]
    if mask_k:
        lim = k_total - k * tk
        x = jnp.where(lax.broadcasted_iota(jnp.int32, x.shape, 1) < lim, x, jnp.zeros_like(x))
        w = jnp.where(lax.broadcasted_iota(jnp.int32, w.shape, 0) < lim, w, jnp.zeros_like(w))
    part = _dot(x, w.astype(BF16))

    @pl.when(k == 0)
    def _():
        acc_ref[...] = part

    @pl.when(k > 0)
    def _():
        acc_ref[...] += part

    @pl.when(k == nk - 1)
    def _():
        z = alpha * y_ref[...] + gate_ref[...] * acc_ref[...]
        y = _ln_epilogue(z, lng_ref, lnb_ref)
        yo_ref[...] = y
        if with_h:
            h = (y * (1.0 + sc_ref[...]) + sh_ref[...]).astype(BF16)
            h_ref[...] = h
            if with_router:
                go_ref[...] = _router_info(h, wr_ref, cnt_ref)
                ca_ref[...] = cnt_ref[...]


def _mm_ln(tok, xs, w, w_layer, y, mods, layer, gate_which, ln_g, ln_b, alpha, tm, tk, h_mod=None,
           w_router=None, router_layer=0, name="mm_ln"):
    m, d = y.shape
    k_total = w.shape[1]
    nk = pl.cdiv(k_total, tk)
    with_h = h_mod is not None
    with_router = w_router is not None
    ins, specs = [], []
    for x in xs:
        wx = x.shape[1] if len(xs) > 1 else tk
        ins.append(x)
        specs.append(pl.BlockSpec((tm, wx), lambda i, k: (i, k)))
    ins += [w, y, mods, ln_g, ln_b]
    specs += [pl.BlockSpec((None, tk, d), lambda i, k: (w_layer, k, 0)),
              pl.BlockSpec((tm, d), lambda i, k: (i, 0)),
              tok.mod_spec(layer, gate_which, tm, d, 2),
              _layer_spec(ln_g, layer), _layer_spec(ln_b, layer)]
    outs = [jax.ShapeDtypeStruct((m, d), F32)]
    ospecs = [pl.BlockSpec((tm, d), lambda i, k: (i, 0))]
    if with_h:
        hl, hsh, hsc = h_mod
        ins += [mods, mods]
        specs += [tok.mod_spec(hl, hsh, tm, d, 2), tok.mod_spec(hl, hsc, tm, d, 2)]
        outs.append(jax.ShapeDtypeStruct((m, d), BF16))
        ospecs.append(pl.BlockSpec((tm, d), lambda i, k: (i, 0)))
    if with_router:
        ins.append(w_router)
        specs.append(_layer_spec(w_router, router_layer))
        outs += [jax.ShapeDtypeStruct((m, LANE), F32), jax.ShapeDtypeStruct((m // tm, 1, LANE), F32)]
        ospecs += [pl.BlockSpec((tm, LANE), lambda i, k: (i, 0)),
                   pl.BlockSpec((None, 1, LANE), lambda i, k: (i, 0, 0))]
    scratch = [pltpu.VMEM((tm, d), F32)]
    if with_router:
        scratch.append(pltpu.VMEM((1, LANE), F32))
    res = pl.pallas_call(
        functools.partial(_mm_ln_kernel, n_x=len(xs), k_total=k_total, tk=tk, alpha=alpha, with_h=with_h,
                          with_router=with_router, mask_k=(k_total % tk != 0)),
        out_shape=outs, grid=(m // tm, nk), in_specs=specs, out_specs=ospecs,
        scratch_shapes=scratch,
        compiler_params=_cparams(("arbitrary", "arbitrary")), name=name,
    )(*ins)
    return res


def _ffn_up_kernel(x_ref, w1_ref, w3_ref, o_ref):
    x = x_ref[...]
    a = _silu(_dot(x, w1_ref[...].astype(BF16))) * _dot(x, w3_ref[...].astype(BF16))
    o_ref[...] = a.astype(BF16)


def _ffn_up(x, w1, w3, layer, tm, tf):
    m, d = x.shape
    f = w1.shape[2]
    wspec = pl.BlockSpec((None, d, tf), lambda j, i: (layer, 0, j))
    return pl.pallas_call(
        _ffn_up_kernel,
        out_shape=jax.ShapeDtypeStruct((m, f), BF16),
        grid=(pl.cdiv(f, tf), m // tm),
        in_specs=[pl.BlockSpec((tm, d), lambda j, i: (i, 0)), wspec, wspec],
        out_specs=pl.BlockSpec((tm, tf), lambda j, i: (i, j)),
        compiler_params=_cparams(("arbitrary", "arbitrary")), name="ffn_up",
    )(x, w1, w3)


MOE_ROWS = 1024
MOE_SEL_ROWS = 256


def _moe_plan(info, c_after, m, mc):
    tr, ts = MOE_ROWS, MOE_SEL_ROWS
    e_n = N_EXPERTS
    n_chunks = m // mc
    n_tiles = (2 * m) // tr + e_n
    n_blocks = n_tiles * (tr // ts)
    maxp = n_blocks + e_n * n_chunks
    i32 = jnp.int32
    i1, i2 = info[:, 0].astype(i32), info[:, 1].astype(i32)
    r0, r1 = info[:, 4].astype(i32), info[:, 5].astype(i32)
    ca = c_after[:, 0, :e_n].astype(i32)
    cb = jnp.concatenate([jnp.zeros((1, e_n), i32), ca[:-1]], axis=0)
    counts = ca[-1]
    padded = ((counts + tr - 1) // tr) * tr
    start = jnp.cumsum(padded) - padded
    eid = jnp.arange(e_n, dtype=i32)

    def pick(idx, table):
        return jnp.sum(jnp.where(idx[:, None] == eid[None, :], table[None, :], 0), axis=1)

    pos0 = pick(i1, start) + r0
    pos1 = pick(i2, start) + r1
    def expert_of(row0):
        return jnp.minimum(jnp.sum((row0[:, None] >= (start + padded)[None, :]).astype(i32), axis=1), e_n - 1)

    trow0 = jnp.arange(n_tiles, dtype=i32) * tr
    te = expert_of(trow0)
    tv = trow0 < jnp.sum(padded)
    row0 = jnp.arange(n_blocks, dtype=i32) * ts
    be = expert_of(row0)
    k0 = row0 - pick(be, start)
    k1 = jnp.minimum(k0 + ts, pick(be, counts))
    sel = (be[:, None] == eid[None, :])
    cb_t = jnp.sum(jnp.where(sel[:, None, :], cb[None], 0), axis=2)
    ca_t = jnp.sum(jnp.where(sel[:, None, :], ca[None], 0), axis=2)
    ov = (row0 < jnp.sum(padded))[:, None] & (cb_t < k1[:, None]) & (ca_t > k0[:, None])
    first_chunk = (jnp.arange(n_chunks) == 0)[None, :]
    ov_g = ov | (~jnp.any(ov, axis=1, keepdims=True) & first_chunk)

    def pairs(mask2d, inner):
        flat = mask2d.reshape(-1)
        n = jnp.sum(flat.astype(i32))
        idx = jnp.nonzero(flat, size=maxp, fill_value=0)[0].astype(i32)
        p = jnp.arange(maxp, dtype=i32)
        valid = p < n
        idx = jnp.where(valid, idx, jnp.max(jnp.where(valid, idx, 0)))
        outer, inn = idx // inner, idx % inner
        prev = jnp.concatenate([jnp.full((1,), -1, i32), outer[:-1]])
        nxt = jnp.concatenate([outer[1:], jnp.full((1,), -1, i32)])
        first = valid & (outer != prev)
        last = valid & ((outer != nxt) | (p == n - 1))
        return outer, inn, first.astype(i32), last.astype(i32), valid.astype(i32)

    g_tile, g_chunk, g_first, _, g_valid = pairs(ov_g, n_chunks)
    c_chunk, c_tile, c_first, c_last, c_valid = pairs(ov.T, n_blocks)
    return dict(pos0=pos0, pos1=pos1, g1=info[:, 2], g2=info[:, 3], te=te, tv=tv.astype(i32),
                gather=(g_tile, g_chunk, g_first, g_valid),
                combine=(c_tile, c_chunk, c_first, c_last, c_valid), n_tiles=n_tiles, maxp=maxp)


def _moe_gather_kernel(pt, pc, pf, pv, h_ref, p0_ref, p1_ref, g0_ref, g1_ref, xs_ref, gr_ref):
    p = pl.program_id(0)
    tr, mc = xs_ref.shape[0], h_ref.shape[0]

    @pl.when(pf[p] == 1)
    def _():
        xs_ref[...] = jnp.zeros_like(xs_ref)
        gr_ref[...] = jnp.zeros_like(gr_ref)

    @pl.when(pv[p] == 1)
    def _():
        rows = pt[p] * tr + lax.broadcasted_iota(jnp.int32, (tr, mc), 0)
        m0 = p0_ref[...] == rows
        m1 = p1_ref[...] == rows
        sel = jnp.where(m0 | m1, 1.0, 0.0).astype(BF16)
        xs_ref[...] = (xs_ref[...].astype(F32) + _dot(sel, h_ref[...])).astype(BF16)
        gr_ref[...] += jnp.sum(jnp.where(m0, g0_ref[...], 0.0) + jnp.where(m1, g1_ref[...], 0.0),
                               axis=1, keepdims=True)


def _moe_gather(h, plan, mc):
    m, d = h.shape
    tr = MOE_SEL_ROWS
    rows = plan["n_tiles"] * MOE_ROWS
    row = lambda a: a.reshape(1, m)
    tok_spec = lambda: pl.BlockSpec((1, mc), lambda p, pt, pc, pf, pv: (0, pc[p]))
    return pl.pallas_call(
        _moe_gather_kernel,
        out_shape=[jax.ShapeDtypeStruct((rows, d), BF16), jax.ShapeDtypeStruct((rows, 1), F32)],
        grid_spec=pltpu.PrefetchScalarGridSpec(
            num_scalar_prefetch=4, grid=(plan["maxp"],),
            in_specs=[pl.BlockSpec((mc, d), lambda p, pt, pc, pf, pv: (pc[p], 0)),
                      tok_spec(), tok_spec(), tok_spec(), tok_spec()],
            out_specs=[pl.BlockSpec((tr, d), lambda p, pt, pc, pf, pv: (pt[p], 0)),
                       pl.BlockSpec((tr, 1), lambda p, pt, pc, pf, pv: (pt[p], 0))]),
        compiler_params=_cparams(("arbitrary",)), name="moe_gather",
    )(*plan["gather"], h, row(plan["pos0"]), row(plan["pos1"]), row(plan["g1"]), row(plan["g2"]))


def _moe_up_kernel(te, tv, x_ref, w1_ref, w3_ref, g_ref, o_ref):
    i = pl.program_id(1)

    @pl.when(tv[i] == 1)
    def _():
        x = x_ref[...]
        a = _silu(_dot(x, w1_ref[...].astype(BF16))) * _dot(x, w3_ref[...].astype(BF16))
        o_ref[...] = (a * g_ref[...]).astype(BF16)

    @pl.when(tv[i] == 0)
    def _():
        o_ref[...] = jnp.zeros_like(o_ref)


def _moe_up(xs, grow, w1, w3, layer, plan, tf):
    rows, d = xs.shape
    tr = MOE_ROWS
    fe = w1.shape[3]
    wspec = pl.BlockSpec((None, None, d, tf), lambda j, i, te, tv: (layer, te[i], 0, j))
    return pl.pallas_call(
        _moe_up_kernel,
        out_shape=jax.ShapeDtypeStruct((rows, fe), BF16),
        grid_spec=pltpu.PrefetchScalarGridSpec(
            num_scalar_prefetch=2, grid=(fe // tf, rows // tr),
            in_specs=[pl.BlockSpec((tr, d), lambda j, i, te, tv: (i, 0)), wspec, wspec,
                      pl.BlockSpec((tr, 1), lambda j, i, te, tv: (i, 0))],
            out_specs=pl.BlockSpec((tr, tf), lambda j, i, te, tv: (i, j))),
        compiler_params=_cparams(("arbitrary", "arbitrary")), name="moe_up",
    )(plan["te"], plan["tv"], xs, w1, w3, grow)


def _moe_down_kernel(te, tv, a_ref, w_ref, o_ref):
    i = pl.program_id(1)

    @pl.when(tv[i] == 1)
    def _():
        o_ref[...] = _dot(a_ref[...], w_ref[...].astype(BF16)).astype(BF16)

    @pl.when(tv[i] == 0)
    def _():
        o_ref[...] = jnp.zeros_like(o_ref)


def _moe_down(a, w2, layer, plan, tn):
    rows, fe = a.shape
    tr = MOE_ROWS
    d = w2.shape[3]
    return pl.pallas_call(
        _moe_down_kernel,
        out_shape=jax.ShapeDtypeStruct((rows, d), BF16),
        grid_spec=pltpu.PrefetchScalarGridSpec(
            num_scalar_prefetch=2, grid=(d // tn, rows // tr),
            in_specs=[pl.BlockSpec((tr, fe), lambda n, i, te, tv: (i, 0)),
                      pl.BlockSpec((None, None, fe, tn), lambda n, i, te, tv: (layer, te[i], 0, n))],
            out_specs=pl.BlockSpec((tr, tn), lambda n, i, te, tv: (i, n))),
        compiler_params=_cparams(("arbitrary", "arbitrary")), name="moe_down",
    )(plan["te"], plan["tv"], a, w2)


def _moe_combine_ln_kernel(ct, cc, cf, cl, cv, ys_ref, p0_ref, p1_ref, y_ref, gate_ref, lng_ref, lnb_ref,
                           *rest, alpha, with_h):
    if with_h:
        sh_ref, sc_ref, yo_ref, h_ref, acc_ref = rest
    else:
        yo_ref, acc_ref = rest
    p = pl.program_id(0)
    tr, mc = ys_ref.shape[0], y_ref.shape[0]

    @pl.when(cf[p] == 1)
    def _():
        acc_ref[...] = jnp.zeros_like(acc_ref)

    @pl.when(cv[p] == 1)
    def _():
        cols = ct[p] * tr + lax.broadcasted_iota(jnp.int32, (mc, tr), 1)
        sel = jnp.where((p0_ref[...] == cols) | (p1_ref[...] == cols), 1.0, 0.0).astype(BF16)
        acc_ref[...] += _dot(sel, ys_ref[...])

    @pl.when(cl[p] == 1)
    def _():
        z = alpha * y_ref[...] + gate_ref[...] * acc_ref[...]
        y = _ln_epilogue(z, lng_ref, lnb_ref)
        yo_ref[...] = y
        if with_h:
            h_ref[...] = (y * (1.0 + sc_ref[...]) + sh_ref[...]).astype(BF16)


def _moe_combine_ln(tok, ys, plan, y, mods, layer, gate_which, ln_g, ln_b, alpha, mc, h_mod=None):
    m, d = y.shape
    tr = MOE_SEL_ROWS
    with_h = h_mod is not None
    col = lambda a: a.reshape(m, 1)

    def mod(l, which):
        return pl.BlockSpec((None, None, 1, d),
                            lambda p, ct, cc, cf, cl, cv: (l, tok.rid(cc[p], mc), 0, which))

    chunk = lambda w: pl.BlockSpec((mc, w), lambda p, ct, cc, cf, cl, cv: (cc[p], 0))
    ins = [ys, col(plan["pos0"]), col(plan["pos1"]), y, mods, ln_g, ln_b]
    specs = [pl.BlockSpec((tr, d), lambda p, ct, cc, cf, cl, cv: (ct[p], 0)), chunk(1), chunk(1), chunk(d),
             mod(layer, gate_which), _layer_spec(ln_g, layer), _layer_spec(ln_b, layer)]
    outs = [jax.ShapeDtypeStruct((m, d), F32)]
    ospecs = [chunk(d)]
    if with_h:
        hl, hsh, hsc = h_mod
        ins += [mods, mods]
        specs += [mod(hl, hsh), mod(hl, hsc)]
        outs.append(jax.ShapeDtypeStruct((m, d), BF16))
        ospecs.append(chunk(d))
    return pl.pallas_call(
        functools.partial(_moe_combine_ln_kernel, alpha=alpha, with_h=with_h),
        out_shape=outs,
        grid_spec=pltpu.PrefetchScalarGridSpec(
            num_scalar_prefetch=5, grid=(plan["maxp"],), in_specs=specs, out_specs=ospecs,
            scratch_shapes=[pltpu.VMEM((mc, d), F32)]),
        compiler_params=_cparams(("arbitrary",)), name="moe_combine_ln",
    )(*plan["combine"], *ins)


def _rope_tables(n_tokens, dim, pad_to):
    t = jnp.arange(n_tokens)
    row = (t // GRID_W).astype(F32)
    col = (t % GRID_W).astype(F32)
    half = dim // 2
    inv_freq = ROPE_THETA ** (-jnp.arange(0, half, 2, dtype=F32) / half)
    ar = row[:, None] * inv_freq[None, :]
    ac = col[:, None] * inv_freq[None, :]
    ang = jnp.concatenate([ar, ar, ac, ac], axis=-1)
    cos, sin = jnp.cos(ang), jnp.sin(ang)
    lo = (np.arange(dim) % (dim // 2)) < dim // 4
    sa = jnp.where(lo[None, :], -sin, 0.0)
    sb = jnp.where(lo[None, :], 0.0, sin)
    return cos, sa, sb


def _pad_lanes(x, width, fill):
    return jnp.concatenate([x, jnp.full((x.shape[0], width - x.shape[1]), fill, x.dtype)], axis=1)


def kernel(x_prompt, x_sample, cache_da_k, cache_da_v, cache_mla_ckv, cache_mla_krope, cache_na_k, cache_na_v, c, c_ctx, w_ada, b_ada, w_in, da_lq1, da_lk1, da_lq2, da_lk2, da_subln, mla_gq, mla_gkv, mla_wuq, mla_wukv, na_rpb, w_out, ln1_g, ln1_b, ln2_g, ln2_b, ffn_w1, ffn_w3, ffn_w2, moe_router, moe_w1, moe_w3, moe_w2):
    nbp, seq, d = x_prompt.shape
    nbs, s_lat, _ = x_sample.shape
    depth = w_in.shape[0]
    past = cache_da_k.shape[2]
    mp, ms = nbp * seq, nbs * s_lat
    m = mp + ms
    tok = _Tok(mp, s_lat, nbs)
    tm = 512
    assert mp % s_lat == 0 and s_lat % tm == 0 and mp % tm == 0 and seq % LANE == 0 and nbs + 1 <= COND_ROWS
    alpha = (2.0 * depth) ** 0.25
    rows_n = s_lat // GRID_W

    cond = jnp.concatenate([c_ctx[None], c, jnp.zeros((COND_ROWS - 1 - nbs, d), F32)], axis=0)
    mods = _ada(cond, w_ada, b_ada).reshape(depth, COND_ROWS, 1, 6 * d)

    cos, sa, sb = _rope_tables(s_lat, DA_QK, LANE)
    rope_d = tuple(jnp.tile(t, (1, 2)) for t in (cos, sa, sb))
    rope_m = (_pad_lanes(cos, LANE, 1.0), _pad_lanes(sa, LANE, 0.0), _pad_lanes(sb, LANE, 0.0))

    y = jnp.concatenate([x_prompt.reshape(mp, d), x_sample.reshape(ms, d)], axis=0)
    h = _modulate(tok, y, mods, 0, tm)

    split = C_CKV + MLA_KV_RANK + MLA_ROPE
    w_in_p = jnp.concatenate([w_in[:, :, :split], jnp.zeros((depth, d, C_NAQ - split), F32), w_in[:, :, split:]],
                             axis=2).astype(BF16)
    wuq = mla_wuq.reshape(depth, MLA_Q_RANK, MLA_HEADS, MLA_NOPE + MLA_ROPE)
    wuq_p = jnp.concatenate(
        [wuq, jnp.zeros((depth, MLA_Q_RANK, MLA_HEADS, MLA_QK_PAD - MLA_NOPE - MLA_ROPE), F32)],
        axis=3).reshape(depth, MLA_Q_RANK, MLA_HEADS * MLA_QK_PAD).astype(BF16)
    wukv = mla_wukv.reshape(depth, MLA_KV_RANK, MLA_HEADS, MLA_NOPE + MLA_V)
    wk = wukv[..., :MLA_NOPE].reshape(depth, MLA_KV_RANK, MLA_HEADS * MLA_NOPE).astype(BF16)
    wv = wukv[..., MLA_NOPE:].reshape(depth, MLA_KV_RANK, MLA_HEADS * MLA_V).astype(BF16)
    w_out_b = w_out.astype(BF16)
    ffn_w2_b = ffn_w2.astype(BF16)
    n_moe = moe_router.shape[0]
    wr = jnp.concatenate([moe_router, jnp.zeros((n_moe, d, LANE - N_EXPERTS), F32)], axis=2).astype(BF16)
    vec = lambda a: a.reshape(a.shape[0], 1, a.shape[1])
    gq, gkv, gsub = vec(mla_gq), vec(mla_gkv), vec(da_subln)
    lams = (vec(da_lq1), vec(da_lk1), vec(da_lq2), vec(da_lk2))
    ln1g, ln1b, ln2g, ln2b = vec(ln1_g), vec(ln1_b), vec(ln2_g), vec(ln2_b)
    cda_k = cache_da_k.reshape(nbs, depth, past, DA_HEADS * LANE)
    cda_v = cache_da_v.reshape(nbs, depth, past, DA_HEADS * DA_V)
    cna_k = cache_na_k.reshape(nbs, depth, past, NA_HEADS * NA_DIM)
    cna_v = cache_na_v.reshape(nbs, depth, past, NA_HEADS * NA_DIM)
    cache_pack = jnp.concatenate(
        [cache_mla_ckv, cache_mla_krope, jnp.zeros((nbs, depth, past, 512 - MLA_KV_RANK - MLA_ROPE), F32)],
        axis=-1).reshape(nbs * depth * past, 512)
    bias = _na_bias_tables(na_rpb, rows_n)
    tm_big = 1024 if m % 1024 == 0 else tm

    st = [[] for _ in range(6)]
    for l in range(depth):
        lam_init = 0.8 - 0.6 * math.exp(-0.3 * l)
        proj = _mm(h, w_in_p, l, tm_big, 1024)

        kcat_p, vb_p, ckvn_p, qcat_p = _mla_prep(proj, l, 0, mp, tm, gkv, wk, wv, gq, wuq_p)
        oa_p = _da(proj, proj, proj, lams, gsub, l, lam_init, nb=nbp, sq=seq, sk=seq, tq=seq,
                   qcol0=C_DAQ, kcol0=C_DAK, vcol0=C_DAV)
        ob_p = _attn(qcat_p, kcat_p, vb_p, nb=nbp, sq=seq, sk=seq, tq=seq, heads=MLA_HEADS, hp=MLA_HEADS,
                     dk=MLA_QK_PAD, dv=MLA_V, scale=MLA_SCALE)
        oc_p = _attn(proj, proj, proj, nb=nbp, sq=seq, sk=seq, tq=seq, heads=NA_HEADS, hp=NA_HEADS,
                     dk=NA_DIM, dv=NA_DIM, scale=NA_DIM ** -0.5, qcol0=C_NAQ, kcol0=C_NAK, vcol0=C_NAV)

        kcat_s, vb_s, _, qcat_s, qa_s, ka_s = _mla_prep(proj, l, mp, ms, tm, gkv, wk, wv, gq, wuq_p,
                                                      rope_m=rope_m, rope_d=rope_d, s_lat=s_lat)
        kcat_c, vb_c, _ = _mla_prep(cache_pack, l, l * past, nbs * past, past, gkv, wk, wv, norm=False,
                                    row_step=depth)
        tq = 512 if s_lat % 512 == 0 else s_lat
        oa_s = _da(qa_s, ka_s, proj, lams, gsub, l, lam_init, nb=nbs, sq=s_lat, sk=s_lat, tq=tq,
                   vrow0=mp, vcol0=C_DAV, kc=cda_k, vc=cda_v, skc=past)
        ob_s = _attn(qcat_s, kcat_s, vb_s, nb=nbs, sq=s_lat, sk=s_lat, tq=tq, heads=MLA_HEADS, hp=1,
                     dk=MLA_QK_PAD, dv=MLA_V, scale=MLA_SCALE, kc=kcat_c, vc=vb_c, skc=past)
        oc_s = _na_latent(proj, l, mp, nbs, s_lat, cna_k, cna_v, bias)

        oa = jnp.concatenate([oa_p, oa_s], axis=0)
        ob = jnp.concatenate([ob_p, ob_s], axis=0)
        oc = jnp.concatenate([oc_p, oc_s], axis=0)

        i = l // 2
        moe = (l % 2 == 1)
        res = _mm_ln(tok, [oa, ob, oc], w_out_b, l, y, mods, l, 2, ln1g, ln1b, alpha, tm, d, h_mod=(l, 3, 4),
                     w_router=wr if moe else None, router_layer=i, name="out_proj_ln")
        y, h2 = res[0], res[1]

        nxt = (l + 1, 0, 1) if l + 1 < depth else None
        if not moe:
            a = _ffn_up(h2, ffn_w1, ffn_w3, i, tm_big, 512)
            res = _mm_ln(tok, [a], ffn_w2_b, i, y, mods, l, 5, ln2g, ln2b, alpha, tm, FFN_DOWN_TK, h_mod=nxt,
                         name="ffn_down_ln")
        else:
            plan = _moe_plan(res[2], res[3], m, tm)
            xs, grow = _moe_gather(h2, plan, tm)
            a = _moe_up(xs, grow, moe_w1, moe_w3, i, plan, 256)
            ys = _moe_down(a, moe_w2, i, plan, 512)
            res = _moe_combine_ln(tok, ys, plan, y, mods, l, 5, ln2g, ln2b, alpha, tm, h_mod=nxt)
        y = res[0]
        if nxt is not None:
            h = res[1]

        pp = proj[:mp]
        st[0].append(pp[:, C_DAK:C_DAK + 512].reshape(nbp, seq, DA_HEADS, 2, DA_QK))
        st[1].append(pp[:, C_DAV:C_DAV + 512].reshape(nbp, seq, DA_HEADS, DA_V))
        st[2].append(ckvn_p.reshape(nbp, seq, MLA_KV_RANK))
        st[3].append(pp[:, C_CKV + MLA_KV_RANK:C_CKV + MLA_KV_RANK + MLA_ROPE].reshape(nbp, seq, MLA_ROPE))
        st[4].append(pp[:, C_NAK:C_NAK + 512].reshape(nbp, seq, NA_HEADS, NA_DIM))
        st[5].append(pp[:, C_NAV:C_NAV + 512].reshape(nbp, seq, NA_HEADS, NA_DIM))

    y_p = y[:mp].reshape(nbp, seq, d)
    y_s = y[mp:].reshape(nbs, s_lat, d)
    return (y_p, y_s) + tuple(jnp.stack(s, axis=1) for s in st)
```

```python
import functools
import math

import numpy as np
import jax
import jax.numpy as jnp
from jax import lax
from jax.experimental import pallas as pl
from jax.experimental.pallas import tpu as pltpu

F32 = jnp.float32
BF16 = jnp.bfloat16

GRID_W = 64
DA_QK = 64
DA_V = 128
DA_HEADS = 4
MLA_Q_RANK = 512
MLA_KV_RANK = 256
MLA_NOPE = 128
MLA_ROPE = 64
MLA_V = 128
MLA_HEADS = 8
MLA_SCALE = (MLA_NOPE + MLA_ROPE) ** -0.5
NA_DIM = 128
NA_HEADS = 4
NA_KH = 8
NA_KW = 16
N_EXPERTS = 8
ROPE_THETA = 10000.0
LN_EPS = 1e-5
RMS_EPS = 1e-6
NEG_INF = -1e30
LOG2E = 1.4426950408889634
LANE = 128
COND_ROWS = 8
VMEM_LIMIT = 56 * 1024 * 1024

C_DAQ, C_DAK, C_DAV, C_CQ, C_CKV, C_NAQ, C_NAK, C_NAV = 0, 512, 1024, 1536, 2048, 2560, 3072, 3584
P_COLS = 4096
MLA_QK_PAD = 256
FFN_DOWN_TK = 1408
NA_ROWS_PER_STEP = 4
MLA_HEADS_PER_STEP = 4


def _cparams(sem):
    return pltpu.CompilerParams(dimension_semantics=sem, vmem_limit_bytes=VMEM_LIMIT)


def _dot(a, b):
    return jnp.dot(a, b, preferred_element_type=F32)


def _dot_nt(a, b):
    return lax.dot_general(a, b, (((1,), (1,)), ((), ())), preferred_element_type=F32)


def _silu(x):
    return x * (1.0 / (1.0 + jnp.exp(-x)))


def _rms(x, g):
    return x * lax.rsqrt(jnp.mean(x * x, axis=-1, keepdims=True) + RMS_EPS) * g


def _rope(x, cos, sa, sb):
    return x * cos + pltpu.roll(x, LANE - 16, 1) * sa + pltpu.roll(x, 16, 1) * sb


def _ada_kernel(c_ref, w_ref, b_ref, o_ref):
    s = _silu(c_ref[...]).astype(BF16)
    o_ref[...] = _dot(s, w_ref[...].astype(BF16)) + b_ref[...]


def _ada(cond, w_ada, b_ada):
    n_layers, d, n = w_ada.shape
    tn = 1024
    return pl.pallas_call(
        _ada_kernel,
        out_shape=jax.ShapeDtypeStruct((n_layers, COND_ROWS, n), F32),
        grid=(n_layers, n // tn),
        in_specs=[pl.BlockSpec((COND_ROWS, d), lambda l, j: (0, 0)),
                  pl.BlockSpec((None, d, tn), lambda l, j: (l, 0, j)),
                  pl.BlockSpec((None, 1, tn), lambda l, j: (l, 0, j))],
        out_specs=pl.BlockSpec((None, COND_ROWS, tn), lambda l, j: (l, 0, j)),
        compiler_params=_cparams(("arbitrary", "arbitrary")), name="ada",
    )(cond, w_ada, b_ada.reshape(n_layers, 1, n))


class _Tok:
    def __init__(self, mp, s_lat, n_lat_batches):
        self.mp, self.s_lat, self.nb = mp, s_lat, n_lat_batches
        self.m = mp + s_lat * n_lat_batches

    def rid(self, i, tm):
        r0 = i * tm
        return jnp.where(r0 < self.mp, 0, 1 + (r0 - self.mp) // self.s_lat)

    def mod_spec(self, layer, which, tm, d, grid_rank=1):
        if grid_rank == 1:
            return pl.BlockSpec((None, None, 1, d), lambda i: (layer, self.rid(i, tm), 0, which))
        return pl.BlockSpec((None, None, 1, d), lambda i, k: (layer, self.rid(i, tm), 0, which))


def _modulate_kernel(x_ref, sh_ref, sc_ref, o_ref):
    o_ref[...] = (x_ref[...] * (1.0 + sc_ref[...]) + sh_ref[...]).astype(BF16)


def _modulate(tok, y, mods, layer, tm):
    m, d = y.shape
    return pl.pallas_call(
        _modulate_kernel,
        out_shape=jax.ShapeDtypeStruct((m, d), BF16),
        grid=(m // tm,),
        in_specs=[pl.BlockSpec((tm, d), lambda i: (i, 0)),
                  tok.mod_spec(layer, 0, tm, d), tok.mod_spec(layer, 1, tm, d)],
        out_specs=pl.BlockSpec((tm, d), lambda i: (i, 0)),
        compiler_params=_cparams(("arbitrary",)), name="modulate",
    )(y, mods, mods)


def _mm_kernel(x_ref, w_ref, o_ref):
    o_ref[...] = _dot(x_ref[...], w_ref[...])


def _layer_spec(arr, layer):
    zeros = (0,) * (arr.ndim - 1)
    return pl.BlockSpec((None,) + arr.shape[1:], lambda *_: (layer,) + zeros)


def _mm(x, w, layer, tm, tn):
    m, k = x.shape
    n = w.shape[2]
    return pl.pallas_call(
        _mm_kernel,
        out_shape=jax.ShapeDtypeStruct((m, n), F32),
        grid=(n // tn, m // tm),
        in_specs=[pl.BlockSpec((tm, k), lambda j, i: (i, 0)),
                  pl.BlockSpec((None, k, tn), lambda j, i: (layer, 0, j))],
        out_specs=pl.BlockSpec((tm, tn), lambda j, i: (i, j)),
        compiler_params=_cparams(("arbitrary", "arbitrary")), name="in_proj",
    )(x, w)


def _mla_prep_kernel(*refs, norm, with_q, rope, da_rope):
    it = iter(refs)
    ckv_ref = next(it)
    cq_ref = next(it) if with_q else None
    daq_ref = next(it) if da_rope else None
    dak_ref = next(it) if da_rope else None
    gkv_ref, wk_ref, wv_ref = next(it), next(it), next(it)
    gq_ref = next(it) if with_q else None
    wuq_ref = next(it) if with_q else None
    if rope:
        cm_ref, sam_ref, sbm_ref = next(it), next(it), next(it)
    if da_rope:
        cd_ref, sad_ref, sbd_ref = next(it), next(it), next(it)
    kcat_ref, vb_ref, ckvn_ref = next(it), next(it), next(it)
    qcat_ref = next(it) if with_q else None
    qa_ref = next(it) if da_rope else None
    ka_ref = next(it) if da_rope else None

    ckv = ckv_ref[:, :MLA_KV_RANK]
    ckvn = _rms(ckv, gkv_ref[...]) if norm else ckv
    ckvn_ref[...] = ckvn
    kr = ckv_ref[:, MLA_KV_RANK:MLA_KV_RANK + LANE]
    if rope:
        kr = _rope(kr, cm_ref[...], sam_ref[...], sbm_ref[...])
    kr = kr.astype(BF16)
    cb = ckvn.astype(BF16)
    kn = _dot(cb, wk_ref[...]).astype(BF16)
    vb_ref[...] = _dot(cb, wv_ref[...]).astype(BF16)
    for h in range(MLA_HEADS):
        kcat_ref[:, h * MLA_QK_PAD:h * MLA_QK_PAD + MLA_NOPE] = kn[:, h * MLA_NOPE:(h + 1) * MLA_NOPE]
        kcat_ref[:, h * MLA_QK_PAD + MLA_NOPE:(h + 1) * MLA_QK_PAD] = kr
    if with_q:
        cqn = _rms(cq_ref[...], gq_ref[...]).astype(BF16)
        q = _dot(cqn, wuq_ref[...])
        for h in range(MLA_HEADS):
            lo = h * MLA_QK_PAD
            qcat_ref[:, lo:lo + MLA_NOPE] = q[:, lo:lo + MLA_NOPE].astype(BF16)
            qr = q[:, lo + MLA_NOPE:lo + MLA_QK_PAD]
            if rope:
                qr = _rope(qr, cm_ref[...], sam_ref[...], sbm_ref[...])
            qcat_ref[:, lo + MLA_NOPE:lo + MLA_QK_PAD] = qr.astype(BF16)
    if da_rope:
        for h in range(DA_HEADS):
            sl = slice(h * LANE, (h + 1) * LANE)
            qa_ref[:, sl] = _rope(daq_ref[:, sl], cd_ref[...], sad_ref[...], sbd_ref[...]).astype(BF16)
            ka_ref[:, sl] = _rope(dak_ref[:, sl], cd_ref[...], sad_ref[...], sbd_ref[...]).astype(BF16)


def _mla_prep(src, layer, row0, rows, tm, gkv, wk, wv, gq=None, wuq=None, rope_m=None, rope_d=None,
              norm=True, s_lat=None, row_step=1):
    with_q = wuq is not None
    rope = rope_m is not None
    da_rope = rope_d is not None
    b0 = row0 // tm
    n = rows // tm
    ckv_blk = C_CKV // 512 if src.shape[1] == P_COLS else 0
    blk = lambda c: pl.BlockSpec((tm, 512), lambda i: (b0 + i * row_step, c))
    ins, specs = [src], [blk(ckv_blk)]
    if with_q:
        ins.append(src)
        specs.append(blk(C_CQ // 512))
    if da_rope:
        ins += [src, src]
        specs += [blk(C_DAQ // 512), blk(C_DAK // 512)]
    ins += [gkv, wk, wv]
    specs += [_layer_spec(gkv, layer), _layer_spec(wk, layer), _layer_spec(wv, layer)]
    if with_q:
        ins += [gq, wuq]
        specs += [_layer_spec(gq, layer), _layer_spec(wuq, layer)]
    nt = (s_lat // tm) if rope else 1
    if rope:
        ins += list(rope_m)
        specs += [pl.BlockSpec((tm, LANE), lambda i: (i % nt, 0))] * 3
    if da_rope:
        ins += list(rope_d)
        specs += [pl.BlockSpec((tm, LANE), lambda i: (i % nt, 0))] * 3
    outs = [jax.ShapeDtypeStruct((rows, MLA_HEADS * MLA_QK_PAD), BF16),
            jax.ShapeDtypeStruct((rows, MLA_HEADS * MLA_V), BF16),
            jax.ShapeDtypeStruct((rows, MLA_KV_RANK), F32)]
    if with_q:
        outs.append(jax.ShapeDtypeStruct((rows, MLA_HEADS * MLA_QK_PAD), BF16))
    if da_rope:
        outs += [jax.ShapeDtypeStruct((rows, DA_HEADS * LANE), BF16)] * 2
    ospecs = [pl.BlockSpec((tm, o.shape[1]), lambda i: (i, 0)) for o in outs]
    return pl.pallas_call(
        functools.partial(_mla_prep_kernel, norm=norm, with_q=with_q, rope=rope, da_rope=da_rope),
        out_shape=outs, grid=(n,), in_specs=specs, out_specs=ospecs,
        compiler_params=_cparams(("arbitrary",)), name="mla_prep",
    )(*ins)


def _mla_cache_kernel(ckv_ref, kr_ref, wk_ref, wv_ref, kcat_ref, vb_ref):
    cb = ckv_ref[...].astype(BF16)
    kn = _dot(cb, wk_ref[...]).astype(BF16)
    vb_ref[...] = _dot(cb, wv_ref[...]).astype(BF16)
    kr = kr_ref[...].astype(BF16)
    for h in range(MLA_HEADS):
        kcat_ref[:, h * MLA_QK_PAD:h * MLA_QK_PAD + MLA_NOPE] = kn[:, h * MLA_NOPE:(h + 1) * MLA_NOPE]
        kcat_ref[:, h * MLA_QK_PAD + MLA_NOPE:(h + 1) * MLA_QK_PAD] = kr


def _mla_cache(ckv, kr, layer, depth, nb, past, wk, wv):
    rows = lambda w: pl.BlockSpec((past, w), lambda b: (b * depth + layer, 0))
    out = lambda w: pl.BlockSpec((past, w), lambda b: (b, 0))
    return pl.pallas_call(
        _mla_cache_kernel,
        out_shape=[jax.ShapeDtypeStruct((nb * past, MLA_HEADS * MLA_QK_PAD), BF16),
                   jax.ShapeDtypeStruct((nb * past, MLA_HEADS * MLA_V), BF16)],
        grid=(nb,),
        in_specs=[rows(MLA_KV_RANK), rows(LANE), _layer_spec(wk, layer), _layer_spec(wv, layer)],
        out_specs=[out(MLA_HEADS * MLA_QK_PAD), out(MLA_HEADS * MLA_V)],
        compiler_params=_cparams(("arbitrary",)), name="mla_cache",
    )(ckv, kr, wk, wv)


def _softmax_parts(s, s2, scale):
    c = scale * LOG2E
    m = jnp.max(s, axis=-1, keepdims=True)
    if s2 is not None:
        m = jnp.maximum(m, jnp.max(s2, axis=-1, keepdims=True))
    e = jnp.exp2((s - m) * c)
    den = jnp.sum(e, axis=-1, keepdims=True)
    e2 = None
    if s2 is not None:
        e2 = jnp.exp2((s2 - m) * c)
        den = den + jnp.sum(e2, axis=-1, keepdims=True)
    return e, e2, 1.0 / den


def _attn_head(q, k, v, scale, kc=None, vc=None):
    s = _dot_nt(q, k)
    s2 = _dot_nt(q, kc) if kc is not None else None
    e, e2, inv = _softmax_parts(s, s2, scale)
    o = _dot(e.astype(BF16), v)
    if kc is not None:
        o = o + _dot(e2.astype(BF16), vc)
    return o * inv


def _attn_kernel(*refs, heads, dk, dv, scale, has_ctx):
    if has_ctx:
        q_ref, k_ref, v_ref, kc_ref, vc_ref, o_ref = refs
    else:
        q_ref, k_ref, v_ref, o_ref = refs
    for h in range(heads):
        ks, vs = slice(h * dk, (h + 1) * dk), slice(h * dv, (h + 1) * dv)
        o = _attn_head(q_ref[:, ks].astype(BF16), k_ref[:, ks].astype(BF16), v_ref[:, vs].astype(BF16), scale,
                       kc_ref[:, ks].astype(BF16) if has_ctx else None,
                       vc_ref[:, vs].astype(BF16) if has_ctx else None)
        o_ref[:, vs] = o.astype(BF16)


def _attn(q, k, v, *, nb, sq, sk, tq, heads, hp, dk, dv, scale, qrow0=0, krow0=0, qcol0=0, kcol0=0, vcol0=0,
          kc=None, vc=None, skc=0):
    nq = sq // tq
    ng = heads // hp
    wq, wv = hp * dk, hp * dv
    has_ctx = kc is not None
    ins = [q, k, v]
    specs = [pl.BlockSpec((tq, wq), lambda b, g, i: (qrow0 // tq + b * nq + i, qcol0 // wq + g)),
             pl.BlockSpec((sk, wq), lambda b, g, i: (krow0 // sk + b, kcol0 // wq + g)),
             pl.BlockSpec((sk, wv), lambda b, g, i: (krow0 // sk + b, vcol0 // wv + g))]
    if has_ctx:
        ins += [kc, vc]
        specs += [pl.BlockSpec((skc, wq), lambda b, g, i: (b, g)),
                  pl.BlockSpec((skc, wv), lambda b, g, i: (b, g))]
    return pl.pallas_call(
        functools.partial(_attn_kernel, heads=hp, dk=dk, dv=dv, scale=scale, has_ctx=has_ctx),
        out_shape=jax.ShapeDtypeStruct((nb * sq, heads * dv), BF16),
        grid=(nb, ng, nq), in_specs=specs,
        out_specs=pl.BlockSpec((tq, wv), lambda b, g, i: (b * nq + i, g)),
        compiler_params=_cparams(("arbitrary", "arbitrary", "arbitrary")), name="attn_h%d" % heads,
    )(*ins)


def _da_lambda(lq1, lk1, lq2, lk2, lam_init):
    return (jnp.exp(jnp.sum(lq1[...] * lk1[...], axis=-1, keepdims=True))
            - jnp.exp(jnp.sum(lq2[...] * lk2[...], axis=-1, keepdims=True)) + lam_init)


def _da_head(q, k, v, lam, g, lam_init, kc=None, vc=None):
    first = lax.broadcasted_iota(jnp.int32, (1, LANE), 1) < DA_QK
    q = q.astype(F32) * (DA_QK ** -0.5)
    qs = (jnp.where(first, q, 0.0).astype(BF16), jnp.where(first, 0.0, q).astype(BF16))
    o = _attn_head(qs[0], k, v, 1.0, kc, vc) - lam * _attn_head(qs[1], k, v, 1.0, kc, vc)
    return _rms(o, g) * (1.0 - lam_init)


def _da_kernel(*refs, has_ctx, lam_init):
    if has_ctx:
        q_ref, k_ref, v_ref, kc_ref, vc_ref, lq1, lk1, lq2, lk2, g_ref, o_ref = refs
    else:
        q_ref, k_ref, v_ref, lq1, lk1, lq2, lk2, g_ref, o_ref = refs
    lam = _da_lambda(lq1, lk1, lq2, lk2, lam_init)
    for h in range(DA_HEADS):
        sl = slice(h * LANE, (h + 1) * LANE)
        o = _da_head(q_ref[:, sl], k_ref[:, sl].astype(BF16), v_ref[:, sl].astype(BF16), lam, g_ref[...],
                     lam_init, kc_ref[:, sl].astype(BF16) if has_ctx else None,
                     vc_ref[:, sl].astype(BF16) if has_ctx else None)
        o_ref[:, sl] = o.astype(BF16)


def _ctx_kernel(p_ref, gkv_ref, wk_ref, wv_ref, gq_ref, wuq_ref, lq1, lk1, lq2, lk2, g_ref, o_ref, ckvn_ref,
                *, lam_init):
    col = lambda c0, h, w=LANE: slice(c0 + h * w, c0 + (h + 1) * w)
    lam = _da_lambda(lq1, lk1, lq2, lk2, lam_init)
    for h in range(DA_HEADS):
        o = _da_head(p_ref[:, col(C_DAQ, h)], p_ref[:, col(C_DAK, h)].astype(BF16),
                     p_ref[:, col(C_DAV, h)].astype(BF16), lam, g_ref[...], lam_init)
        o_ref[:, col(0, h)] = o.astype(BF16)
    ckvn = _rms(p_ref[:, C_CKV:C_CKV + MLA_KV_RANK], gkv_ref[...])
    ckvn_ref[...] = ckvn
    cb = ckvn.astype(BF16)
    kn = _dot(cb, wk_ref[...]).astype(BF16)
    vb = _dot(cb, wv_ref[...]).astype(BF16)
    kr = p_ref[:, C_CKV + MLA_KV_RANK:C_CKV + MLA_KV_RANK + LANE].astype(BF16)
    q = _dot(_rms(p_ref[:, C_CQ:C_CQ + MLA_Q_RANK], gq_ref[...]).astype(BF16), wuq_ref[...]).astype(BF16)
    ob0 = DA_HEADS * DA_V
    for h in range(MLA_HEADS):
        kh = jnp.concatenate([kn[:, col(0, h)], kr], axis=1)
        o = _attn_head(q[:, col(0, h, MLA_QK_PAD)], kh, vb[:, col(0, h)], MLA_SCALE)
        o_ref[:, col(ob0, h)] = o.astype(BF16)
    oc0 = ob0 + MLA_HEADS * MLA_V
    for h in range(NA_HEADS):
        o = _attn_head(p_ref[:, col(C_NAQ, h)].astype(BF16), p_ref[:, col(C_NAK, h)].astype(BF16),
                       p_ref[:, col(C_NAV, h)].astype(BF16), NA_DIM ** -0.5)
        o_ref[:, col(oc0, h)] = o.astype(BF16)


def _ctx_attention(proj, layer, nb, seq, gkv, wk, wv, gq, wuq, lams, g, lam_init):
    d_out = DA_HEADS * DA_V + MLA_HEADS * MLA_V + NA_HEADS * NA_DIM
    params = [gkv, wk, wv, gq, wuq] + list(lams) + [g]
    return pl.pallas_call(
        functools.partial(_ctx_kernel, lam_init=lam_init),
        out_shape=[jax.ShapeDtypeStruct((nb * seq, d_out), BF16),
                   jax.ShapeDtypeStruct((nb * seq, MLA_KV_RANK), F32)],
        grid=(nb,),
        in_specs=[pl.BlockSpec((seq, P_COLS), lambda b: (b, 0))] + [_layer_spec(a, layer) for a in params],
        out_specs=[pl.BlockSpec((seq, d_out), lambda b: (b, 0)),
                   pl.BlockSpec((seq, MLA_KV_RANK), lambda b: (b, 0))],
        compiler_params=_cparams(("arbitrary",)), name="ctx_attention",
    )(proj, *params)


def _da(q, k, v, lams, g, layer, lam_init, *, nb, sq, sk, tq, qrow0=0, krow0=0, qcol0=0, kcol0=0, vcol0=0,
        vrow0=None, kc=None, vc=None, skc=0):
    nq = sq // tq
    w = DA_HEADS * LANE
    vrow0 = krow0 if vrow0 is None else vrow0
    has_ctx = kc is not None
    ins = [q, k, v]
    specs = [pl.BlockSpec((tq, w), lambda b, i: (qrow0 // tq + b * nq + i, qcol0 // w)),
             pl.BlockSpec((sk, w), lambda b, i: (krow0 // sk + b, kcol0 // w)),
             pl.BlockSpec((sk, w), lambda b, i: (vrow0 // sk + b, vcol0 // w))]
    if has_ctx:
        ins += [kc, vc]
        specs += [pl.BlockSpec((None, None, skc, w), lambda b, i: (b, layer, 0, 0))] * 2
    ins += list(lams) + [g]
    specs += [_layer_spec(a, layer) for a in ins[-5:]]
    return pl.pallas_call(
        functools.partial(_da_kernel, has_ctx=has_ctx, lam_init=lam_init),
        out_shape=jax.ShapeDtypeStruct((nb * sq, w), BF16),
        grid=(nb, nq), in_specs=specs,
        out_specs=pl.BlockSpec((tq, w), lambda b, i: (b * nq + i, 0)),
        compiler_params=_cparams(("arbitrary", "arbitrary")), name="diff_attn",
    )(*ins)


def _na_kernel(q_ref, k_ref, v_ref, kc_ref, vc_ref, bias_ref, o_ref, *, rows_n, kh, rows_per_step):
    nwin = kh * GRID_W
    scale = NA_DIM ** -0.5
    for rr in range(rows_per_step):
        r = pl.program_id(1) * rows_per_step + rr
        rs = jnp.clip(r - kh // 2, 0, rows_n - kh)
        start = pl.multiple_of(rs * GRID_W, GRID_W)
        qrows = slice(rr * GRID_W, (rr + 1) * GRID_W)
        for h in range(NA_HEADS):
            sl = slice(h * NA_DIM, (h + 1) * NA_DIM)
            q = q_ref[qrows, sl].astype(BF16)
            kw = k_ref[pl.ds(start, nwin), sl].astype(BF16)
            vw = v_ref[pl.ds(start, nwin), sl].astype(BF16)
            s = _dot_nt(q, kw) * scale + bias_ref[h, r - rs]
            s2 = _dot_nt(q, kc_ref[:, sl].astype(BF16)) * scale
            e, e2, inv = _softmax_parts(s, s2, 1.0)
            o = _dot(e.astype(BF16), vw) + _dot(e2.astype(BF16), vc_ref[:, sl].astype(BF16))
            o_ref[qrows, sl] = (o * inv).astype(BF16)


def _na_latent(proj, layer, row0, nb, s_lat, kc, vc, bias):
    rows_n = s_lat // GRID_W
    kh = min(NA_KH, rows_n)
    w = NA_HEADS * NA_DIM
    skc = kc.shape[2]
    rps = NA_ROWS_PER_STEP if rows_n % NA_ROWS_PER_STEP == 0 else 1
    tq = rps * GRID_W
    steps = rows_n // rps
    return pl.pallas_call(
        functools.partial(_na_kernel, rows_n=rows_n, kh=kh, rows_per_step=rps),
        out_shape=jax.ShapeDtypeStruct((nb * s_lat, w), BF16),
        grid=(nb, steps),
        in_specs=[pl.BlockSpec((tq, w), lambda b, r: (row0 // tq + b * steps + r, C_NAQ // w)),
                  pl.BlockSpec((s_lat, w), lambda b, r: (row0 // s_lat + b, C_NAK // w)),
                  pl.BlockSpec((s_lat, w), lambda b, r: (row0 // s_lat + b, C_NAV // w)),
                  pl.BlockSpec((None, None, skc, w), lambda b, r: (b, layer, 0, 0)),
                  pl.BlockSpec((None, None, skc, w), lambda b, r: (b, layer, 0, 0)),
                  _layer_spec(bias, layer)],
        out_specs=pl.BlockSpec((tq, w), lambda b, r: (b * steps + r, 0)),
        compiler_params=_cparams(("arbitrary", "arbitrary")), name="na_latent",
    )(proj, proj, proj, kc, vc, bias)


def _na_bias_tables(rpb, rows_n):
    kh = min(NA_KH, rows_n)
    qc = np.arange(GRID_W)[:, None]
    kc = np.arange(GRID_W)[None, :]
    col_start = np.clip(qc - NA_KW // 2, 0, GRID_W - NA_KW)
    valid = (kc >= col_start) & (kc < col_start + NA_KW)
    coff = np.clip(kc - qc, -(NA_KW - 1), NA_KW - 1) + (NA_KW - 1)
    onehot = (coff.reshape(-1)[None, :] == np.arange(2 * NA_KW - 1)[:, None]).astype(np.float32)
    n_l = rpb.shape[0]
    t = jnp.einsum('lhrc,cx->lhrx', rpb.astype(F32), jnp.asarray(onehot), precision=lax.Precision.HIGHEST)
    t = t.reshape(n_l, NA_HEADS, 2 * NA_KH - 1, GRID_W, GRID_W)
    t = jnp.where(jnp.asarray(valid)[None, None, None], t, NEG_INF)
    strips = []
    for v in range(kh):
        lo = NA_KH - 1 - v
        s = t[:, :, lo:lo + kh]
        strips.append(jnp.transpose(s, (0, 1, 3, 2, 4)).reshape(n_l, NA_HEADS, GRID_W, kh * GRID_W))
    return jnp.stack(strips, axis=2)


def _ln_epilogue(z, g_ref, b_ref):
    mu = jnp.mean(z, axis=-1, keepdims=True)
    zc = z - mu
    var = jnp.mean(zc * zc, axis=-1, keepdims=True)
    return zc * lax.rsqrt(var + LN_EPS) * g_ref[...] + b_ref[...]


def _router_info(h, wr_ref, cnt_ref):
    n = h.shape[0]
    logits = _dot(h, wr_ref[...])
    lane = lax.broadcasted_iota(jnp.int32, logits.shape, 1)
    lg = jnp.where(lane < N_EXPERTS, logits, -jnp.inf)
    m1 = jnp.max(lg, axis=-1, keepdims=True)
    i1 = jnp.min(jnp.where(lg == m1, lane, LANE), axis=-1, keepdims=True)
    lg2 = jnp.where(lane == i1, -jnp.inf, lg)
    m2 = jnp.max(lg2, axis=-1, keepdims=True)
    i2 = jnp.min(jnp.where(lg2 == m2, lane, LANE), axis=-1, keepdims=True)
    e2 = jnp.exp(m2 - m1)
    inv = 1.0 / (1.0 + e2)
    oh1, oh2 = lane == i1, lane == i2
    o1, o2 = jnp.where(oh1, 1.0, 0.0), jnp.where(oh2, 1.0, 0.0)
    below = lax.broadcasted_iota(jnp.int32, (n, n), 1) < lax.broadcasted_iota(jnp.int32, (n, n), 0)
    tri = jnp.where(below, 1.0, 0.0).astype(BF16)
    p1 = _dot(tri, o1.astype(BF16))
    p2 = _dot(tri, o2.astype(BF16))
    tot1 = jnp.sum(o1, axis=0, keepdims=True)
    tot2 = jnp.sum(o2, axis=0, keepdims=True)
    cnt = cnt_ref[...]
    rank0 = jnp.sum(jnp.where(oh1, cnt + p1, 0.0), axis=-1, keepdims=True)
    rank1 = jnp.sum(jnp.where(oh2, cnt + tot1 + p2, 0.0), axis=-1, keepdims=True)
    cnt_ref[...] = cnt + tot1 + tot2
    cols = (i1.astype(F32), i2.astype(F32), inv, e2 * inv, rank0, rank1)
    info = jnp.zeros(logits.shape, F32)
    for c, v in enumerate(cols):
        info = jnp.where(lane == c, v, info)
    return info


def _mm_ln_kernel(*refs, n_x, n_tail, head_tiles, k_total, tk, alpha, with_h, with_router, mask_k):
    it = iter(refs)
    x_refs = [next(it) for _ in range(n_x)]
    tail_refs = [next(it) for _ in range(n_tail)]
    w_ref, y_ref, gate_ref, lng_ref, lnb_ref = next(it), next(it), next(it), next(it), next(it)
    sh_ref = next(it) if with_h else None
    sc_ref = next(it) if with_h else None
    wr_ref = next(it) if with_router else None
    yo_ref = next(it)
    h_ref = next(it) if with_h else None
    go_ref = next(it) if with_router else None
    ca_ref = next(it) if with_router else None
    acc_ref = next(it)
    cnt_ref = next(it) if with_router else None
    k = pl.program_id(1)
    nk = pl.num_programs(1)
    if with_router:
        @pl.when((pl.program_id(0) == 0) & (k == 0))
        def _():
            cnt_ref[...] = jnp.zeros_like(cnt_ref)
    cat = lambda rs: rs[0][...] if len(rs) == 1 else jnp.concatenate([r[...] for r in rs], axis=1)
    x = cat(x_refs)
    if n_tail:
        x = jnp.where(pl.program_id(0) < head_tiles, x, cat(tail_refs))
    w = w_ref[...]
    if mask_k:
        lim = k_total - k * tk
        x = jnp.where(lax.broadcasted_iota(jnp.int32, x.shape, 1) < lim, x, jnp.zeros_like(x))
        w = jnp.where(lax.broadcasted_iota(jnp.int32, w.shape, 0) < lim, w, jnp.zeros_like(w))
    part = _dot(x, w.astype(BF16))

    @pl.when(k == 0)
    def _():
        acc_ref[...] = part

    @pl.when(k > 0)
    def _():
        acc_ref[...] += part

    @pl.when(k == nk - 1)
    def _():
        z = alpha * y_ref[...] + gate_ref[...] * acc_ref[...]
        y = _ln_epilogue(z, lng_ref, lnb_ref)
        yo_ref[...] = y
        if with_h:
            h = (y * (1.0 + sc_ref[...]) + sh_ref[...]).astype(BF16)
            h_ref[...] = h
            if with_router:
                go_ref[...] = _router_info(h, wr_ref, cnt_ref)
                ca_ref[...] = cnt_ref[...]


def _mm_ln(tok, xs, w, w_layer, y, mods, layer, gate_which, ln_g, ln_b, alpha, tm, tk, h_mod=None,
           w_router=None, router_layer=0, xs_tail=None, name="mm_ln"):
    m, d = y.shape
    k_total = w.shape[1]
    nk = pl.cdiv(k_total, tk)
    with_h = h_mod is not None
    with_router = w_router is not None
    ins, specs = [], []
    xs_tail = xs_tail or []
    head_tiles = xs[0].shape[0] // tm if xs_tail else 0
    for x in xs:
        wx = x.shape[1] if (len(xs) > 1 or xs_tail) else tk
        ins.append(x)
        if xs_tail:
            specs.append(pl.BlockSpec((tm, wx), lambda i, k: (jnp.minimum(i, head_tiles - 1), k)))
        else:
            specs.append(pl.BlockSpec((tm, wx), lambda i, k: (i, k)))
    for x in xs_tail:
        ins.append(x)
        specs.append(pl.BlockSpec((tm, x.shape[1]), lambda i, k: (jnp.maximum(i - head_tiles, 0), k)))
    ins += [w, y, mods, ln_g, ln_b]
    specs += [pl.BlockSpec((None, tk, d), lambda i, k: (w_layer, k, 0)),
              pl.BlockSpec((tm, d), lambda i, k: (i, 0)),
              tok.mod_spec(layer, gate_which, tm, d, 2),
              _layer_spec(ln_g, layer), _layer_spec(ln_b, layer)]
    outs = [jax.ShapeDtypeStruct((m, d), F32)]
    ospecs = [pl.BlockSpec((tm, d), lambda i, k: (i, 0))]
    if with_h:
        hl, hsh, hsc = h_mod
        ins += [mods, mods]
        specs += [tok.mod_spec(hl, hsh, tm, d, 2), tok.mod_spec(hl, hsc, tm, d, 2)]
        outs.append(jax.ShapeDtypeStruct((m, d), BF16))
        ospecs.append(pl.BlockSpec((tm, d), lambda i, k: (i, 0)))
    if with_router:
        ins.append(w_router)
        specs.append(_layer_spec(w_router, router_layer))
        outs += [jax.ShapeDtypeStruct((m, LANE), F32), jax.ShapeDtypeStruct((m // tm, 1, LANE), F32)]
        ospecs += [pl.BlockSpec((tm, LANE), lambda i, k: (i, 0)),
                   pl.BlockSpec((None, 1, LANE), lambda i, k: (i, 0, 0))]
    scratch = [pltpu.VMEM((tm, d), F32)]
    if with_router:
        scratch.append(pltpu.VMEM((1, LANE), F32))
    res = pl.pallas_call(
        functools.partial(_mm_ln_kernel, n_x=len(xs), n_tail=len(xs_tail), head_tiles=head_tiles,
                          k_total=k_total, tk=tk, alpha=alpha, with_h=with_h,
                          with_router=with_router, mask_k=(k_total % tk != 0)),
        out_shape=outs, grid=(m // tm, nk), in_specs=specs, out_specs=ospecs,
        scratch_shapes=scratch,
        compiler_params=_cparams(("arbitrary", "arbitrary")), name=name,
    )(*ins)
    return res


def _ffn_up_kernel(x_ref, w1_ref, w3_ref, o_ref):
    x = x_ref[...]
    a = _silu(_dot(x, w1_ref[...].astype(BF16))) * _dot(x, w3_ref[...].astype(BF16))
    o_ref[...] = a.astype(BF16)


def _ffn_up(x, w1, w3, layer, tm, tf):
    m, d = x.shape
    f = w1.shape[2]
    wspec = pl.BlockSpec((None, d, tf), lambda j, i: (layer, 0, j))
    return pl.pallas_call(
        _ffn_up_kernel,
        out_shape=jax.ShapeDtypeStruct((m, f), BF16),
        grid=(pl.cdiv(f, tf), m // tm),
        in_specs=[pl.BlockSpec((tm, d), lambda j, i: (i, 0)), wspec, wspec],
        out_specs=pl.BlockSpec((tm, tf), lambda j, i: (i, j)),
        compiler_params=_cparams(("arbitrary", "arbitrary")), name="ffn_up",
    )(x, w1, w3)


MOE_ROWS = 1024
MOE_SEL_ROWS = 256


def _moe_plan(info, c_after, m, mc):
    tr, ts = MOE_ROWS, MOE_SEL_ROWS
    e_n = N_EXPERTS
    n_chunks = m // mc
    n_tiles = (2 * m) // tr + e_n
    n_blocks = n_tiles * (tr // ts)
    maxp = n_blocks + e_n * n_chunks
    i32 = jnp.int32
    i1, i2 = info[:, 0].astype(i32), info[:, 1].astype(i32)
    r0, r1 = info[:, 4].astype(i32), info[:, 5].astype(i32)
    ca = c_after[:, 0, :e_n].astype(i32)
    cb = jnp.concatenate([jnp.zeros((1, e_n), i32), ca[:-1]], axis=0)
    counts = ca[-1]
    padded = ((counts + tr - 1) // tr) * tr
    start = jnp.cumsum(padded) - padded
    eid = jnp.arange(e_n, dtype=i32)

    def pick(idx, table):
        return jnp.sum(jnp.where(idx[:, None] == eid[None, :], table[None, :], 0), axis=1)

    pos0 = pick(i1, start) + r0
    pos1 = pick(i2, start) + r1
    def expert_of(row0):
        return jnp.minimum(jnp.sum((row0[:, None] >= (start + padded)[None, :]).astype(i32), axis=1), e_n - 1)

    trow0 = jnp.arange(n_tiles, dtype=i32) * tr
    te = expert_of(trow0)
    tv = trow0 < jnp.sum(padded)
    row0 = jnp.arange(n_blocks, dtype=i32) * ts
    be = expert_of(row0)
    k0 = row0 - pick(be, start)
    k1 = jnp.minimum(k0 + ts, pick(be, counts))
    sel = (be[:, None] == eid[None, :])
    cb_t = jnp.sum(jnp.where(sel[:, None, :], cb[None], 0), axis=2)
    ca_t = jnp.sum(jnp.where(sel[:, None, :], ca[None], 0), axis=2)
    ov = (row0 < jnp.sum(padded))[:, None] & (cb_t < k1[:, None]) & (ca_t > k0[:, None])
    first_chunk = (jnp.arange(n_chunks) == 0)[None, :]
    ov_g = ov | (~jnp.any(ov, axis=1, keepdims=True) & first_chunk)

    def pairs(mask2d, inner):
        flat = mask2d.reshape(-1)
        n = jnp.sum(flat.astype(i32))
        idx = jnp.nonzero(flat, size=maxp, fill_value=0)[0].astype(i32)
        p = jnp.arange(maxp, dtype=i32)
        valid = p < n
        idx = jnp.where(valid, idx, jnp.max(jnp.where(valid, idx, 0)))
        outer, inn = idx // inner, idx % inner
        prev = jnp.concatenate([jnp.full((1,), -1, i32), outer[:-1]])
        nxt = jnp.concatenate([outer[1:], jnp.full((1,), -1, i32)])
        first = valid & (outer != prev)
        last = valid & ((outer != nxt) | (p == n - 1))
        return outer, inn, first.astype(i32), last.astype(i32), valid.astype(i32)

    g_tile, g_chunk, g_first, _, g_valid = pairs(ov_g, n_chunks)
    c_chunk, c_tile, c_first, c_last, c_valid = pairs(ov.T, n_blocks)
    return dict(pos0=pos0, pos1=pos1, g1=info[:, 2], g2=info[:, 3], te=te, tv=tv.astype(i32),
                gather=(g_tile, g_chunk, g_first, g_valid),
                combine=(c_tile, c_chunk, c_first, c_last, c_valid), n_tiles=n_tiles, maxp=maxp)


def _moe_gather_kernel(pt, pc, pf, pv, h_ref, p0_ref, p1_ref, g0_ref, g1_ref, xs_ref, gr_ref):
    p = pl.program_id(0)
    tr, mc = xs_ref.shape[0], h_ref.shape[0]

    @pl.when(pf[p] == 1)
    def _():
        xs_ref[...] = jnp.zeros_like(xs_ref)
        gr_ref[...] = jnp.zeros_like(gr_ref)

    @pl.when(pv[p] == 1)
    def _():
        rows = pt[p] * tr + lax.broadcasted_iota(jnp.int32, (tr, mc), 0)
        m0 = p0_ref[...] == rows
        m1 = p1_ref[...] == rows
        sel = jnp.where(m0 | m1, 1.0, 0.0).astype(BF16)
        xs_ref[...] = (xs_ref[...].astype(F32) + _dot(sel, h_ref[...])).astype(BF16)
        gr_ref[...] += jnp.sum(jnp.where(m0, g0_ref[...], 0.0) + jnp.where(m1, g1_ref[...], 0.0),
                               axis=1, keepdims=True)


def _moe_gather(h, plan, mc):
    m, d = h.shape
    tr = MOE_SEL_ROWS
    rows = plan["n_tiles"] * MOE_ROWS
    row = lambda a: a.reshape(1, m)
    tok_spec = lambda: pl.BlockSpec((1, mc), lambda p, pt, pc, pf, pv: (0, pc[p]))
    return pl.pallas_call(
        _moe_gather_kernel,
        out_shape=[jax.ShapeDtypeStruct((rows, d), BF16), jax.ShapeDtypeStruct((rows, 1), F32)],
        grid_spec=pltpu.PrefetchScalarGridSpec(
            num_scalar_prefetch=4, grid=(plan["maxp"],),
            in_specs=[pl.BlockSpec((mc, d), lambda p, pt, pc, pf, pv: (pc[p], 0)),
                      tok_spec(), tok_spec(), tok_spec(), tok_spec()],
            out_specs=[pl.BlockSpec((tr, d), lambda p, pt, pc, pf, pv: (pt[p], 0)),
                       pl.BlockSpec((tr, 1), lambda p, pt, pc, pf, pv: (pt[p], 0))]),
        compiler_params=_cparams(("arbitrary",)), name="moe_gather",
    )(*plan["gather"], h, row(plan["pos0"]), row(plan["pos1"]), row(plan["g1"]), row(plan["g2"]))


def _moe_up_kernel(te, tv, x_ref, w1_ref, w3_ref, g_ref, o_ref):
    i = pl.program_id(1)

    @pl.when(tv[i] == 1)
    def _():
        x = x_ref[...]
        a = _silu(_dot(x, w1_ref[...].astype(BF16))) * _dot(x, w3_ref[...].astype(BF16))
        o_ref[...] = (a * g_ref[...]).astype(BF16)

    @pl.when(tv[i] == 0)
    def _():
        o_ref[...] = jnp.zeros_like(o_ref)


def _moe_up(xs, grow, w1, w3, layer, plan, tf):
    rows, d = xs.shape
    tr = MOE_ROWS
    fe = w1.shape[3]
    wspec = pl.BlockSpec((None, None, d, tf), lambda j, i, te, tv: (layer, te[i], 0, j))
    return pl.pallas_call(
        _moe_up_kernel,
        out_shape=jax.ShapeDtypeStruct((rows, fe), BF16),
        grid_spec=pltpu.PrefetchScalarGridSpec(
            num_scalar_prefetch=2, grid=(fe // tf, rows // tr),
            in_specs=[pl.BlockSpec((tr, d), lambda j, i, te, tv: (i, 0)), wspec, wspec,
                      pl.BlockSpec((tr, 1), lambda j, i, te, tv: (i, 0))],
            out_specs=pl.BlockSpec((tr, tf), lambda j, i, te, tv: (i, j))),
        compiler_params=_cparams(("arbitrary", "arbitrary")), name="moe_up",
    )(plan["te"], plan["tv"], xs, w1, w3, grow)


def _moe_down_kernel(te, tv, a_ref, w_ref, o_ref):
    i = pl.program_id(1)

    @pl.when(tv[i] == 1)
    def _():
        o_ref[...] = _dot(a_ref[...], w_ref[...].astype(BF16)).astype(BF16)

    @pl.when(tv[i] == 0)
    def _():
        o_ref[...] = jnp.zeros_like(o_ref)


def _moe_down(a, w2, layer, plan, tn):
    rows, fe = a.shape
    tr = MOE_ROWS
    d = w2.shape[3]
    return pl.pallas_call(
        _moe_down_kernel,
        out_shape=jax.ShapeDtypeStruct((rows, d), BF16),
        grid_spec=pltpu.PrefetchScalarGridSpec(
            num_scalar_prefetch=2, grid=(d // tn, rows // tr),
            in_specs=[pl.BlockSpec((tr, fe), lambda n, i, te, tv: (i, 0)),
                      pl.BlockSpec((None, None, fe, tn), lambda n, i, te, tv: (layer, te[i], 0, n))],
            out_specs=pl.BlockSpec((tr, tn), lambda n, i, te, tv: (i, n))),
        compiler_params=_cparams(("arbitrary", "arbitrary")), name="moe_down",
    )(plan["te"], plan["tv"], a, w2)


def _moe_combine_ln_kernel(ct, cc, cf, cl, cv, ys_ref, p0_ref, p1_ref, y_ref, gate_ref, lng_ref, lnb_ref,
                           *rest, alpha, with_h):
    if with_h:
        sh_ref, sc_ref, yo_ref, h_ref, acc_ref = rest
    else:
        yo_ref, acc_ref = rest
    p = pl.program_id(0)
    tr, mc = ys_ref.shape[0], y_ref.shape[0]

    @pl.when(cf[p] == 1)
    def _():
        acc_ref[...] = jnp.zeros_like(acc_ref)

    @pl.when(cv[p] == 1)
    def _():
        cols = ct[p] * tr + lax.broadcasted_iota(jnp.int32, (mc, tr), 1)
        sel = jnp.where((p0_ref[...] == cols) | (p1_ref[...] == cols), 1.0, 0.0).astype(BF16)
        acc_ref[...] += _dot(sel, ys_ref[...])

    @pl.when(cl[p] == 1)
    def _():
        z = alpha * y_ref[...] + gate_ref[...] * acc_ref[...]
        y = _ln_epilogue(z, lng_ref, lnb_ref)
        yo_ref[...] = y
        if with_h:
            h_ref[...] = (y * (1.0 + sc_ref[...]) + sh_ref[...]).astype(BF16)


def _moe_combine_ln(tok, ys, plan, y, mods, layer, gate_which, ln_g, ln_b, alpha, mc, h_mod=None):
    m, d = y.shape
    tr = MOE_SEL_ROWS
    with_h = h_mod is not None
    col = lambda a: a.reshape(m, 1)

    def mod(l, which):
        return pl.BlockSpec((None, None, 1, d),
                            lambda p, ct, cc, cf, cl, cv: (l, tok.rid(cc[p], mc), 0, which))

    chunk = lambda w: pl.BlockSpec((mc, w), lambda p, ct, cc, cf, cl, cv: (cc[p], 0))
    ins = [ys, col(plan["pos0"]), col(plan["pos1"]), y, mods, ln_g, ln_b]
    specs = [pl.BlockSpec((tr, d), lambda p, ct, cc, cf, cl, cv: (ct[p], 0)), chunk(1), chunk(1), chunk(d),
             mod(layer, gate_which), _layer_spec(ln_g, layer), _layer_spec(ln_b, layer)]
    outs = [jax.ShapeDtypeStruct((m, d), F32)]
    ospecs = [chunk(d)]
    if with_h:
        hl, hsh, hsc = h_mod
        ins += [mods, mods]
        specs += [mod(hl, hsh), mod(hl, hsc)]
        outs.append(jax.ShapeDtypeStruct((m, d), BF16))
        ospecs.append(chunk(d))
    return pl.pallas_call(
        functools.partial(_moe_combine_ln_kernel, alpha=alpha, with_h=with_h),
        out_shape=outs,
        grid_spec=pltpu.PrefetchScalarGridSpec(
            num_scalar_prefetch=5, grid=(plan["maxp"],), in_specs=specs, out_specs=ospecs,
            scratch_shapes=[pltpu.VMEM((mc, d), F32)]),
        compiler_params=_cparams(("arbitrary",)), name="moe_combine_ln",
    )(*plan["combine"], *ins)


def _rope_tables(n_tokens, dim, pad_to):
    t = jnp.arange(n_tokens)
    row = (t // GRID_W).astype(F32)
    col = (t % GRID_W).astype(F32)
    half = dim // 2
    inv_freq = ROPE_THETA ** (-jnp.arange(0, half, 2, dtype=F32) / half)
    ar = row[:, None] * inv_freq[None, :]
    ac = col[:, None] * inv_freq[None, :]
    ang = jnp.concatenate([ar, ar, ac, ac], axis=-1)
    cos, sin = jnp.cos(ang), jnp.sin(ang)
    lo = (np.arange(dim) % (dim // 2)) < dim // 4
    sa = jnp.where(lo[None, :], -sin, 0.0)
    sb = jnp.where(lo[None, :], 0.0, sin)
    return cos, sa, sb


def _pad_lanes(x, width, fill):
    return jnp.concatenate([x, jnp.full((x.shape[0], width - x.shape[1]), fill, x.dtype)], axis=1)


def kernel(x_prompt, x_sample, cache_da_k, cache_da_v, cache_mla_ckv, cache_mla_krope, cache_na_k, cache_na_v, c, c_ctx, w_ada, b_ada, w_in, da_lq1, da_lk1, da_lq2, da_lk2, da_subln, mla_gq, mla_gkv, mla_wuq, mla_wukv, na_rpb, w_out, ln1_g, ln1_b, ln2_g, ln2_b, ffn_w1, ffn_w3, ffn_w2, moe_router, moe_w1, moe_w3, moe_w2):
    nbp, seq, d = x_prompt.shape
    nbs, s_lat, _ = x_sample.shape
    depth = w_in.shape[0]
    past = cache_da_k.shape[2]
    mp, ms = nbp * seq, nbs * s_lat
    m = mp + ms
    tok = _Tok(mp, s_lat, nbs)
    tm = 512
    assert mp % s_lat == 0 and s_lat % tm == 0 and mp % tm == 0 and seq % LANE == 0 and nbs + 1 <= COND_ROWS
    alpha = (2.0 * depth) ** 0.25
    rows_n = s_lat // GRID_W

    cond = jnp.concatenate([c_ctx[None], c, jnp.zeros((COND_ROWS - 1 - nbs, d), F32)], axis=0)
    mods = _ada(cond, w_ada, b_ada).reshape(depth, COND_ROWS, 1, 6 * d)

    cos, sa, sb = _rope_tables(s_lat, DA_QK, LANE)
    rope_d = tuple(jnp.tile(t, (1, 2)) for t in (cos, sa, sb))
    rope_m = (_pad_lanes(cos, LANE, 1.0), _pad_lanes(sa, LANE, 0.0), _pad_lanes(sb, LANE, 0.0))

    y = jnp.concatenate([x_prompt.reshape(mp, d), x_sample.reshape(ms, d)], axis=0)
    h = _modulate(tok, y, mods, 0, tm)

    split = C_CKV + MLA_KV_RANK + MLA_ROPE
    w_in_p = jnp.concatenate([w_in[:, :, :split], jnp.zeros((depth, d, C_NAQ - split), F32), w_in[:, :, split:]],
                             axis=2).astype(BF16)
    wuq = mla_wuq.reshape(depth, MLA_Q_RANK, MLA_HEADS, MLA_NOPE + MLA_ROPE)
    wuq_p = jnp.concatenate(
        [wuq, jnp.zeros((depth, MLA_Q_RANK, MLA_HEADS, MLA_QK_PAD - MLA_NOPE - MLA_ROPE), F32)],
        axis=3).reshape(depth, MLA_Q_RANK, MLA_HEADS * MLA_QK_PAD).astype(BF16)
    wukv = mla_wukv.reshape(depth, MLA_KV_RANK, MLA_HEADS, MLA_NOPE + MLA_V)
    wk = wukv[..., :MLA_NOPE].reshape(depth, MLA_KV_RANK, MLA_HEADS * MLA_NOPE).astype(BF16)
    wv = wukv[..., MLA_NOPE:].reshape(depth, MLA_KV_RANK, MLA_HEADS * MLA_V).astype(BF16)
    w_out_b = w_out.astype(BF16)
    ffn_w2_b = ffn_w2.astype(BF16)
    n_moe = moe_router.shape[0]
    wr = jnp.concatenate([moe_router, jnp.zeros((n_moe, d, LANE - N_EXPERTS), F32)], axis=2).astype(BF16)
    vec = lambda a: a.reshape(a.shape[0], 1, a.shape[1])
    gq, gkv, gsub = vec(mla_gq), vec(mla_gkv), vec(da_subln)
    lams = (vec(da_lq1), vec(da_lk1), vec(da_lq2), vec(da_lk2))
    ln1g, ln1b, ln2g, ln2b = vec(ln1_g), vec(ln1_b), vec(ln2_g), vec(ln2_b)
    cda_k = cache_da_k.reshape(nbs, depth, past, DA_HEADS * LANE)
    cda_v = cache_da_v.reshape(nbs, depth, past, DA_HEADS * DA_V)
    cna_k = cache_na_k.reshape(nbs, depth, past, NA_HEADS * NA_DIM)
    cna_v = cache_na_v.reshape(nbs, depth, past, NA_HEADS * NA_DIM)
    c_ckv = cache_mla_ckv.reshape(nbs * depth * past, MLA_KV_RANK)
    c_kr = jnp.concatenate([cache_mla_krope, jnp.zeros((nbs, depth, past, LANE - MLA_ROPE), F32)],
                           axis=-1).reshape(nbs * depth * past, LANE)
    bias = _na_bias_tables(na_rpb, rows_n)
    tm_big = 1024 if m % 1024 == 0 else tm

    st = [[] for _ in range(6)]
    for l in range(depth):
        lam_init = 0.8 - 0.6 * math.exp(-0.3 * l)
        proj = _mm(h, w_in_p, l, tm_big, 1024)

        o_p, ckvn_p = _ctx_attention(proj, l, nbp, seq, gkv, wk, wv, gq, wuq_p, lams, gsub, lam_init)

        kcat_s, vb_s, _, qcat_s, qa_s, ka_s = _mla_prep(proj, l, mp, ms, tm, gkv, wk, wv, gq, wuq_p,
                                                      rope_m=rope_m, rope_d=rope_d, s_lat=s_lat)
        kcat_c, vb_c = _mla_cache(c_ckv, c_kr, l, depth, nbs, past, wk, wv)
        tq = 512 if s_lat % 512 == 0 else s_lat
        oa_s = _da(qa_s, ka_s, proj, lams, gsub, l, lam_init, nb=nbs, sq=s_lat, sk=s_lat, tq=tq,
                   vrow0=mp, vcol0=C_DAV, kc=cda_k, vc=cda_v, skc=past)
        ob_s = _attn(qcat_s, kcat_s, vb_s, nb=nbs, sq=s_lat, sk=s_lat, tq=tq, heads=MLA_HEADS,
                     hp=MLA_HEADS_PER_STEP, dk=MLA_QK_PAD, dv=MLA_V, scale=MLA_SCALE, kc=kcat_c, vc=vb_c,
                     skc=past)
        oc_s = _na_latent(proj, l, mp, nbs, s_lat, cna_k, cna_v, bias)

        i = l // 2
        moe = (l % 2 == 1)
        res = _mm_ln(tok, [o_p], w_out_b, l, y, mods, l, 2, ln1g, ln1b, alpha, tm, d, h_mod=(l, 3, 4),
                     w_router=wr if moe else None, router_layer=i, xs_tail=[oa_s, ob_s, oc_s],
                     name="out_proj_ln")
        y, h2 = res[0], res[1]

        nxt = (l + 1, 0, 1) if l + 1 < depth else None
        if not moe:
            a = _ffn_up(h2, ffn_w1, ffn_w3, i, tm_big, 512)
            res = _mm_ln(tok, [a], ffn_w2_b, i, y, mods, l, 5, ln2g, ln2b, alpha, tm, FFN_DOWN_TK, h_mod=nxt,
                         name="ffn_down_ln")
        else:
            plan = _moe_plan(res[2], res[3], m, tm)
            xs, grow = _moe_gather(h2, plan, tm)
            a = _moe_up(xs, grow, moe_w1, moe_w3, i, plan, 256)
            ys = _moe_down(a, moe_w2, i, plan, 512)
            res = _moe_combine_ln(tok, ys, plan, y, mods, l, 5, ln2g, ln2b, alpha, tm, h_mod=nxt)
        y = res[0]
        if nxt is not None:
            h = res[1]

        pp = proj[:mp]
        st[0].append(pp[:, C_DAK:C_DAK + 512].reshape(nbp, seq, DA_HEADS, 2, DA_QK))
        st[1].append(pp[:, C_DAV:C_DAV + 512].reshape(nbp, seq, DA_HEADS, DA_V))
        st[2].append(ckvn_p.reshape(nbp, seq, MLA_KV_RANK))
        st[3].append(pp[:, C_CKV + MLA_KV_RANK:C_CKV + MLA_KV_RANK + MLA_ROPE].reshape(nbp, seq, MLA_ROPE))
        st[4].append(pp[:, C_NAK:C_NAK + 512].reshape(nbp, seq, NA_HEADS, NA_DIM))
        st[5].append(pp[:, C_NAV:C_NAV + 512].reshape(nbp, seq, NA_HEADS, NA_DIM))

    y_p = y[:mp].reshape(nbp, seq, d)
    y_s = y[mp:].reshape(nbs, s_lat, d)
    return (y_p, y_s) + tuple(jnp.stack(s, axis=1) for s in st)
```

```python
import functools
import math

import numpy as np
import jax
import jax.numpy as jnp
from jax import lax
from jax.experimental import pallas as pl
from jax.experimental.pallas import tpu as pltpu

F32 = jnp.float32
BF16 = jnp.bfloat16

GRID_W = 64
DA_QK = 64
DA_V = 128
DA_HEADS = 4
MLA_Q_RANK = 512
MLA_KV_RANK = 256
MLA_NOPE = 128
MLA_ROPE = 64
MLA_V = 128
MLA_HEADS = 8
MLA_SCALE = (MLA_NOPE + MLA_ROPE) ** -0.5
NA_DIM = 128
NA_HEADS = 4
NA_KH = 8
NA_KW = 16
N_EXPERTS = 8
ROPE_THETA = 10000.0
LN_EPS = 1e-5
RMS_EPS = 1e-6
NEG_INF = -1e30
LOG2E = 1.4426950408889634
LANE = 128
COND_ROWS = 8
VMEM_LIMIT = 56 * 1024 * 1024

C_DAQ, C_DAK, C_DAV, C_CQ, C_CKV, C_NAQ, C_NAK, C_NAV = 0, 512, 1024, 1536, 2048, 2560, 3072, 3584
P_COLS = 4096
MLA_QK_PAD = 256
FFN_DOWN_TK = 640
NA_ROWS_PER_STEP = 4
MLA_HEADS_PER_STEP = 4


def _cparams(sem):
    return pltpu.CompilerParams(dimension_semantics=sem, vmem_limit_bytes=VMEM_LIMIT)


def _dot(a, b):
    return jnp.dot(a, b, preferred_element_type=F32)


def _dot_nt(a, b):
    return lax.dot_general(a, b, (((1,), (1,)), ((), ())), preferred_element_type=F32)


def _silu(x):
    return x * (1.0 / (1.0 + jnp.exp(-x)))


def _rms(x, g):
    return x * lax.rsqrt(jnp.mean(x * x, axis=-1, keepdims=True) + RMS_EPS) * g


def _rope(x, cos, sa, sb):
    return x * cos + pltpu.roll(x, LANE - 16, 1) * sa + pltpu.roll(x, 16, 1) * sb


def _ada_kernel(c_ref, w_ref, b_ref, o_ref):
    s = _silu(c_ref[...]).astype(BF16)
    o_ref[...] = _dot(s, w_ref[...].astype(BF16)) + b_ref[...]


def _ada(cond, w_ada, b_ada):
    n_layers, d, n = w_ada.shape
    tn = 1024
    return pl.pallas_call(
        _ada_kernel,
        out_shape=jax.ShapeDtypeStruct((n_layers, COND_ROWS, n), F32),
        grid=(n_layers, n // tn),
        in_specs=[pl.BlockSpec((COND_ROWS, d), lambda l, j: (0, 0)),
                  pl.BlockSpec((None, d, tn), lambda l, j: (l, 0, j)),
                  pl.BlockSpec((None, 1, tn), lambda l, j: (l, 0, j))],
        out_specs=pl.BlockSpec((None, COND_ROWS, tn), lambda l, j: (l, 0, j)),
        compiler_params=_cparams(("arbitrary", "arbitrary")), name="ada",
    )(cond, w_ada, b_ada.reshape(n_layers, 1, n))


class _Tok:
    def __init__(self, mp, s_lat, n_lat_batches):
        self.mp, self.s_lat, self.nb = mp, s_lat, n_lat_batches
        self.m = mp + s_lat * n_lat_batches

    def rid(self, i, tm):
        r0 = i * tm
        return jnp.where(r0 < self.mp, 0, 1 + (r0 - self.mp) // self.s_lat)

    def mod_spec(self, layer, which, tm, d, grid_rank=1):
        if grid_rank == 1:
            return pl.BlockSpec((None, None, 1, d), lambda i: (layer, self.rid(i, tm), 0, which))
        return pl.BlockSpec((None, None, 1, d), lambda i, k: (layer, self.rid(i, tm), 0, which))


def _head_tail_specs(head, tail, tm, width, rank):
    ht = head.shape[0] // tm
    if rank == 1:
        return ht, [pl.BlockSpec((tm, width), lambda i: (jnp.minimum(i, ht - 1), 0)),
                    pl.BlockSpec((tm, width), lambda i: (jnp.maximum(i - ht, 0), 0))]
    return ht, [pl.BlockSpec((tm, width), lambda i, k: (jnp.minimum(i, ht - 1), 0)),
                pl.BlockSpec((tm, width), lambda i, k: (jnp.maximum(i - ht, 0), 0))]


def _modulate_kernel(xp_ref, xs_ref, sh_ref, sc_ref, o_ref, *, head_tiles):
    x = jnp.where(pl.program_id(0) < head_tiles, xp_ref[...], xs_ref[...])
    o_ref[...] = (x * (1.0 + sc_ref[...]) + sh_ref[...]).astype(BF16)


def _modulate(tok, x_p, x_s, mods, layer, tm):
    d = x_p.shape[1]
    ht, xspecs = _head_tail_specs(x_p, x_s, tm, d, 1)
    return pl.pallas_call(
        functools.partial(_modulate_kernel, head_tiles=ht),
        out_shape=jax.ShapeDtypeStruct((tok.m, d), BF16),
        grid=(tok.m // tm,),
        in_specs=xspecs + [tok.mod_spec(layer, 0, tm, d), tok.mod_spec(layer, 1, tm, d)],
        out_specs=pl.BlockSpec((tm, d), lambda i: (i, 0)),
        compiler_params=_cparams(("arbitrary",)), name="modulate",
    )(x_p, x_s, mods, mods)


def _mm_kernel(x_ref, w_ref, o_ref):
    o_ref[...] = _dot(x_ref[...], w_ref[...])


def _layer_spec(arr, layer):
    zeros = (0,) * (arr.ndim - 1)
    return pl.BlockSpec((None,) + arr.shape[1:], lambda *_: (layer,) + zeros)


def _mm(x, w, layer, tm, tn):
    m, k = x.shape
    n = w.shape[2]
    return pl.pallas_call(
        _mm_kernel,
        out_shape=jax.ShapeDtypeStruct((m, n), F32),
        grid=(n // tn, m // tm),
        in_specs=[pl.BlockSpec((tm, k), lambda j, i: (i, 0)),
                  pl.BlockSpec((None, k, tn), lambda j, i: (layer, 0, j))],
        out_specs=pl.BlockSpec((tm, tn), lambda j, i: (i, j)),
        compiler_params=_cparams(("arbitrary", "arbitrary")), name="in_proj",
    )(x, w)


def _mla_prep_kernel(*refs, norm, with_q, rope, da_rope):
    it = iter(refs)
    ckv_ref = next(it)
    cq_ref = next(it) if with_q else None
    daq_ref = next(it) if da_rope else None
    dak_ref = next(it) if da_rope else None
    gkv_ref, wk_ref, wv_ref = next(it), next(it), next(it)
    gq_ref = next(it) if with_q else None
    wuq_ref = next(it) if with_q else None
    if rope:
        cm_ref, sam_ref, sbm_ref = next(it), next(it), next(it)
    if da_rope:
        cd_ref, sad_ref, sbd_ref = next(it), next(it), next(it)
    kcat_ref, vb_ref, ckvn_ref = next(it), next(it), next(it)
    qcat_ref = next(it) if with_q else None
    qa_ref = next(it) if da_rope else None
    ka_ref = next(it) if da_rope else None

    ckv = ckv_ref[:, :MLA_KV_RANK]
    ckvn = _rms(ckv, gkv_ref[...]) if norm else ckv
    ckvn_ref[...] = ckvn
    kr = ckv_ref[:, MLA_KV_RANK:MLA_KV_RANK + LANE]
    if rope:
        kr = _rope(kr, cm_ref[...], sam_ref[...], sbm_ref[...])
    kr = kr.astype(BF16)
    cb = ckvn.astype(BF16)
    kn = _dot(cb, wk_ref[...]).astype(BF16)
    vb_ref[...] = _dot(cb, wv_ref[...]).astype(BF16)
    for h in range(MLA_HEADS):
        kcat_ref[:, h * MLA_QK_PAD:h * MLA_QK_PAD + MLA_NOPE] = kn[:, h * MLA_NOPE:(h + 1) * MLA_NOPE]
        kcat_ref[:, h * MLA_QK_PAD + MLA_NOPE:(h + 1) * MLA_QK_PAD] = kr
    if with_q:
        cqn = _rms(cq_ref[...], gq_ref[...]).astype(BF16)
        q = _dot(cqn, wuq_ref[...])
        for h in range(MLA_HEADS):
            lo = h * MLA_QK_PAD
            qcat_ref[:, lo:lo + MLA_NOPE] = q[:, lo:lo + MLA_NOPE].astype(BF16)
            qr = q[:, lo + MLA_NOPE:lo + MLA_QK_PAD]
            if rope:
                qr = _rope(qr, cm_ref[...], sam_ref[...], sbm_ref[...])
            qcat_ref[:, lo + MLA_NOPE:lo + MLA_QK_PAD] = qr.astype(BF16)
    if da_rope:
        for h in range(DA_HEADS):
            sl = slice(h * LANE, (h + 1) * LANE)
            qa_ref[:, sl] = _rope(daq_ref[:, sl], cd_ref[...], sad_ref[...], sbd_ref[...]).astype(BF16)
            ka_ref[:, sl] = _rope(dak_ref[:, sl], cd_ref[...], sad_ref[...], sbd_ref[...]).astype(BF16)


def _mla_prep(src, layer, row0, rows, tm, gkv, wk, wv, gq=None, wuq=None, rope_m=None, rope_d=None,
              norm=True, s_lat=None, row_step=1):
    with_q = wuq is not None
    rope = rope_m is not None
    da_rope = rope_d is not None
    b0 = row0 // tm
    n = rows // tm
    ckv_blk = C_CKV // 512 if src.shape[1] == P_COLS else 0
    blk = lambda c: pl.BlockSpec((tm, 512), lambda i: (b0 + i * row_step, c))
    ins, specs = [src], [blk(ckv_blk)]
    if with_q:
        ins.append(src)
        specs.append(blk(C_CQ // 512))
    if da_rope:
        ins += [src, src]
        specs += [blk(C_DAQ // 512), blk(C_DAK // 512)]
    ins += [gkv, wk, wv]
    specs += [_layer_spec(gkv, layer), _layer_spec(wk, layer), _layer_spec(wv, layer)]
    if with_q:
        ins += [gq, wuq]
        specs += [_layer_spec(gq, layer), _layer_spec(wuq, layer)]
    nt = (s_lat // tm) if rope else 1
    if rope:
        ins += list(rope_m)
        specs += [pl.BlockSpec((tm, LANE), lambda i: (i % nt, 0))] * 3
    if da_rope:
        ins += list(rope_d)
        specs += [pl.BlockSpec((tm, LANE), lambda i: (i % nt, 0))] * 3
    outs = [jax.ShapeDtypeStruct((rows, MLA_HEADS * MLA_QK_PAD), BF16),
            jax.ShapeDtypeStruct((rows, MLA_HEADS * MLA_V), BF16),
            jax.ShapeDtypeStruct((rows, MLA_KV_RANK), F32)]
    if with_q:
        outs.append(jax.ShapeDtypeStruct((rows, MLA_HEADS * MLA_QK_PAD), BF16))
    if da_rope:
        outs += [jax.ShapeDtypeStruct((rows, DA_HEADS * LANE), BF16)] * 2
    ospecs = [pl.BlockSpec((tm, o.shape[1]), lambda i: (i, 0)) for o in outs]
    return pl.pallas_call(
        functools.partial(_mla_prep_kernel, norm=norm, with_q=with_q, rope=rope, da_rope=da_rope),
        out_shape=outs, grid=(n,), in_specs=specs, out_specs=ospecs,
        compiler_params=_cparams(("arbitrary",)), name="mla_prep",
    )(*ins)


def _mla_cache_kernel(ckv_ref, kr_ref, wk_ref, wv_ref, kcat_ref, vb_ref):
    cb = ckv_ref[...].astype(BF16)
    kn = _dot(cb, wk_ref[...]).astype(BF16)
    vb_ref[...] = _dot(cb, wv_ref[...]).astype(BF16)
    kr = kr_ref[...].astype(BF16)
    for h in range(MLA_HEADS):
        kcat_ref[:, h * MLA_QK_PAD:h * MLA_QK_PAD + MLA_NOPE] = kn[:, h * MLA_NOPE:(h + 1) * MLA_NOPE]
        kcat_ref[:, h * MLA_QK_PAD + MLA_NOPE:(h + 1) * MLA_QK_PAD] = kr


def _mla_cache(ckv, kr, layer, depth, nb, past, wk, wv):
    rows = lambda w: pl.BlockSpec((past, w), lambda b: (b * depth + layer, 0))
    out = lambda w: pl.BlockSpec((past, w), lambda b: (b, 0))
    return pl.pallas_call(
        _mla_cache_kernel,
        out_shape=[jax.ShapeDtypeStruct((nb * past, MLA_HEADS * MLA_QK_PAD), BF16),
                   jax.ShapeDtypeStruct((nb * past, MLA_HEADS * MLA_V), BF16)],
        grid=(nb,),
        in_specs=[rows(MLA_KV_RANK), rows(LANE), _layer_spec(wk, layer), _layer_spec(wv, layer)],
        out_specs=[out(MLA_HEADS * MLA_QK_PAD), out(MLA_HEADS * MLA_V)],
        compiler_params=_cparams(("arbitrary",)), name="mla_cache",
    )(ckv, kr, wk, wv)


def _softmax_parts(s, s2, scale):
    c = scale * LOG2E
    m = jnp.max(s, axis=-1, keepdims=True)
    if s2 is not None:
        m = jnp.maximum(m, jnp.max(s2, axis=-1, keepdims=True))
    e = jnp.exp2((s - m) * c)
    den = jnp.sum(e, axis=-1, keepdims=True)
    e2 = None
    if s2 is not None:
        e2 = jnp.exp2((s2 - m) * c)
        den = den + jnp.sum(e2, axis=-1, keepdims=True)
    return e, e2, 1.0 / den


def _attn_head(q, k, v, scale, kc=None, vc=None):
    s = _dot_nt(q, k)
    s2 = _dot_nt(q, kc) if kc is not None else None
    e, e2, inv = _softmax_parts(s, s2, scale)
    o = _dot(e.astype(BF16), v)
    if kc is not None:
        o = o + _dot(e2.astype(BF16), vc)
    return o * inv


def _attn_kernel(*refs, heads, dk, dv, scale, has_ctx):
    if has_ctx:
        q_ref, k_ref, v_ref, kc_ref, vc_ref, o_ref = refs
    else:
        q_ref, k_ref, v_ref, o_ref = refs
    for h in range(heads):
        ks, vs = slice(h * dk, (h + 1) * dk), slice(h * dv, (h + 1) * dv)
        o = _attn_head(q_ref[:, ks].astype(BF16), k_ref[:, ks].astype(BF16), v_ref[:, vs].astype(BF16), scale,
                       kc_ref[:, ks].astype(BF16) if has_ctx else None,
                       vc_ref[:, vs].astype(BF16) if has_ctx else None)
        o_ref[:, vs] = o.astype(BF16)


def _attn(q, k, v, *, nb, sq, sk, tq, heads, hp, dk, dv, scale, qrow0=0, krow0=0, qcol0=0, kcol0=0, vcol0=0,
          kc=None, vc=None, skc=0):
    nq = sq // tq
    ng = heads // hp
    wq, wv = hp * dk, hp * dv
    has_ctx = kc is not None
    ins = [q, k, v]
    specs = [pl.BlockSpec((tq, wq), lambda b, g, i: (qrow0 // tq + b * nq + i, qcol0 // wq + g)),
             pl.BlockSpec((sk, wq), lambda b, g, i: (krow0 // sk + b, kcol0 // wq + g)),
             pl.BlockSpec((sk, wv), lambda b, g, i: (krow0 // sk + b, vcol0 // wv + g))]
    if has_ctx:
        ins += [kc, vc]
        specs += [pl.BlockSpec((skc, wq), lambda b, g, i: (b, g)),
                  pl.BlockSpec((skc, wv), lambda b, g, i: (b, g))]
    return pl.pallas_call(
        functools.partial(_attn_kernel, heads=hp, dk=dk, dv=dv, scale=scale, has_ctx=has_ctx),
        out_shape=jax.ShapeDtypeStruct((nb * sq, heads * dv), BF16),
        grid=(nb, ng, nq), in_specs=specs,
        out_specs=pl.BlockSpec((tq, wv), lambda b, g, i: (b * nq + i, g)),
        compiler_params=_cparams(("arbitrary", "arbitrary", "arbitrary")), name="attn_h%d" % heads,
    )(*ins)


def _da_lambda(lq1, lk1, lq2, lk2, lam_init):
    return (jnp.exp(jnp.sum(lq1[...] * lk1[...], axis=-1, keepdims=True))
            - jnp.exp(jnp.sum(lq2[...] * lk2[...], axis=-1, keepdims=True)) + lam_init)


def _da_head(q, k, v, lam, g, lam_init, kc=None, vc=None):
    first = lax.broadcasted_iota(jnp.int32, (1, LANE), 1) < DA_QK
    q = q.astype(F32) * (DA_QK ** -0.5)
    qs = (jnp.where(first, q, 0.0).astype(BF16), jnp.where(first, 0.0, q).astype(BF16))
    o = _attn_head(qs[0], k, v, 1.0, kc, vc) - lam * _attn_head(qs[1], k, v, 1.0, kc, vc)
    return _rms(o, g) * (1.0 - lam_init)


def _da_kernel(*refs, has_ctx, lam_init):
    if has_ctx:
        q_ref, k_ref, v_ref, kc_ref, vc_ref, lq1, lk1, lq2, lk2, g_ref, o_ref = refs
    else:
        q_ref, k_ref, v_ref, lq1, lk1, lq2, lk2, g_ref, o_ref = refs
    lam = _da_lambda(lq1, lk1, lq2, lk2, lam_init)
    for h in range(DA_HEADS):
        sl = slice(h * LANE, (h + 1) * LANE)
        o = _da_head(q_ref[:, sl], k_ref[:, sl].astype(BF16), v_ref[:, sl].astype(BF16), lam, g_ref[...],
                     lam_init, kc_ref[:, sl].astype(BF16) if has_ctx else None,
                     vc_ref[:, sl].astype(BF16) if has_ctx else None)
        o_ref[:, sl] = o.astype(BF16)


def _ctx_kernel(p_ref, gkv_ref, wk_ref, wv_ref, gq_ref, wuq_ref, lq1, lk1, lq2, lk2, g_ref, o_ref, ckvn_ref,
                *, lam_init):
    col = lambda c0, h, w=LANE: slice(c0 + h * w, c0 + (h + 1) * w)
    lam = _da_lambda(lq1, lk1, lq2, lk2, lam_init)
    for h in range(DA_HEADS):
        o = _da_head(p_ref[:, col(C_DAQ, h)], p_ref[:, col(C_DAK, h)].astype(BF16),
                     p_ref[:, col(C_DAV, h)].astype(BF16), lam, g_ref[...], lam_init)
        o_ref[:, col(0, h)] = o.astype(BF16)
    ckvn = _rms(p_ref[:, C_CKV:C_CKV + MLA_KV_RANK], gkv_ref[...])
    ckvn_ref[...] = ckvn
    cb = ckvn.astype(BF16)
    kn = _dot(cb, wk_ref[...]).astype(BF16)
    vb = _dot(cb, wv_ref[...]).astype(BF16)
    kr = p_ref[:, C_CKV + MLA_KV_RANK:C_CKV + MLA_KV_RANK + LANE].astype(BF16)
    q = _dot(_rms(p_ref[:, C_CQ:C_CQ + MLA_Q_RANK], gq_ref[...]).astype(BF16), wuq_ref[...]).astype(BF16)
    ob0 = DA_HEADS * DA_V
    for h in range(MLA_HEADS):
        kh = jnp.concatenate([kn[:, col(0, h)], kr], axis=1)
        o = _attn_head(q[:, col(0, h, MLA_QK_PAD)], kh, vb[:, col(0, h)], MLA_SCALE)
        o_ref[:, col(ob0, h)] = o.astype(BF16)
    oc0 = ob0 + MLA_HEADS * MLA_V
    for h in range(NA_HEADS):
        o = _attn_head(p_ref[:, col(C_NAQ, h)].astype(BF16), p_ref[:, col(C_NAK, h)].astype(BF16),
                       p_ref[:, col(C_NAV, h)].astype(BF16), NA_DIM ** -0.5)
        o_ref[:, col(oc0, h)] = o.astype(BF16)


def _ctx_attention(proj, layer, nb, seq, gkv, wk, wv, gq, wuq, lams, g, lam_init):
    d_out = DA_HEADS * DA_V + MLA_HEADS * MLA_V + NA_HEADS * NA_DIM
    params = [gkv, wk, wv, gq, wuq] + list(lams) + [g]
    return pl.pallas_call(
        functools.partial(_ctx_kernel, lam_init=lam_init),
        out_shape=[jax.ShapeDtypeStruct((nb * seq, d_out), BF16),
                   jax.ShapeDtypeStruct((nb * seq, MLA_KV_RANK), F32)],
        grid=(nb,),
        in_specs=[pl.BlockSpec((seq, P_COLS), lambda b: (b, 0))] + [_layer_spec(a, layer) for a in params],
        out_specs=[pl.BlockSpec((seq, d_out), lambda b: (b, 0)),
                   pl.BlockSpec((seq, MLA_KV_RANK), lambda b: (b, 0))],
        compiler_params=_cparams(("arbitrary",)), name="ctx_attention",
    )(proj, *params)


def _da(q, k, v, lams, g, layer, lam_init, *, nb, sq, sk, tq, qrow0=0, krow0=0, qcol0=0, kcol0=0, vcol0=0,
        vrow0=None, kc=None, vc=None, skc=0):
    nq = sq // tq
    w = DA_HEADS * LANE
    vrow0 = krow0 if vrow0 is None else vrow0
    has_ctx = kc is not None
    ins = [q, k, v]
    specs = [pl.BlockSpec((tq, w), lambda b, i: (qrow0 // tq + b * nq + i, qcol0 // w)),
             pl.BlockSpec((sk, w), lambda b, i: (krow0 // sk + b, kcol0 // w)),
             pl.BlockSpec((sk, w), lambda b, i: (vrow0 // sk + b, vcol0 // w))]
    if has_ctx:
        ins += [kc, vc]
        specs += [pl.BlockSpec((None, None, skc, w), lambda b, i: (b, layer, 0, 0))] * 2
    ins += list(lams) + [g]
    specs += [_layer_spec(a, layer) for a in ins[-5:]]
    return pl.pallas_call(
        functools.partial(_da_kernel, has_ctx=has_ctx, lam_init=lam_init),
        out_shape=jax.ShapeDtypeStruct((nb * sq, w), BF16),
        grid=(nb, nq), in_specs=specs,
        out_specs=pl.BlockSpec((tq, w), lambda b, i: (b * nq + i, 0)),
        compiler_params=_cparams(("arbitrary", "arbitrary")), name="diff_attn",
    )(*ins)


def _na_kernel(q_ref, k_ref, v_ref, kc_ref, vc_ref, bias_ref, o_ref, *, rows_n, kh, rows_per_step):
    nwin = kh * GRID_W
    scale = NA_DIM ** -0.5
    for rr in range(rows_per_step):
        r = pl.program_id(1) * rows_per_step + rr
        rs = jnp.clip(r - kh // 2, 0, rows_n - kh)
        start = pl.multiple_of(rs * GRID_W, GRID_W)
        qrows = slice(rr * GRID_W, (rr + 1) * GRID_W)
        for h in range(NA_HEADS):
            sl = slice(h * NA_DIM, (h + 1) * NA_DIM)
            q = q_ref[qrows, sl].astype(BF16)
            kw = k_ref[pl.ds(start, nwin), sl].astype(BF16)
            vw = v_ref[pl.ds(start, nwin), sl].astype(BF16)
            s = _dot_nt(q, kw) * scale + bias_ref[h, r - rs]
            s2 = _dot_nt(q, kc_ref[:, sl].astype(BF16)) * scale
            e, e2, inv = _softmax_parts(s, s2, 1.0)
            o = _dot(e.astype(BF16), vw) + _dot(e2.astype(BF16), vc_ref[:, sl].astype(BF16))
            o_ref[qrows, sl] = (o * inv).astype(BF16)


def _na_latent(proj, layer, row0, nb, s_lat, kc, vc, bias):
    rows_n = s_lat // GRID_W
    kh = min(NA_KH, rows_n)
    w = NA_HEADS * NA_DIM
    skc = kc.shape[2]
    rps = NA_ROWS_PER_STEP if rows_n % NA_ROWS_PER_STEP == 0 else 1
    tq = rps * GRID_W
    steps = rows_n // rps
    return pl.pallas_call(
        functools.partial(_na_kernel, rows_n=rows_n, kh=kh, rows_per_step=rps),
        out_shape=jax.ShapeDtypeStruct((nb * s_lat, w), BF16),
        grid=(nb, steps),
        in_specs=[pl.BlockSpec((tq, w), lambda b, r: (row0 // tq + b * steps + r, C_NAQ // w)),
                  pl.BlockSpec((s_lat, w), lambda b, r: (row0 // s_lat + b, C_NAK // w)),
                  pl.BlockSpec((s_lat, w), lambda b, r: (row0 // s_lat + b, C_NAV // w)),
                  pl.BlockSpec((None, None, skc, w), lambda b, r: (b, layer, 0, 0)),
                  pl.BlockSpec((None, None, skc, w), lambda b, r: (b, layer, 0, 0)),
                  _layer_spec(bias, layer)],
        out_specs=pl.BlockSpec((tq, w), lambda b, r: (b * steps + r, 0)),
        compiler_params=_cparams(("arbitrary", "arbitrary")), name="na_latent",
    )(proj, proj, proj, kc, vc, bias)


def _na_bias_tables(rpb, rows_n):
    kh = min(NA_KH, rows_n)
    qc = np.arange(GRID_W)[:, None]
    kc = np.arange(GRID_W)[None, :]
    col_start = np.clip(qc - NA_KW // 2, 0, GRID_W - NA_KW)
    valid = (kc >= col_start) & (kc < col_start + NA_KW)
    coff = np.clip(kc - qc, -(NA_KW - 1), NA_KW - 1) + (NA_KW - 1)
    onehot = (coff.reshape(-1)[None, :] == np.arange(2 * NA_KW - 1)[:, None]).astype(np.float32)
    n_l = rpb.shape[0]
    t = jnp.einsum('lhrc,cx->lhrx', rpb.astype(F32), jnp.asarray(onehot), precision=lax.Precision.HIGHEST)
    t = t.reshape(n_l, NA_HEADS, 2 * NA_KH - 1, GRID_W, GRID_W)
    t = jnp.where(jnp.asarray(valid)[None, None, None], t, NEG_INF)
    strips = []
    for v in range(kh):
        lo = NA_KH - 1 - v
        s = t[:, :, lo:lo + kh]
        strips.append(jnp.transpose(s, (0, 1, 3, 2, 4)).reshape(n_l, NA_HEADS, GRID_W, kh * GRID_W))
    return jnp.stack(strips, axis=2)


def _ln_epilogue(z, g_ref, b_ref):
    mu = jnp.mean(z, axis=-1, keepdims=True)
    zc = z - mu
    var = jnp.mean(zc * zc, axis=-1, keepdims=True)
    return zc * lax.rsqrt(var + LN_EPS) * g_ref[...] + b_ref[...]


def _router_info(h, wr_ref, cnt_ref):
    n = h.shape[0]
    logits = _dot(h, wr_ref[...])
    lane = lax.broadcasted_iota(jnp.int32, logits.shape, 1)
    lg = jnp.where(lane < N_EXPERTS, logits, -jnp.inf)
    m1 = jnp.max(lg, axis=-1, keepdims=True)
    i1 = jnp.min(jnp.where(lg == m1, lane, LANE), axis=-1, keepdims=True)
    lg2 = jnp.where(lane == i1, -jnp.inf, lg)
    m2 = jnp.max(lg2, axis=-1, keepdims=True)
    i2 = jnp.min(jnp.where(lg2 == m2, lane, LANE), axis=-1, keepdims=True)
    e2 = jnp.exp(m2 - m1)
    inv = 1.0 / (1.0 + e2)
    oh1, oh2 = lane == i1, lane == i2
    o1, o2 = jnp.where(oh1, 1.0, 0.0), jnp.where(oh2, 1.0, 0.0)
    below = lax.broadcasted_iota(jnp.int32, (n, n), 1) < lax.broadcasted_iota(jnp.int32, (n, n), 0)
    tri = jnp.where(below, 1.0, 0.0).astype(BF16)
    p1 = _dot(tri, o1.astype(BF16))
    p2 = _dot(tri, o2.astype(BF16))
    tot1 = jnp.sum(o1, axis=0, keepdims=True)
    tot2 = jnp.sum(o2, axis=0, keepdims=True)
    cnt = cnt_ref[...]
    rank0 = jnp.sum(jnp.where(oh1, cnt + p1, 0.0), axis=-1, keepdims=True)
    rank1 = jnp.sum(jnp.where(oh2, cnt + tot1 + p2, 0.0), axis=-1, keepdims=True)
    cnt_ref[...] = cnt + tot1 + tot2
    cols = (i1.astype(F32), i2.astype(F32), inv, e2 * inv, rank0, rank1)
    info = jnp.zeros(logits.shape, F32)
    for c, v in enumerate(cols):
        info = jnp.where(lane == c, v, info)
    return info


def _mm_ln_kernel(*refs, n_x, n_tail, head_tiles, y_head_tiles, k_total, tk, alpha, with_h, with_router,
                  mask_k):
    it = iter(refs)
    x_refs = [next(it) for _ in range(n_x)]
    tail_refs = [next(it) for _ in range(n_tail)]
    w_ref, y_ref = next(it), next(it)
    ys_ref = next(it) if y_head_tiles else None
    gate_ref, lng_ref, lnb_ref = next(it), next(it), next(it)
    sh_ref = next(it) if with_h else None
    sc_ref = next(it) if with_h else None
    wr_ref = next(it) if with_router else None
    yo_ref = next(it)
    h_ref = next(it) if with_h else None
    go_ref = next(it) if with_router else None
    ca_ref = next(it) if with_router else None
    acc_ref = next(it)
    cnt_ref = next(it) if with_router else None
    k = pl.program_id(1)
    nk = pl.num_programs(1)
    if with_router:
        @pl.when((pl.program_id(0) == 0) & (k == 0))
        def _():
            cnt_ref[...] = jnp.zeros_like(cnt_ref)
    cat = lambda rs: rs[0][...] if len(rs) == 1 else jnp.concatenate([r[...] for r in rs], axis=1)
    x = cat(x_refs)
    if n_tail:
        x = jnp.where(pl.program_id(0) < head_tiles, x, cat(tail_refs))
    w = w_ref[...]
    if mask_k:
        lim = k_total - k * tk
        x = jnp.where(lax.broadcasted_iota(jnp.int32, x.shape, 1) < lim, x, jnp.zeros_like(x))
        w = jnp.where(lax.broadcasted_iota(jnp.int32, w.shape, 0) < lim, w, jnp.zeros_like(w))
    part = _dot(x, w.astype(BF16))

    @pl.when(k == 0)
    def _():
        acc_ref[...] = part

    @pl.when(k > 0)
    def _():
        acc_ref[...] += part

    @pl.when(k == nk - 1)
    def _():
        y_in = y_ref[...]
        if y_head_tiles:
            y_in = jnp.where(pl.program_id(0) < y_head_tiles, y_in, ys_ref[...])
        z = alpha * y_in + gate_ref[...] * acc_ref[...]
        y = _ln_epilogue(z, lng_ref, lnb_ref)
        yo_ref[...] = y
        if with_h:
            h = (y * (1.0 + sc_ref[...]) + sh_ref[...]).astype(BF16)
            h_ref[...] = h
            if with_router:
                go_ref[...] = _router_info(h, wr_ref, cnt_ref)
                ca_ref[...] = cnt_ref[...]


def _mm_ln(tok, xs, w, w_layer, y, mods, layer, gate_which, ln_g, ln_b, alpha, tm, tk, h_mod=None,
           w_router=None, router_layer=0, xs_tail=None, single_buffer_rows=False, name="mm_ln"):
    m, d = tok.m, w.shape[2]
    k_total = w.shape[1]
    nk = pl.cdiv(k_total, tk)
    with_h = h_mod is not None
    with_router = w_router is not None
    ins, specs = [], []
    xs_tail = xs_tail or []
    head_tiles = xs[0].shape[0] // tm if xs_tail else 0
    for x in xs:
        wx = x.shape[1] if (len(xs) > 1 or xs_tail) else tk
        ins.append(x)
        if xs_tail:
            specs.append(pl.BlockSpec((tm, wx), lambda i, k: (jnp.minimum(i, head_tiles - 1), k)))
        else:
            specs.append(pl.BlockSpec((tm, wx), lambda i, k: (i, k)))
    for x in xs_tail:
        ins.append(x)
        specs.append(pl.BlockSpec((tm, x.shape[1]), lambda i, k: (jnp.maximum(i - head_tiles, 0), k)))
    mode = dict(pipeline_mode=pl.Buffered(1)) if single_buffer_rows else {}
    ins.append(w)
    specs.append(pl.BlockSpec((None, tk, d), lambda i, k: (w_layer, k, 0)))
    y_head_tiles = 0
    if isinstance(y, tuple):
        y_head_tiles, yspecs = _head_tail_specs(y[0], y[1], tm, d, 2)
        ins += list(y)
        specs += yspecs
    else:
        ins.append(y)
        specs.append(pl.BlockSpec((tm, d), lambda i, k: (i, 0), **mode))
    ins += [mods, ln_g, ln_b]
    specs += [tok.mod_spec(layer, gate_which, tm, d, 2),
              _layer_spec(ln_g, layer), _layer_spec(ln_b, layer)]
    outs = [jax.ShapeDtypeStruct((m, d), F32)]
    ospecs = [pl.BlockSpec((tm, d), lambda i, k: (i, 0), **mode)]
    if with_h:
        hl, hsh, hsc = h_mod
        ins += [mods, mods]
        specs += [tok.mod_spec(hl, hsh, tm, d, 2), tok.mod_spec(hl, hsc, tm, d, 2)]
        outs.append(jax.ShapeDtypeStruct((m, d), BF16))
        ospecs.append(pl.BlockSpec((tm, d), lambda i, k: (i, 0), **mode))
    if with_router:
        ins.append(w_router)
        specs.append(_layer_spec(w_router, router_layer))
        outs += [jax.ShapeDtypeStruct((m, LANE), F32), jax.ShapeDtypeStruct((m // tm, 1, LANE), F32)]
        ospecs += [pl.BlockSpec((tm, LANE), lambda i, k: (i, 0)),
                   pl.BlockSpec((None, 1, LANE), lambda i, k: (i, 0, 0))]
    scratch = [pltpu.VMEM((tm, d), F32)]
    if with_router:
        scratch.append(pltpu.VMEM((1, LANE), F32))
    res = pl.pallas_call(
        functools.partial(_mm_ln_kernel, n_x=len(xs), n_tail=len(xs_tail), head_tiles=head_tiles,
                          y_head_tiles=y_head_tiles, k_total=k_total, tk=tk, alpha=alpha, with_h=with_h,
                          with_router=with_router, mask_k=(k_total % tk != 0)),
        out_shape=outs, grid=(m // tm, nk), in_specs=specs, out_specs=ospecs,
        scratch_shapes=scratch,
        compiler_params=_cparams(("arbitrary", "arbitrary")), name=name,
    )(*ins)
    return res


def _ffn_up_kernel(x_ref, w1_ref, w3_ref, o_ref):
    x = x_ref[...]
    a = _silu(_dot(x, w1_ref[...].astype(BF16))) * _dot(x, w3_ref[...].astype(BF16))
    o_ref[...] = a.astype(BF16)


def _ffn_up(x, w1, w3, layer, tm, tf):
    m, d = x.shape
    f = w1.shape[2]
    wspec = pl.BlockSpec((None, d, tf), lambda j, i: (layer, 0, j))
    return pl.pallas_call(
        _ffn_up_kernel,
        out_shape=jax.ShapeDtypeStruct((m, f), BF16),
        grid=(pl.cdiv(f, tf), m // tm),
        in_specs=[pl.BlockSpec((tm, d), lambda j, i: (i, 0)), wspec, wspec],
        out_specs=pl.BlockSpec((tm, tf), lambda j, i: (i, j)),
        compiler_params=_cparams(("arbitrary", "arbitrary")), name="ffn_up",
    )(x, w1, w3)


MOE_ROWS = 1024
MOE_SEL_ROWS = 256


def _moe_plan(info, c_after, m, mc):
    tr, ts = MOE_ROWS, MOE_SEL_ROWS
    e_n = N_EXPERTS
    n_chunks = m // mc
    n_tiles = (2 * m) // tr + e_n
    n_blocks = n_tiles * (tr // ts)
    maxp = n_blocks + e_n * n_chunks
    i32 = jnp.int32
    i1, i2 = info[:, 0].astype(i32), info[:, 1].astype(i32)
    r0, r1 = info[:, 4].astype(i32), info[:, 5].astype(i32)
    ca = c_after[:, 0, :e_n].astype(i32)
    cb = jnp.concatenate([jnp.zeros((1, e_n), i32), ca[:-1]], axis=0)
    counts = ca[-1]
    padded = ((counts + tr - 1) // tr) * tr
    start = jnp.cumsum(padded) - padded
    eid = jnp.arange(e_n, dtype=i32)

    def pick(idx, table):
        return jnp.sum(jnp.where(idx[:, None] == eid[None, :], table[None, :], 0), axis=1)

    pos0 = pick(i1, start) + r0
    pos1 = pick(i2, start) + r1
    def expert_of(row0):
        return jnp.minimum(jnp.sum((row0[:, None] >= (start + padded)[None, :]).astype(i32), axis=1), e_n - 1)

    trow0 = jnp.arange(n_tiles, dtype=i32) * tr
    te = expert_of(trow0)
    tv = trow0 < jnp.sum(padded)
    row0 = jnp.arange(n_blocks, dtype=i32) * ts
    be = expert_of(row0)
    k0 = row0 - pick(be, start)
    k1 = jnp.minimum(k0 + ts, pick(be, counts))
    sel = (be[:, None] == eid[None, :])
    cb_t = jnp.sum(jnp.where(sel[:, None, :], cb[None], 0), axis=2)
    ca_t = jnp.sum(jnp.where(sel[:, None, :], ca[None], 0), axis=2)
    ov = (row0 < jnp.sum(padded))[:, None] & (cb_t < k1[:, None]) & (ca_t > k0[:, None])
    first_chunk = (jnp.arange(n_chunks) == 0)[None, :]
    ov_g = ov | (~jnp.any(ov, axis=1, keepdims=True) & first_chunk)

    def pairs(mask2d, inner):
        flat = mask2d.reshape(-1)
        n = jnp.sum(flat.astype(i32))
        idx = jnp.nonzero(flat, size=maxp, fill_value=0)[0].astype(i32)
        p = jnp.arange(maxp, dtype=i32)
        valid = p < n
        idx = jnp.where(valid, idx, jnp.max(jnp.where(valid, idx, 0)))
        outer, inn = idx // inner, idx % inner
        prev = jnp.concatenate([jnp.full((1,), -1, i32), outer[:-1]])
        nxt = jnp.concatenate([outer[1:], jnp.full((1,), -1, i32)])
        first = valid & (outer != prev)
        last = valid & ((outer != nxt) | (p == n - 1))
        return outer, inn, first.astype(i32), last.astype(i32), valid.astype(i32)

    g_tile, g_chunk, g_first, _, g_valid = pairs(ov_g, n_chunks)
    c_chunk, c_tile, c_first, c_last, c_valid = pairs(ov.T, n_blocks)
    ti = jnp.minimum(jnp.arange(n_tiles, dtype=i32), jnp.sum(tv.astype(i32)) - 1)
    return dict(pos0=pos0, pos1=pos1, g1=info[:, 2], g2=info[:, 3], te=te, tv=tv.astype(i32), ti=ti,
                gather=(g_tile, g_chunk, g_first, g_valid),
                combine=(c_tile, c_chunk, c_first, c_last, c_valid), n_tiles=n_tiles, maxp=maxp)


def _moe_gather_kernel(pt, pc, pf, pv, h_ref, p0_ref, p1_ref, g0_ref, g1_ref, xs_ref, gr_ref):
    p = pl.program_id(0)
    tr, mc = xs_ref.shape[0], h_ref.shape[0]

    @pl.when(pf[p] == 1)
    def _():
        xs_ref[...] = jnp.zeros_like(xs_ref)
        gr_ref[...] = jnp.zeros_like(gr_ref)

    @pl.when(pv[p] == 1)
    def _():
        rows = pt[p] * tr + lax.broadcasted_iota(jnp.int32, (tr, mc), 0)
        m0 = p0_ref[...] == rows
        m1 = p1_ref[...] == rows
        sel = jnp.where(m0 | m1, 1.0, 0.0).astype(BF16)
        xs_ref[...] = (xs_ref[...].astype(F32) + _dot(sel, h_ref[...])).astype(BF16)
        gr_ref[...] += jnp.sum(jnp.where(m0, g0_ref[...], 0.0) + jnp.where(m1, g1_ref[...], 0.0),
                               axis=1, keepdims=True)


def _moe_gather(h, plan, mc):
    m, d = h.shape
    tr = MOE_SEL_ROWS
    rows = plan["n_tiles"] * MOE_ROWS
    row = lambda a: a.reshape(1, m)
    tok_spec = lambda: pl.BlockSpec((1, mc), lambda p, pt, pc, pf, pv: (0, pc[p]))
    return pl.pallas_call(
        _moe_gather_kernel,
        out_shape=[jax.ShapeDtypeStruct((rows, d), BF16), jax.ShapeDtypeStruct((rows, 1), F32)],
        grid_spec=pltpu.PrefetchScalarGridSpec(
            num_scalar_prefetch=4, grid=(plan["maxp"],),
            in_specs=[pl.BlockSpec((mc, d), lambda p, pt, pc, pf, pv: (pc[p], 0)),
                      tok_spec(), tok_spec(), tok_spec(), tok_spec()],
            out_specs=[pl.BlockSpec((tr, d), lambda p, pt, pc, pf, pv: (pt[p], 0)),
                       pl.BlockSpec((tr, 1), lambda p, pt, pc, pf, pv: (pt[p], 0))]),
        compiler_params=_cparams(("arbitrary",)), name="moe_gather",
    )(*plan["gather"], h, row(plan["pos0"]), row(plan["pos1"]), row(plan["g1"]), row(plan["g2"]))


def _moe_up_kernel(te, tv, ti, x_ref, w1_ref, w3_ref, g_ref, o_ref):
    i = pl.program_id(1)

    @pl.when(tv[i] == 1)
    def _():
        x = x_ref[...]
        a = _silu(_dot(x, w1_ref[...].astype(BF16))) * _dot(x, w3_ref[...].astype(BF16))
        o_ref[...] = (a * g_ref[...]).astype(BF16)

    @pl.when(tv[i] == 0)
    def _():
        o_ref[...] = jnp.zeros_like(o_ref)


def _moe_up(xs, grow, w1, w3, layer, plan, tf):
    rows, d = xs.shape
    tr = MOE_ROWS
    fe = w1.shape[3]
    wspec = pl.BlockSpec((None, None, d, tf), lambda j, i, te, tv, ti: (layer, te[i], 0, j))
    return pl.pallas_call(
        _moe_up_kernel,
        out_shape=jax.ShapeDtypeStruct((rows, fe), BF16),
        grid_spec=pltpu.PrefetchScalarGridSpec(
            num_scalar_prefetch=3, grid=(fe // tf, rows // tr),
            in_specs=[pl.BlockSpec((tr, d), lambda j, i, te, tv, ti: (ti[i], 0)), wspec, wspec,
                      pl.BlockSpec((tr, 1), lambda j, i, te, tv, ti: (ti[i], 0))],
            out_specs=pl.BlockSpec((tr, tf), lambda j, i, te, tv, ti: (i, j))),
        compiler_params=_cparams(("arbitrary", "arbitrary")), name="moe_up",
    )(plan["te"], plan["tv"], plan["ti"], xs, w1, w3, grow)


def _moe_down_kernel(te, tv, ti, a_ref, w_ref, o_ref):
    i = pl.program_id(1)

    @pl.when(tv[i] == 1)
    def _():
        o_ref[...] = _dot(a_ref[...], w_ref[...].astype(BF16)).astype(BF16)

    @pl.when(tv[i] == 0)
    def _():
        o_ref[...] = jnp.zeros_like(o_ref)


def _moe_down(a, w2, layer, plan, tn):
    rows, fe = a.shape
    tr = MOE_ROWS
    d = w2.shape[3]
    return pl.pallas_call(
        _moe_down_kernel,
        out_shape=jax.ShapeDtypeStruct((rows, d), BF16),
        grid_spec=pltpu.PrefetchScalarGridSpec(
            num_scalar_prefetch=3, grid=(d // tn, rows // tr),
            in_specs=[pl.BlockSpec((tr, fe), lambda n, i, te, tv, ti: (ti[i], 0)),
                      pl.BlockSpec((None, None, fe, tn), lambda n, i, te, tv, ti: (layer, te[i], 0, n))],
            out_specs=pl.BlockSpec((tr, tn), lambda n, i, te, tv, ti: (i, n))),
        compiler_params=_cparams(("arbitrary", "arbitrary")), name="moe_down",
    )(plan["te"], plan["tv"], plan["ti"], a, w2)


def _moe_combine_ln_kernel(ct, cc, cf, cl, cv, ys_ref, p0_ref, p1_ref, y_ref, gate_ref, lng_ref, lnb_ref,
                           *rest, alpha, with_h, split_chunks):
    yos_ref = None
    if with_h:
        sh_ref, sc_ref, yo_ref, h_ref, acc_ref = rest
    elif split_chunks:
        yo_ref, yos_ref, acc_ref = rest
    else:
        yo_ref, acc_ref = rest
    p = pl.program_id(0)
    tr, mc = ys_ref.shape[0], y_ref.shape[0]

    @pl.when(cf[p] == 1)
    def _():
        acc_ref[...] = jnp.zeros_like(acc_ref)

    @pl.when(cv[p] == 1)
    def _():
        cols = ct[p] * tr + lax.broadcasted_iota(jnp.int32, (mc, tr), 1)
        sel = jnp.where((p0_ref[...] == cols) | (p1_ref[...] == cols), 1.0, 0.0).astype(BF16)
        acc_ref[...] += _dot(sel, ys_ref[...])

    @pl.when(cl[p] == 1)
    def _():
        z = alpha * y_ref[...] + gate_ref[...] * acc_ref[...]
        y = _ln_epilogue(z, lng_ref, lnb_ref)
        if yos_ref is None:
            yo_ref[...] = y
        else:
            @pl.when(cc[p] < split_chunks)
            def _():
                yo_ref[...] = y

            @pl.when(cc[p] >= split_chunks)
            def _():
                yos_ref[...] = y
        if with_h:
            h_ref[...] = (y * (1.0 + sc_ref[...]) + sh_ref[...]).astype(BF16)


def _moe_combine_ln(tok, ys, plan, y, mods, layer, gate_which, ln_g, ln_b, alpha, mc, h_mod=None,
                    split_rows=0):
    m, d = y.shape
    tr = MOE_SEL_ROWS
    with_h = h_mod is not None
    col = lambda a: a.reshape(m, 1)

    def mod(l, which):
        return pl.BlockSpec((None, None, 1, d),
                            lambda p, ct, cc, cf, cl, cv: (l, tok.rid(cc[p], mc), 0, which))

    chunk = lambda w: pl.BlockSpec((mc, w), lambda p, ct, cc, cf, cl, cv: (cc[p], 0))
    ins = [ys, col(plan["pos0"]), col(plan["pos1"]), y, mods, ln_g, ln_b]
    specs = [pl.BlockSpec((tr, d), lambda p, ct, cc, cf, cl, cv: (ct[p], 0)), chunk(1), chunk(1), chunk(d),
             mod(layer, gate_which), _layer_spec(ln_g, layer), _layer_spec(ln_b, layer)]
    outs = [jax.ShapeDtypeStruct((m, d), F32)]
    ospecs = [chunk(d)]
    sc = split_rows // mc
    if sc:
        assert not with_h
        outs = [jax.ShapeDtypeStruct((split_rows, d), F32), jax.ShapeDtypeStruct((m - split_rows, d), F32)]
        ospecs = [pl.BlockSpec((mc, d), lambda p, ct, cc, cf, cl, cv: (jnp.minimum(cc[p], sc - 1), 0)),
                  pl.BlockSpec((mc, d), lambda p, ct, cc, cf, cl, cv: (jnp.maximum(cc[p] - sc, 0), 0))]
    if with_h:
        hl, hsh, hsc = h_mod
        ins += [mods, mods]
        specs += [mod(hl, hsh), mod(hl, hsc)]
        outs.append(jax.ShapeDtypeStruct((m, d), BF16))
        ospecs.append(chunk(d))
    return pl.pallas_call(
        functools.partial(_moe_combine_ln_kernel, alpha=alpha, with_h=with_h, split_chunks=sc),
        out_shape=outs,
        grid_spec=pltpu.PrefetchScalarGridSpec(
            num_scalar_prefetch=5, grid=(plan["maxp"],), in_specs=specs, out_specs=ospecs,
            scratch_shapes=[pltpu.VMEM((mc, d), F32)]),
        compiler_params=_cparams(("arbitrary",)), name="moe_combine_ln",
    )(*plan["combine"], *ins)


def _rope_tables(n_tokens, dim, pad_to):
    t = jnp.arange(n_tokens)
    row = (t // GRID_W).astype(F32)
    col = (t % GRID_W).astype(F32)
    half = dim // 2
    inv_freq = ROPE_THETA ** (-jnp.arange(0, half, 2, dtype=F32) / half)
    ar = row[:, None] * inv_freq[None, :]
    ac = col[:, None] * inv_freq[None, :]
    ang = jnp.concatenate([ar, ar, ac, ac], axis=-1)
    cos, sin = jnp.cos(ang), jnp.sin(ang)
    lo = (np.arange(dim) % (dim // 2)) < dim // 4
    sa = jnp.where(lo[None, :], -sin, 0.0)
    sb = jnp.where(lo[None, :], 0.0, sin)
    return cos, sa, sb


def _pad_lanes(x, width, fill):
    return jnp.concatenate([x, jnp.full((x.shape[0], width - x.shape[1]), fill, x.dtype)], axis=1)


def kernel(x_prompt, x_sample, cache_da_k, cache_da_v, cache_mla_ckv, cache_mla_krope, cache_na_k, cache_na_v, c, c_ctx, w_ada, b_ada, w_in, da_lq1, da_lk1, da_lq2, da_lk2, da_subln, mla_gq, mla_gkv, mla_wuq, mla_wukv, na_rpb, w_out, ln1_g, ln1_b, ln2_g, ln2_b, ffn_w1, ffn_w3, ffn_w2, moe_router, moe_w1, moe_w3, moe_w2):
    nbp, seq, d = x_prompt.shape
    nbs, s_lat, _ = x_sample.shape
    depth = w_in.shape[0]
    past = cache_da_k.shape[2]
    mp, ms = nbp * seq, nbs * s_lat
    m = mp + ms
    tok = _Tok(mp, s_lat, nbs)
    tm = 512
    assert mp % s_lat == 0 and s_lat % tm == 0 and mp % tm == 0 and seq % LANE == 0 and nbs + 1 <= COND_ROWS
    alpha = (2.0 * depth) ** 0.25
    rows_n = s_lat // GRID_W

    cond = jnp.concatenate([c_ctx[None], c, jnp.zeros((COND_ROWS - 1 - nbs, d), F32)], axis=0)
    mods = _ada(cond, w_ada, b_ada).reshape(depth, COND_ROWS, 1, 6 * d)

    cos, sa, sb = _rope_tables(s_lat, DA_QK, LANE)
    rope_d = tuple(jnp.tile(t, (1, 2)) for t in (cos, sa, sb))
    rope_m = (_pad_lanes(cos, LANE, 1.0), _pad_lanes(sa, LANE, 0.0), _pad_lanes(sb, LANE, 0.0))

    y = (x_prompt.reshape(mp, d), x_sample.reshape(ms, d))
    h = _modulate(tok, y[0], y[1], mods, 0, tm)

    split = C_CKV + MLA_KV_RANK + MLA_ROPE
    w_in_p = jnp.concatenate([w_in[:, :, :split], jnp.zeros((depth, d, C_NAQ - split), F32), w_in[:, :, split:]],
                             axis=2).astype(BF16)
    wuq = mla_wuq.reshape(depth, MLA_Q_RANK, MLA_HEADS, MLA_NOPE + MLA_ROPE)
    wuq_p = jnp.concatenate(
        [wuq, jnp.zeros((depth, MLA_Q_RANK, MLA_HEADS, MLA_QK_PAD - MLA_NOPE - MLA_ROPE), F32)],
        axis=3).reshape(depth, MLA_Q_RANK, MLA_HEADS * MLA_QK_PAD).astype(BF16)
    wukv = mla_wukv.reshape(depth, MLA_KV_RANK, MLA_HEADS, MLA_NOPE + MLA_V)
    wk = wukv[..., :MLA_NOPE].reshape(depth, MLA_KV_RANK, MLA_HEADS * MLA_NOPE).astype(BF16)
    wv = wukv[..., MLA_NOPE:].reshape(depth, MLA_KV_RANK, MLA_HEADS * MLA_V).astype(BF16)
    w_out_b = w_out.astype(BF16)
    ffn_w2_b = ffn_w2.astype(BF16)
    n_moe = moe_router.shape[0]
    wr = jnp.concatenate([moe_router, jnp.zeros((n_moe, d, LANE - N_EXPERTS), F32)], axis=2).astype(BF16)
    vec = lambda a: a.reshape(a.shape[0], 1, a.shape[1])
    gq, gkv, gsub = vec(mla_gq), vec(mla_gkv), vec(da_subln)
    lams = (vec(da_lq1), vec(da_lk1), vec(da_lq2), vec(da_lk2))
    ln1g, ln1b, ln2g, ln2b = vec(ln1_g), vec(ln1_b), vec(ln2_g), vec(ln2_b)
    cda_k = cache_da_k.reshape(nbs, depth, past, DA_HEADS * LANE)
    cda_v = cache_da_v.reshape(nbs, depth, past, DA_HEADS * DA_V)
    cna_k = cache_na_k.reshape(nbs, depth, past, NA_HEADS * NA_DIM)
    cna_v = cache_na_v.reshape(nbs, depth, past, NA_HEADS * NA_DIM)
    c_ckv = cache_mla_ckv.reshape(nbs * depth * past, MLA_KV_RANK)
    c_kr = jnp.concatenate([cache_mla_krope, jnp.zeros((nbs, depth, past, LANE - MLA_ROPE), F32)],
                           axis=-1).reshape(nbs * depth * past, LANE)
    bias = _na_bias_tables(na_rpb, rows_n)
    tm_big = 1024 if (mp % 1024 == 0 and s_lat % 1024 == 0) else tm

    st = [[] for _ in range(6)]
    for l in range(depth):
        lam_init = 0.8 - 0.6 * math.exp(-0.3 * l)
        proj = _mm(h, w_in_p, l, tm_big, 1024)

        o_p, ckvn_p = _ctx_attention(proj, l, nbp, seq, gkv, wk, wv, gq, wuq_p, lams, gsub, lam_init)

        kcat_s, vb_s, _, qcat_s, qa_s, ka_s = _mla_prep(proj, l, mp, ms, tm, gkv, wk, wv, gq, wuq_p,
                                                      rope_m=rope_m, rope_d=rope_d, s_lat=s_lat)
        kcat_c, vb_c = _mla_cache(c_ckv, c_kr, l, depth, nbs, past, wk, wv)
        tq = 512 if s_lat % 512 == 0 else s_lat
        oa_s = _da(qa_s, ka_s, proj, lams, gsub, l, lam_init, nb=nbs, sq=s_lat, sk=s_lat, tq=tq,
                   vrow0=mp, vcol0=C_DAV, kc=cda_k, vc=cda_v, skc=past)
        ob_s = _attn(qcat_s, kcat_s, vb_s, nb=nbs, sq=s_lat, sk=s_lat, tq=tq, heads=MLA_HEADS,
                     hp=MLA_HEADS_PER_STEP, dk=MLA_QK_PAD, dv=MLA_V, scale=MLA_SCALE, kc=kcat_c, vc=vb_c,
                     skc=past)
        oc_s = _na_latent(proj, l, mp, nbs, s_lat, cna_k, cna_v, bias)

        i = l // 2
        moe = (l % 2 == 1)
        res = _mm_ln(tok, [o_p], w_out_b, l, y, mods, l, 2, ln1g, ln1b, alpha, tm, d, h_mod=(l, 3, 4),
                     w_router=wr if moe else None, router_layer=i, xs_tail=[oa_s, ob_s, oc_s],
                     name="out_proj_ln")
        y, h2 = res[0], res[1]

        nxt = (l + 1, 0, 1) if l + 1 < depth else None
        if not moe:
            a = _ffn_up(h2, ffn_w1, ffn_w3, i, tm_big, 512)
            res = _mm_ln(tok, [a], ffn_w2_b, i, y, mods, l, 5, ln2g, ln2b, alpha, tm_big, FFN_DOWN_TK,
                         h_mod=nxt, single_buffer_rows=True, name="ffn_down_ln")
        else:
            plan = _moe_plan(res[2], res[3], m, tm)
            xs, grow = _moe_gather(h2, plan, tm)
            a = _moe_up(xs, grow, moe_w1, moe_w3, i, plan, 256)
            ys = _moe_down(a, moe_w2, i, plan, 512)
            res = _moe_combine_ln(tok, ys, plan, y, mods, l, 5, ln2g, ln2b, alpha, tm, h_mod=nxt,
                                  split_rows=0 if nxt is not None else mp)
        y = res[0] if nxt is not None else tuple(res)
        if nxt is not None:
            h = res[1]

        pp = proj[:mp]
        st[0].append(pp[:, C_DAK:C_DAK + 512].reshape(nbp, seq, DA_HEADS, 2, DA_QK))
        st[1].append(pp[:, C_DAV:C_DAV + 512].reshape(nbp, seq, DA_HEADS, DA_V))
        st[2].append(ckvn_p.reshape(nbp, seq, MLA_KV_RANK))
        st[3].append(pp[:, C_CKV + MLA_KV_RANK:C_CKV + MLA_KV_RANK + MLA_ROPE].reshape(nbp, seq, MLA_ROPE))
        st[4].append(pp[:, C_NAK:C_NAK + 512].reshape(nbp, seq, NA_HEADS, NA_DIM))
        st[5].append(pp[:, C_NAV:C_NAV + 512].reshape(nbp, seq, NA_HEADS, NA_DIM))

    if not isinstance(y, tuple):
        y = (y[:mp], y[mp:])
    return (y[0].reshape(nbp, seq, d), y[1].reshape(nbs, s_lat, d)) + tuple(jnp.stack(s, axis=1) for s in st)
```

```python
import functools
import math

import numpy as np
import jax
import jax.numpy as jnp
from jax import lax
from jax.experimental import pallas as pl
from jax.experimental.pallas import tpu as pltpu

F32 = jnp.float32
BF16 = jnp.bfloat16

GRID_W = 64
DA_QK = 64
DA_V = 128
DA_HEADS = 4
MLA_Q_RANK = 512
MLA_KV_RANK = 256
MLA_NOPE = 128
MLA_ROPE = 64
MLA_V = 128
MLA_HEADS = 8
MLA_SCALE = (MLA_NOPE + MLA_ROPE) ** -0.5
NA_DIM = 128
NA_HEADS = 4
NA_KH = 8
NA_KW = 16
N_EXPERTS = 8
ROPE_THETA = 10000.0
LN_EPS = 1e-5
RMS_EPS = 1e-6
NEG_INF = -1e30
LOG2E = 1.4426950408889634
LANE = 128
COND_ROWS = 8
VMEM_LIMIT = 56 * 1024 * 1024

C_DAQ, C_DAK, C_DAV, C_CQ, C_CKV, C_NAQ, C_NAK, C_NAV = 0, 512, 1024, 1536, 2048, 2560, 3072, 3584
P_COLS = 4096
MLA_QK_PAD = 256
FFN_DOWN_TK = 1408
NA_ROWS_PER_STEP = 4
MLA_HEADS_PER_STEP = 4


def _cparams(sem):
    return pltpu.CompilerParams(dimension_semantics=sem, vmem_limit_bytes=VMEM_LIMIT)


def _dot(a, b):
    return jnp.dot(a, b, preferred_element_type=F32)


def _dot_nt(a, b):
    return lax.dot_general(a, b, (((1,), (1,)), ((), ())), preferred_element_type=F32)


def _silu(x):
    return x * (1.0 / (1.0 + jnp.exp(-x)))


def _rms(x, g):
    return x * lax.rsqrt(jnp.mean(x * x, axis=-1, keepdims=True) + RMS_EPS) * g


def _rope(x, cos, sa, sb):
    return x * cos + pltpu.roll(x, LANE - 16, 1) * sa + pltpu.roll(x, 16, 1) * sb


def _ada_kernel(c_ref, w_ref, b_ref, o_ref):
    s = _silu(c_ref[...]).astype(BF16)
    o_ref[...] = _dot(s, w_ref[...].astype(BF16)) + b_ref[...]


def _ada(cond, w_ada, b_ada):
    n_layers, d, n = w_ada.shape
    tn = 1024
    return pl.pallas_call(
        _ada_kernel,
        out_shape=jax.ShapeDtypeStruct((n_layers, COND_ROWS, n), F32),
        grid=(n_layers, n // tn),
        in_specs=[pl.BlockSpec((COND_ROWS, d), lambda l, j: (0, 0)),
                  pl.BlockSpec((None, d, tn), lambda l, j: (l, 0, j)),
                  pl.BlockSpec((None, 1, tn), lambda l, j: (l, 0, j))],
        out_specs=pl.BlockSpec((None, COND_ROWS, tn), lambda l, j: (l, 0, j)),
        compiler_params=_cparams(("arbitrary", "arbitrary")), name="ada",
    )(cond, w_ada, b_ada.reshape(n_layers, 1, n))


class _Tok:
    def __init__(self, mp, s_lat, n_lat_batches):
        self.mp, self.s_lat, self.nb = mp, s_lat, n_lat_batches
        self.m = mp + s_lat * n_lat_batches

    def rid(self, i, tm):
        r0 = i * tm
        return jnp.where(r0 < self.mp, 0, 1 + (r0 - self.mp) // self.s_lat)

    def mod_spec(self, layer, which, tm, d, grid_rank=1):
        if grid_rank == 1:
            return pl.BlockSpec((None, None, 1, d), lambda i: (layer, self.rid(i, tm), 0, which))
        return pl.BlockSpec((None, None, 1, d), lambda i, k: (layer, self.rid(i, tm), 0, which))


def _head_tail_specs(head, tail, tm, width, rank):
    ht = head.shape[0] // tm
    if rank == 1:
        return ht, [pl.BlockSpec((tm, width), lambda i: (jnp.minimum(i, ht - 1), 0)),
                    pl.BlockSpec((tm, width), lambda i: (jnp.maximum(i - ht, 0), 0))]
    return ht, [pl.BlockSpec((tm, width), lambda i, k: (jnp.minimum(i, ht - 1), 0)),
                pl.BlockSpec((tm, width), lambda i, k: (jnp.maximum(i - ht, 0), 0))]


def _modulate_kernel(xp_ref, xs_ref, sh_ref, sc_ref, o_ref, *, head_tiles):
    x = jnp.where(pl.program_id(0) < head_tiles, xp_ref[...], xs_ref[...])
    o_ref[...] = (x * (1.0 + sc_ref[...]) + sh_ref[...]).astype(BF16)


def _modulate(tok, x_p, x_s, mods, layer, tm):
    d = x_p.shape[1]
    ht, xspecs = _head_tail_specs(x_p, x_s, tm, d, 1)
    return pl.pallas_call(
        functools.partial(_modulate_kernel, head_tiles=ht),
        out_shape=jax.ShapeDtypeStruct((tok.m, d), BF16),
        grid=(tok.m // tm,),
        in_specs=xspecs + [tok.mod_spec(layer, 0, tm, d), tok.mod_spec(layer, 1, tm, d)],
        out_specs=pl.BlockSpec((tm, d), lambda i: (i, 0)),
        compiler_params=_cparams(("arbitrary",)), name="modulate",
    )(x_p, x_s, mods, mods)


def _mm_kernel(x_ref, w_ref, o_ref):
    o_ref[...] = _dot(x_ref[...], w_ref[...])


def _layer_spec(arr, layer):
    zeros = (0,) * (arr.ndim - 1)
    return pl.BlockSpec((None,) + arr.shape[1:], lambda *_: (layer,) + zeros)


def _mm(x, w, layer, tm, tn):
    m, k = x.shape
    n = w.shape[2]
    return pl.pallas_call(
        _mm_kernel,
        out_shape=jax.ShapeDtypeStruct((m, n), F32),
        grid=(n // tn, m // tm),
        in_specs=[pl.BlockSpec((tm, k), lambda j, i: (i, 0)),
                  pl.BlockSpec((None, k, tn), lambda j, i: (layer, 0, j))],
        out_specs=pl.BlockSpec((tm, tn), lambda j, i: (i, j)),
        compiler_params=_cparams(("arbitrary", "arbitrary")), name="in_proj",
    )(x, w)


def _mla_prep_kernel(*refs, norm, with_q, rope, da_rope):
    it = iter(refs)
    ckv_ref = next(it)
    cq_ref = next(it) if with_q else None
    daq_ref = next(it) if da_rope else None
    dak_ref = next(it) if da_rope else None
    gkv_ref, wk_ref, wv_ref = next(it), next(it), next(it)
    gq_ref = next(it) if with_q else None
    wuq_ref = next(it) if with_q else None
    if rope:
        cm_ref, sam_ref, sbm_ref = next(it), next(it), next(it)
    if da_rope:
        cd_ref, sad_ref, sbd_ref = next(it), next(it), next(it)
    kcat_ref, vb_ref, ckvn_ref = next(it), next(it), next(it)
    qcat_ref = next(it) if with_q else None
    qa_ref = next(it) if da_rope else None
    ka_ref = next(it) if da_rope else None

    ckv = ckv_ref[:, :MLA_KV_RANK]
    ckvn = _rms(ckv, gkv_ref[...]) if norm else ckv
    ckvn_ref[...] = ckvn
    kr = ckv_ref[:, MLA_KV_RANK:MLA_KV_RANK + LANE]
    if rope:
        kr = _rope(kr, cm_ref[...], sam_ref[...], sbm_ref[...])
    kr = kr.astype(BF16)
    cb = ckvn.astype(BF16)
    kn = _dot(cb, wk_ref[...]).astype(BF16)
    vb_ref[...] = _dot(cb, wv_ref[...]).astype(BF16)
    for h in range(MLA_HEADS):
        kcat_ref[:, h * MLA_QK_PAD:h * MLA_QK_PAD + MLA_NOPE] = kn[:, h * MLA_NOPE:(h + 1) * MLA_NOPE]
        kcat_ref[:, h * MLA_QK_PAD + MLA_NOPE:(h + 1) * MLA_QK_PAD] = kr
    if with_q:
        cqn = _rms(cq_ref[...], gq_ref[...]).astype(BF16)
        q = _dot(cqn, wuq_ref[...])
        for h in range(MLA_HEADS):
            lo = h * MLA_QK_PAD
            qcat_ref[:, lo:lo + MLA_NOPE] = q[:, lo:lo + MLA_NOPE].astype(BF16)
            qr = q[:, lo + MLA_NOPE:lo + MLA_QK_PAD]
            if rope:
                qr = _rope(qr, cm_ref[...], sam_ref[...], sbm_ref[...])
            qcat_ref[:, lo + MLA_NOPE:lo + MLA_QK_PAD] = qr.astype(BF16)
    if da_rope:
        for h in range(DA_HEADS):
            sl = slice(h * LANE, (h + 1) * LANE)
            qa_ref[:, sl] = _rope(daq_ref[:, sl], cd_ref[...], sad_ref[...], sbd_ref[...]).astype(BF16)
            ka_ref[:, sl] = _rope(dak_ref[:, sl], cd_ref[...], sad_ref[...], sbd_ref[...]).astype(BF16)


def _mla_prep(src, layer, row0, rows, tm, gkv, wk, wv, gq=None, wuq=None, rope_m=None, rope_d=None,
              norm=True, s_lat=None, row_step=1):
    with_q = wuq is not None
    rope = rope_m is not None
    da_rope = rope_d is not None
    b0 = row0 // tm
    n = rows // tm
    ckv_blk = C_CKV // 512 if src.shape[1] == P_COLS else 0
    blk = lambda c: pl.BlockSpec((tm, 512), lambda i: (b0 + i * row_step, c))
    ins, specs = [src], [blk(ckv_blk)]
    if with_q:
        ins.append(src)
        specs.append(blk(C_CQ // 512))
    if da_rope:
        ins += [src, src]
        specs += [blk(C_DAQ // 512), blk(C_DAK // 512)]
    ins += [gkv, wk, wv]
    specs += [_layer_spec(gkv, layer), _layer_spec(wk, layer), _layer_spec(wv, layer)]
    if with_q:
        ins += [gq, wuq]
        specs += [_layer_spec(gq, layer), _layer_spec(wuq, layer)]
    nt = (s_lat // tm) if rope else 1
    if rope:
        ins += list(rope_m)
        specs += [pl.BlockSpec((tm, LANE), lambda i: (i % nt, 0))] * 3
    if da_rope:
        ins += list(rope_d)
        specs += [pl.BlockSpec((tm, LANE), lambda i: (i % nt, 0))] * 3
    outs = [jax.ShapeDtypeStruct((rows, MLA_HEADS * MLA_QK_PAD), BF16),
            jax.ShapeDtypeStruct((rows, MLA_HEADS * MLA_V), BF16),
            jax.ShapeDtypeStruct((rows, MLA_KV_RANK), F32)]
    if with_q:
        outs.append(jax.ShapeDtypeStruct((rows, MLA_HEADS * MLA_QK_PAD), BF16))
    if da_rope:
        outs += [jax.ShapeDtypeStruct((rows, DA_HEADS * LANE), BF16)] * 2
    ospecs = [pl.BlockSpec((tm, o.shape[1]), lambda i: (i, 0)) for o in outs]
    return pl.pallas_call(
        functools.partial(_mla_prep_kernel, norm=norm, with_q=with_q, rope=rope, da_rope=da_rope),
        out_shape=outs, grid=(n,), in_specs=specs, out_specs=ospecs,
        compiler_params=_cparams(("arbitrary",)), name="mla_prep",
    )(*ins)


def _mla_cache_kernel(ckv_ref, kr_ref, wk_ref, wv_ref, kcat_ref, vb_ref):
    cb = ckv_ref[...].astype(BF16)
    kn = _dot(cb, wk_ref[...]).astype(BF16)
    vb_ref[...] = _dot(cb, wv_ref[...]).astype(BF16)
    kr = kr_ref[...].astype(BF16)
    for h in range(MLA_HEADS):
        kcat_ref[:, h * MLA_QK_PAD:h * MLA_QK_PAD + MLA_NOPE] = kn[:, h * MLA_NOPE:(h + 1) * MLA_NOPE]
        kcat_ref[:, h * MLA_QK_PAD + MLA_NOPE:(h + 1) * MLA_QK_PAD] = kr


def _mla_cache(ckv, kr, layer, depth, nb, past, wk, wv):
    rows = lambda w: pl.BlockSpec((past, w), lambda b: (b * depth + layer, 0))
    out = lambda w: pl.BlockSpec((past, w), lambda b: (b, 0))
    return pl.pallas_call(
        _mla_cache_kernel,
        out_shape=[jax.ShapeDtypeStruct((nb * past, MLA_HEADS * MLA_QK_PAD), BF16),
                   jax.ShapeDtypeStruct((nb * past, MLA_HEADS * MLA_V), BF16)],
        grid=(nb,),
        in_specs=[rows(MLA_KV_RANK), rows(LANE), _layer_spec(wk, layer), _layer_spec(wv, layer)],
        out_specs=[out(MLA_HEADS * MLA_QK_PAD), out(MLA_HEADS * MLA_V)],
        compiler_params=_cparams(("arbitrary",)), name="mla_cache",
    )(ckv, kr, wk, wv)


def _softmax_parts(s, s2, scale):
    c = scale * LOG2E
    m = jnp.max(s, axis=-1, keepdims=True)
    if s2 is not None:
        m = jnp.maximum(m, jnp.max(s2, axis=-1, keepdims=True))
    e = jnp.exp2((s - m) * c)
    den = jnp.sum(e, axis=-1, keepdims=True)
    e2 = None
    if s2 is not None:
        e2 = jnp.exp2((s2 - m) * c)
        den = den + jnp.sum(e2, axis=-1, keepdims=True)
    return e, e2, 1.0 / den


def _attn_head(q, k, v, scale, kc=None, vc=None):
    s = _dot_nt(q, k)
    s2 = _dot_nt(q, kc) if kc is not None else None
    e, e2, inv = _softmax_parts(s, s2, scale)
    o = _dot(e.astype(BF16), v)
    if kc is not None:
        o = o + _dot(e2.astype(BF16), vc)
    return o * inv


def _attn_kernel(*refs, heads, dk, dv, scale, has_ctx):
    if has_ctx:
        q_ref, k_ref, v_ref, kc_ref, vc_ref, o_ref = refs
    else:
        q_ref, k_ref, v_ref, o_ref = refs
    for h in range(heads):
        ks, vs = slice(h * dk, (h + 1) * dk), slice(h * dv, (h + 1) * dv)
        o = _attn_head(q_ref[:, ks].astype(BF16), k_ref[:, ks].astype(BF16), v_ref[:, vs].astype(BF16), scale,
                       kc_ref[:, ks].astype(BF16) if has_ctx else None,
                       vc_ref[:, vs].astype(BF16) if has_ctx else None)
        o_ref[:, vs] = o.astype(BF16)


def _attn(q, k, v, *, nb, sq, sk, tq, heads, hp, dk, dv, scale, qrow0=0, krow0=0, qcol0=0, kcol0=0, vcol0=0,
          kc=None, vc=None, skc=0):
    nq = sq // tq
    ng = heads // hp
    wq, wv = hp * dk, hp * dv
    has_ctx = kc is not None
    ins = [q, k, v]
    specs = [pl.BlockSpec((tq, wq), lambda b, g, i: (qrow0 // tq + b * nq + i, qcol0 // wq + g)),
             pl.BlockSpec((sk, wq), lambda b, g, i: (krow0 // sk + b, kcol0 // wq + g)),
             pl.BlockSpec((sk, wv), lambda b, g, i: (krow0 // sk + b, vcol0 // wv + g))]
    if has_ctx:
        ins += [kc, vc]
        specs += [pl.BlockSpec((skc, wq), lambda b, g, i: (b, g)),
                  pl.BlockSpec((skc, wv), lambda b, g, i: (b, g))]
    return pl.pallas_call(
        functools.partial(_attn_kernel, heads=hp, dk=dk, dv=dv, scale=scale, has_ctx=has_ctx),
        out_shape=jax.ShapeDtypeStruct((nb * sq, heads * dv), BF16),
        grid=(nb, ng, nq), in_specs=specs,
        out_specs=pl.BlockSpec((tq, wv), lambda b, g, i: (b * nq + i, g)),
        compiler_params=_cparams(("arbitrary", "arbitrary", "arbitrary")), name="attn_h%d" % heads,
    )(*ins)


def _da_lambda(lq1, lk1, lq2, lk2, lam_init):
    return (jnp.exp(jnp.sum(lq1[...] * lk1[...], axis=-1, keepdims=True))
            - jnp.exp(jnp.sum(lq2[...] * lk2[...], axis=-1, keepdims=True)) + lam_init)


def _da_head(q, k, v, lam, g, lam_init, kc=None, vc=None):
    first = lax.broadcasted_iota(jnp.int32, (1, LANE), 1) < DA_QK
    q = q.astype(F32) * (DA_QK ** -0.5)
    qs = (jnp.where(first, q, 0.0).astype(BF16), jnp.where(first, 0.0, q).astype(BF16))
    o = _attn_head(qs[0], k, v, 1.0, kc, vc) - lam * _attn_head(qs[1], k, v, 1.0, kc, vc)
    return _rms(o, g) * (1.0 - lam_init)


def _da_kernel(*refs, has_ctx, lam_init):
    if has_ctx:
        q_ref, k_ref, v_ref, kc_ref, vc_ref, lq1, lk1, lq2, lk2, g_ref, o_ref = refs
    else:
        q_ref, k_ref, v_ref, lq1, lk1, lq2, lk2, g_ref, o_ref = refs
    lam = _da_lambda(lq1, lk1, lq2, lk2, lam_init)
    for h in range(DA_HEADS):
        sl = slice(h * LANE, (h + 1) * LANE)
        o = _da_head(q_ref[:, sl], k_ref[:, sl].astype(BF16), v_ref[:, sl].astype(BF16), lam, g_ref[...],
                     lam_init, kc_ref[:, sl].astype(BF16) if has_ctx else None,
                     vc_ref[:, h, :].astype(BF16) if has_ctx else None)
        o_ref[:, sl] = o.astype(BF16)


def _ctx_kernel(p_ref, gkv_ref, wk_ref, wv_ref, gq_ref, wuq_ref, lq1, lk1, lq2, lk2, g_ref, o_ref, ckvn_ref,
                *, lam_init):
    col = lambda c0, h, w=LANE: slice(c0 + h * w, c0 + (h + 1) * w)
    lam = _da_lambda(lq1, lk1, lq2, lk2, lam_init)
    for h in range(DA_HEADS):
        o = _da_head(p_ref[:, col(C_DAQ, h)], p_ref[:, col(C_DAK, h)].astype(BF16),
                     p_ref[:, col(C_DAV, h)].astype(BF16), lam, g_ref[...], lam_init)
        o_ref[:, col(0, h)] = o.astype(BF16)
    ckvn = _rms(p_ref[:, C_CKV:C_CKV + MLA_KV_RANK], gkv_ref[...])
    ckvn_ref[...] = ckvn
    cb = ckvn.astype(BF16)
    kn = _dot(cb, wk_ref[...]).astype(BF16)
    vb = _dot(cb, wv_ref[...]).astype(BF16)
    kr = p_ref[:, C_CKV + MLA_KV_RANK:C_CKV + MLA_KV_RANK + LANE].astype(BF16)
    q = _dot(_rms(p_ref[:, C_CQ:C_CQ + MLA_Q_RANK], gq_ref[...]).astype(BF16), wuq_ref[...]).astype(BF16)
    ob0 = DA_HEADS * DA_V
    for h in range(MLA_HEADS):
        kh = jnp.concatenate([kn[:, col(0, h)], kr], axis=1)
        o = _attn_head(q[:, col(0, h, MLA_QK_PAD)], kh, vb[:, col(0, h)], MLA_SCALE)
        o_ref[:, col(ob0, h)] = o.astype(BF16)
    oc0 = ob0 + MLA_HEADS * MLA_V
    for h in range(NA_HEADS):
        o = _attn_head(p_ref[:, col(C_NAQ, h)].astype(BF16), p_ref[:, col(C_NAK, h)].astype(BF16),
                       p_ref[:, col(C_NAV, h)].astype(BF16), NA_DIM ** -0.5)
        o_ref[:, col(oc0, h)] = o.astype(BF16)


def _ctx_attention(proj, layer, nb, seq, gkv, wk, wv, gq, wuq, lams, g, lam_init):
    d_out = DA_HEADS * DA_V + MLA_HEADS * MLA_V + NA_HEADS * NA_DIM
    params = [gkv, wk, wv, gq, wuq] + list(lams) + [g]
    return pl.pallas_call(
        functools.partial(_ctx_kernel, lam_init=lam_init),
        out_shape=[jax.ShapeDtypeStruct((nb * seq, d_out), BF16),
                   jax.ShapeDtypeStruct((nb * seq, MLA_KV_RANK), F32)],
        grid=(nb,),
        in_specs=[pl.BlockSpec((seq, P_COLS), lambda b: (b, 0))] + [_layer_spec(a, layer) for a in params],
        out_specs=[pl.BlockSpec((seq, d_out), lambda b: (b, 0)),
                   pl.BlockSpec((seq, MLA_KV_RANK), lambda b: (b, 0))],
        compiler_params=_cparams(("arbitrary",)), name="ctx_attention",
    )(proj, *params)


def _da(q, k, v, lams, g, layer, lam_init, *, nb, sq, sk, tq, qrow0=0, krow0=0, qcol0=0, kcol0=0, vcol0=0,
        vrow0=None, kc=None, vc=None, skc=0):
    nq = sq // tq
    w = DA_HEADS * LANE
    vrow0 = krow0 if vrow0 is None else vrow0
    has_ctx = kc is not None
    ins = [q, k, v]
    specs = [pl.BlockSpec((tq, w), lambda b, i: (qrow0 // tq + b * nq + i, qcol0 // w)),
             pl.BlockSpec((sk, w), lambda b, i: (krow0 // sk + b, kcol0 // w)),
             pl.BlockSpec((sk, w), lambda b, i: (vrow0 // sk + b, vcol0 // w))]
    if has_ctx:
        ins += [kc, vc]
        specs += [pl.BlockSpec((None, None, skc, w), lambda b, i: (b, layer, 0, 0)),
                  pl.BlockSpec((None, None, skc, DA_HEADS, DA_V), lambda b, i: (b, layer, 0, 0, 0))]
    ins += list(lams) + [g]
    specs += [_layer_spec(a, layer) for a in ins[-5:]]
    return pl.pallas_call(
        functools.partial(_da_kernel, has_ctx=has_ctx, lam_init=lam_init),
        out_shape=jax.ShapeDtypeStruct((nb * sq, w), BF16),
        grid=(nb, nq), in_specs=specs,
        out_specs=pl.BlockSpec((tq, w), lambda b, i: (b * nq + i, 0)),
        compiler_params=_cparams(("arbitrary", "arbitrary")), name="diff_attn",
    )(*ins)


def _na_kernel(q_ref, k_ref, v_ref, kc_ref, vc_ref, bias_ref, o_ref, *, rows_n, kh, rows_per_step):
    nwin = kh * GRID_W
    scale = NA_DIM ** -0.5
    for rr in range(rows_per_step):
        r = pl.program_id(1) * rows_per_step + rr
        rs = jnp.clip(r - kh // 2, 0, rows_n - kh)
        start = pl.multiple_of(rs * GRID_W, GRID_W)
        qrows = slice(rr * GRID_W, (rr + 1) * GRID_W)
        for h in range(NA_HEADS):
            sl = slice(h * NA_DIM, (h + 1) * NA_DIM)
            q = q_ref[qrows, sl].astype(BF16)
            kw = k_ref[pl.ds(start, nwin), sl].astype(BF16)
            vw = v_ref[pl.ds(start, nwin), sl].astype(BF16)
            s = _dot_nt(q, kw) * scale + bias_ref[h, r - rs]
            s2 = _dot_nt(q, kc_ref[:, h, :].astype(BF16)) * scale
            e, e2, inv = _softmax_parts(s, s2, 1.0)
            o = _dot(e.astype(BF16), vw) + _dot(e2.astype(BF16), vc_ref[:, h, :].astype(BF16))
            o_ref[qrows, sl] = (o * inv).astype(BF16)


def _na_latent(proj, layer, row0, nb, s_lat, kc, vc, bias):
    rows_n = s_lat // GRID_W
    kh = min(NA_KH, rows_n)
    w = NA_HEADS * NA_DIM
    skc = kc.shape[2]
    rps = NA_ROWS_PER_STEP if rows_n % NA_ROWS_PER_STEP == 0 else 1
    tq = rps * GRID_W
    steps = rows_n // rps
    return pl.pallas_call(
        functools.partial(_na_kernel, rows_n=rows_n, kh=kh, rows_per_step=rps),
        out_shape=jax.ShapeDtypeStruct((nb * s_lat, w), BF16),
        grid=(nb, steps),
        in_specs=[pl.BlockSpec((tq, w), lambda b, r: (row0 // tq + b * steps + r, C_NAQ // w)),
                  pl.BlockSpec((s_lat, w), lambda b, r: (row0 // s_lat + b, C_NAK // w)),
                  pl.BlockSpec((s_lat, w), lambda b, r: (row0 // s_lat + b, C_NAV // w)),
                  pl.BlockSpec((None, None, skc, NA_HEADS, NA_DIM), lambda b, r: (b, layer, 0, 0, 0)),
                  pl.BlockSpec((None, None, skc, NA_HEADS, NA_DIM), lambda b, r: (b, layer, 0, 0, 0)),
                  _layer_spec(bias, layer)],
        out_specs=pl.BlockSpec((tq, w), lambda b, r: (b * steps + r, 0)),
        compiler_params=_cparams(("arbitrary", "arbitrary")), name="na_latent",
    )(proj, proj, proj, kc, vc, bias)


def _na_bias_tables(rpb, rows_n):
    kh = min(NA_KH, rows_n)
    qc = np.arange(GRID_W)[:, None]
    kc = np.arange(GRID_W)[None, :]
    col_start = np.clip(qc - NA_KW // 2, 0, GRID_W - NA_KW)
    valid = (kc >= col_start) & (kc < col_start + NA_KW)
    coff = np.clip(kc - qc, -(NA_KW - 1), NA_KW - 1) + (NA_KW - 1)
    onehot = (coff.reshape(-1)[None, :] == np.arange(2 * NA_KW - 1)[:, None]).astype(np.float32)
    n_l = rpb.shape[0]
    t = jnp.einsum('lhrc,cx->lhrx', rpb.astype(F32), jnp.asarray(onehot), precision=lax.Precision.HIGHEST)
    t = t.reshape(n_l, NA_HEADS, 2 * NA_KH - 1, GRID_W, GRID_W)
    t = jnp.where(jnp.asarray(valid)[None, None, None], t, NEG_INF)
    strips = []
    for v in range(kh):
        lo = NA_KH - 1 - v
        s = t[:, :, lo:lo + kh]
        strips.append(jnp.transpose(s, (0, 1, 3, 2, 4)).reshape(n_l, NA_HEADS, GRID_W, kh * GRID_W))
    return jnp.stack(strips, axis=2)


def _ln_epilogue(z, g_ref, b_ref):
    mu = jnp.mean(z, axis=-1, keepdims=True)
    zc = z - mu
    var = jnp.mean(zc * zc, axis=-1, keepdims=True)
    return zc * lax.rsqrt(var + LN_EPS) * g_ref[...] + b_ref[...]


def _router_info(h, wr_ref, cnt_ref):
    n = h.shape[0]
    logits = _dot(h, wr_ref[...])
    lane = lax.broadcasted_iota(jnp.int32, logits.shape, 1)
    lg = jnp.where(lane < N_EXPERTS, logits, -jnp.inf)
    m1 = jnp.max(lg, axis=-1, keepdims=True)
    i1 = jnp.min(jnp.where(lg == m1, lane, LANE), axis=-1, keepdims=True)
    lg2 = jnp.where(lane == i1, -jnp.inf, lg)
    m2 = jnp.max(lg2, axis=-1, keepdims=True)
    i2 = jnp.min(jnp.where(lg2 == m2, lane, LANE), axis=-1, keepdims=True)
    e2 = jnp.exp(m2 - m1)
    inv = 1.0 / (1.0 + e2)
    oh1, oh2 = lane == i1, lane == i2
    o1, o2 = jnp.where(oh1, 1.0, 0.0), jnp.where(oh2, 1.0, 0.0)
    below = lax.broadcasted_iota(jnp.int32, (n, n), 1) < lax.broadcasted_iota(jnp.int32, (n, n), 0)
    tri = jnp.where(below, 1.0, 0.0).astype(BF16)
    p1 = _dot(tri, o1.astype(BF16))
    p2 = _dot(tri, o2.astype(BF16))
    tot1 = jnp.sum(o1, axis=0, keepdims=True)
    tot2 = jnp.sum(o2, axis=0, keepdims=True)
    cnt = cnt_ref[...]
    rank0 = jnp.sum(jnp.where(oh1, cnt + p1, 0.0), axis=-1, keepdims=True)
    rank1 = jnp.sum(jnp.where(oh2, cnt + tot1 + p2, 0.0), axis=-1, keepdims=True)
    cnt_ref[...] = cnt + tot1 + tot2
    cols = (i1.astype(F32), i2.astype(F32), inv, e2 * inv, rank0, rank1)
    info = jnp.zeros(logits.shape, F32)
    for c, v in enumerate(cols):
        info = jnp.where(lane == c, v, info)
    return info


def _mm_ln_kernel(*refs, n_x, n_tail, head_tiles, y_head_tiles, k_total, tk, alpha, with_h, with_router,
                  mask_k):
    it = iter(refs)
    x_refs = [next(it) for _ in range(n_x)]
    tail_refs = [next(it) for _ in range(n_tail)]
    w_ref, y_ref = next(it), next(it)
    ys_ref = next(it) if y_head_tiles else None
    gate_ref, lng_ref, lnb_ref = next(it), next(it), next(it)
    sh_ref = next(it) if with_h else None
    sc_ref = next(it) if with_h else None
    wr_ref = next(it) if with_router else None
    yo_ref = next(it)
    h_ref = next(it) if with_h else None
    go_ref = next(it) if with_router else None
    ca_ref = next(it) if with_router else None
    acc_ref = next(it)
    cnt_ref = next(it) if with_router else None
    k = pl.program_id(1)
    nk = pl.num_programs(1)
    if with_router:
        @pl.when((pl.program_id(0) == 0) & (k == 0))
        def _():
            cnt_ref[...] = jnp.zeros_like(cnt_ref)
    cat = lambda rs: rs[0][...] if len(rs) == 1 else jnp.concatenate([r[...] for r in rs], axis=1)
    x = cat(x_refs)
    if n_tail:
        x = jnp.where(pl.program_id(0) < head_tiles, x, cat(tail_refs))
    w = w_ref[...]
    if mask_k:
        lim = k_total - k * tk
        x = jnp.where(lax.broadcasted_iota(jnp.int32, x.shape, 1) < lim, x, jnp.zeros_like(x))
        w = jnp.where(lax.broadcasted_iota(jnp.int32, w.shape, 0) < lim, w, jnp.zeros_like(w))
    part = _dot(x, w.astype(BF16))

    @pl.when(k == 0)
    def _():
        acc_ref[...] = part

    @pl.when(k > 0)
    def _():
        acc_ref[...] += part

    @pl.when(k == nk - 1)
    def _():
        y_in = y_ref[...]
        if y_head_tiles:
            y_in = jnp.where(pl.program_id(0) < y_head_tiles, y_in, ys_ref[...])
        z = alpha * y_in + gate_ref[...] * acc_ref[...]
        y = _ln_epilogue(z, lng_ref, lnb_ref)
        yo_ref[...] = y
        if with_h:
            h = (y * (1.0 + sc_ref[...]) + sh_ref[...]).astype(BF16)
            h_ref[...] = h
            if with_router:
                go_ref[...] = _router_info(h, wr_ref, cnt_ref)
                ca_ref[...] = cnt_ref[...]


def _mm_ln(tok, xs, w, w_layer, y, mods, layer, gate_which, ln_g, ln_b, alpha, tm, tk, h_mod=None,
           w_router=None, router_layer=0, xs_tail=None, name="mm_ln"):
    m, d = tok.m, w.shape[2]
    k_total = w.shape[1]
    nk = pl.cdiv(k_total, tk)
    with_h = h_mod is not None
    with_router = w_router is not None
    ins, specs = [], []
    xs_tail = xs_tail or []
    head_tiles = xs[0].shape[0] // tm if xs_tail else 0
    for x in xs:
        wx = x.shape[1] if (len(xs) > 1 or xs_tail) else tk
        ins.append(x)
        if xs_tail:
            specs.append(pl.BlockSpec((tm, wx), lambda i, k: (jnp.minimum(i, head_tiles - 1), k)))
        else:
            specs.append(pl.BlockSpec((tm, wx), lambda i, k: (i, k)))
    for x in xs_tail:
        ins.append(x)
        specs.append(pl.BlockSpec((tm, x.shape[1]), lambda i, k: (jnp.maximum(i - head_tiles, 0), k)))
    ins.append(w)
    specs.append(pl.BlockSpec((None, tk, d), lambda i, k: (w_layer, k, 0)))
    y_head_tiles = 0
    if isinstance(y, tuple):
        y_head_tiles, yspecs = _head_tail_specs(y[0], y[1], tm, d, 2)
        ins += list(y)
        specs += yspecs
    else:
        ins.append(y)
        specs.append(pl.BlockSpec((tm, d), lambda i, k: (i, 0)))
    ins += [mods, ln_g, ln_b]
    specs += [tok.mod_spec(layer, gate_which, tm, d, 2),
              _layer_spec(ln_g, layer), _layer_spec(ln_b, layer)]
    outs = [jax.ShapeDtypeStruct((m, d), F32)]
    ospecs = [pl.BlockSpec((tm, d), lambda i, k: (i, 0))]
    if with_h:
        hl, hsh, hsc = h_mod
        ins += [mods, mods]
        specs += [tok.mod_spec(hl, hsh, tm, d, 2), tok.mod_spec(hl, hsc, tm, d, 2)]
        outs.append(jax.ShapeDtypeStruct((m, d), BF16))
        ospecs.append(pl.BlockSpec((tm, d), lambda i, k: (i, 0)))
    if with_router:
        ins.append(w_router)
        specs.append(_layer_spec(w_router, router_layer))
        outs += [jax.ShapeDtypeStruct((m, LANE), F32), jax.ShapeDtypeStruct((m // tm, 1, LANE), F32)]
        ospecs += [pl.BlockSpec((tm, LANE), lambda i, k: (i, 0)),
                   pl.BlockSpec((None, 1, LANE), lambda i, k: (i, 0, 0))]
    scratch = [pltpu.VMEM((tm, d), F32)]
    if with_router:
        scratch.append(pltpu.VMEM((1, LANE), F32))
    res = pl.pallas_call(
        functools.partial(_mm_ln_kernel, n_x=len(xs), n_tail=len(xs_tail), head_tiles=head_tiles,
                          y_head_tiles=y_head_tiles, k_total=k_total, tk=tk, alpha=alpha, with_h=with_h,
                          with_router=with_router, mask_k=(k_total % tk != 0)),
        out_shape=outs, grid=(m // tm, nk), in_specs=specs, out_specs=ospecs,
        scratch_shapes=scratch,
        compiler_params=_cparams(("arbitrary", "arbitrary")), name=name,
    )(*ins)
    return res


def _ffn_up_kernel(x_ref, w1_ref, w3_ref, o_ref):
    x = x_ref[...]
    a = _silu(_dot(x, w1_ref[...].astype(BF16))) * _dot(x, w3_ref[...].astype(BF16))
    o_ref[...] = a.astype(BF16)


def _ffn_up(x, w1, w3, layer, tm, tf):
    m, d = x.shape
    f = w1.shape[2]
    wspec = pl.BlockSpec((None, d, tf), lambda j, i: (layer, 0, j))
    return pl.pallas_call(
        _ffn_up_kernel,
        out_shape=jax.ShapeDtypeStruct((m, f), BF16),
        grid=(pl.cdiv(f, tf), m // tm),
        in_specs=[pl.BlockSpec((tm, d), lambda j, i: (i, 0)), wspec, wspec],
        out_specs=pl.BlockSpec((tm, tf), lambda j, i: (i, j)),
        compiler_params=_cparams(("arbitrary", "arbitrary")), name="ffn_up",
    )(x, w1, w3)


MOE_ROWS = 1024
MOE_SEL_ROWS = 256


def _moe_plan(info, c_after, m, mc):
    tr, ts = MOE_ROWS, MOE_SEL_ROWS
    e_n = N_EXPERTS
    n_chunks = m // mc
    n_tiles = (2 * m) // tr + e_n
    n_blocks = n_tiles * (tr // ts)
    maxp = n_blocks + e_n * n_chunks
    i32 = jnp.int32
    i1, i2 = info[:, 0].astype(i32), info[:, 1].astype(i32)
    r0, r1 = info[:, 4].astype(i32), info[:, 5].astype(i32)
    ca = c_after[:, 0, :e_n].astype(i32)
    cb = jnp.concatenate([jnp.zeros((1, e_n), i32), ca[:-1]], axis=0)
    counts = ca[-1]
    padded = ((counts + tr - 1) // tr) * tr
    start = jnp.cumsum(padded) - padded
    eid = jnp.arange(e_n, dtype=i32)

    def pick(idx, table):
        return jnp.sum(jnp.where(idx[:, None] == eid[None, :], table[None, :], 0), axis=1)

    pos0 = pick(i1, start) + r0
    pos1 = pick(i2, start) + r1
    def expert_of(row0):
        return jnp.minimum(jnp.sum((row0[:, None] >= (start + padded)[None, :]).astype(i32), axis=1), e_n - 1)

    trow0 = jnp.arange(n_tiles, dtype=i32) * tr
    te = expert_of(trow0)
    tv = trow0 < jnp.sum(padded)
    row0 = jnp.arange(n_blocks, dtype=i32) * ts
    be = expert_of(row0)
    k0 = row0 - pick(be, start)
    k1 = jnp.minimum(k0 + ts, pick(be, counts))
    sel = (be[:, None] == eid[None, :])
    cb_t = jnp.sum(jnp.where(sel[:, None, :], cb[None], 0), axis=2)
    ca_t = jnp.sum(jnp.where(sel[:, None, :], ca[None], 0), axis=2)
    ov = (row0 < jnp.sum(padded))[:, None] & (cb_t < k1[:, None]) & (ca_t > k0[:, None])
    first_chunk = (jnp.arange(n_chunks) == 0)[None, :]
    ov_g = ov | (~jnp.any(ov, axis=1, keepdims=True) & first_chunk)

    def pairs(mask2d, inner):
        flat = mask2d.reshape(-1)
        n = jnp.sum(flat.astype(i32))
        idx = jnp.nonzero(flat, size=maxp, fill_value=0)[0].astype(i32)
        p = jnp.arange(maxp, dtype=i32)
        valid = p < n
        idx = jnp.where(valid, idx, jnp.max(jnp.where(valid, idx, 0)))
        outer, inn = idx // inner, idx % inner
        prev = jnp.concatenate([jnp.full((1,), -1, i32), outer[:-1]])
        nxt = jnp.concatenate([outer[1:], jnp.full((1,), -1, i32)])
        first = valid & (outer != prev)
        last = valid & ((outer != nxt) | (p == n - 1))
        return outer, inn, first.astype(i32), last.astype(i32), valid.astype(i32)

    g_tile, g_chunk, g_first, _, g_valid = pairs(ov_g, n_chunks)
    c_chunk, c_tile, c_first, c_last, c_valid = pairs(ov.T, n_blocks)
    ti = jnp.minimum(jnp.arange(n_tiles, dtype=i32), jnp.sum(tv.astype(i32)) - 1)
    return dict(pos0=pos0, pos1=pos1, g1=info[:, 2], g2=info[:, 3], te=te, tv=tv.astype(i32), ti=ti,
                gather=(g_tile, g_chunk, g_first, g_valid),
                combine=(c_tile, c_chunk, c_first, c_last, c_valid), n_tiles=n_tiles, maxp=maxp)


def _moe_gather_kernel(pt, pc, pf, pv, h_ref, p0_ref, p1_ref, g0_ref, g1_ref, xs_ref, gr_ref):
    p = pl.program_id(0)
    tr, mc = xs_ref.shape[0], h_ref.shape[0]

    @pl.when(pf[p] == 1)
    def _():
        xs_ref[...] = jnp.zeros_like(xs_ref)
        gr_ref[...] = jnp.zeros_like(gr_ref)

    @pl.when(pv[p] == 1)
    def _():
        rows = pt[p] * tr + lax.broadcasted_iota(jnp.int32, (tr, mc), 0)
        m0 = p0_ref[...] == rows
        m1 = p1_ref[...] == rows
        sel = jnp.where(m0 | m1, 1.0, 0.0).astype(BF16)
        xs_ref[...] = (xs_ref[...].astype(F32) + _dot(sel, h_ref[...])).astype(BF16)
        gr_ref[...] += jnp.sum(jnp.where(m0, g0_ref[...], 0.0) + jnp.where(m1, g1_ref[...], 0.0),
                               axis=1, keepdims=True)


def _moe_gather(h, plan, mc):
    m, d = h.shape
    tr = MOE_SEL_ROWS
    rows = plan["n_tiles"] * MOE_ROWS
    row = lambda a: a.reshape(1, m)
    tok_spec = lambda: pl.BlockSpec((1, mc), lambda p, pt, pc, pf, pv: (0, pc[p]))
    return pl.pallas_call(
        _moe_gather_kernel,
        out_shape=[jax.ShapeDtypeStruct((rows, d), BF16), jax.ShapeDtypeStruct((rows, 1), F32)],
        grid_spec=pltpu.PrefetchScalarGridSpec(
            num_scalar_prefetch=4, grid=(plan["maxp"],),
            in_specs=[pl.BlockSpec((mc, d), lambda p, pt, pc, pf, pv: (pc[p], 0)),
                      tok_spec(), tok_spec(), tok_spec(), tok_spec()],
            out_specs=[pl.BlockSpec((tr, d), lambda p, pt, pc, pf, pv: (pt[p], 0)),
                       pl.BlockSpec((tr, 1), lambda p, pt, pc, pf, pv: (pt[p], 0))]),
        compiler_params=_cparams(("arbitrary",)), name="moe_gather",
    )(*plan["gather"], h, row(plan["pos0"]), row(plan["pos1"]), row(plan["g1"]), row(plan["g2"]))


def _moe_up_kernel(te, tv, ti, x_ref, w1_ref, w3_ref, g_ref, o_ref):
    i = pl.program_id(1)

    @pl.when(tv[i] == 1)
    def _():
        x = x_ref[...]
        a = _silu(_dot(x, w1_ref[...].astype(BF16))) * _dot(x, w3_ref[...].astype(BF16))
        o_ref[...] = (a * g_ref[...]).astype(BF16)

    @pl.when(tv[i] == 0)
    def _():
        o_ref[...] = jnp.zeros_like(o_ref)


def _moe_up(xs, grow, w1, w3, layer, plan, tf):
    rows, d = xs.shape
    tr = MOE_ROWS
    fe = w1.shape[3]
    wspec = pl.BlockSpec((None, None, d, tf), lambda j, i, te, tv, ti: (layer, te[i], 0, j))
    return pl.pallas_call(
        _moe_up_kernel,
        out_shape=jax.ShapeDtypeStruct((rows, fe), BF16),
        grid_spec=pltpu.PrefetchScalarGridSpec(
            num_scalar_prefetch=3, grid=(fe // tf, rows // tr),
            in_specs=[pl.BlockSpec((tr, d), lambda j, i, te, tv, ti: (ti[i], 0)), wspec, wspec,
                      pl.BlockSpec((tr, 1), lambda j, i, te, tv, ti: (ti[i], 0))],
            out_specs=pl.BlockSpec((tr, tf), lambda j, i, te, tv, ti: (i, j))),
        compiler_params=_cparams(("arbitrary", "arbitrary")), name="moe_up",
    )(plan["te"], plan["tv"], plan["ti"], xs, w1, w3, grow)


def _moe_down_kernel(te, tv, ti, a_ref, w_ref, o_ref):
    i = pl.program_id(1)

    @pl.when(tv[i] == 1)
    def _():
        o_ref[...] = _dot(a_ref[...], w_ref[...].astype(BF16)).astype(BF16)

    @pl.when(tv[i] == 0)
    def _():
        o_ref[...] = jnp.zeros_like(o_ref)


def _moe_down(a, w2, layer, plan, tn):
    rows, fe = a.shape
    tr = MOE_ROWS
    d = w2.shape[3]
    return pl.pallas_call(
        _moe_down_kernel,
        out_shape=jax.ShapeDtypeStruct((rows, d), BF16),
        grid_spec=pltpu.PrefetchScalarGridSpec(
            num_scalar_prefetch=3, grid=(d // tn, rows // tr),
            in_specs=[pl.BlockSpec((tr, fe), lambda n, i, te, tv, ti: (ti[i], 0)),
                      pl.BlockSpec((None, None, fe, tn), lambda n, i, te, tv, ti: (layer, te[i], 0, n))],
            out_specs=pl.BlockSpec((tr, tn), lambda n, i, te, tv, ti: (i, n))),
        compiler_params=_cparams(("arbitrary", "arbitrary")), name="moe_down",
    )(plan["te"], plan["tv"], plan["ti"], a, w2)


def _moe_combine_ln_kernel(ct, cc, cf, cl, cv, ys_ref, p0_ref, p1_ref, y_ref, gate_ref, lng_ref, lnb_ref,
                           *rest, alpha, with_h, split_chunks):
    yos_ref = None
    if with_h:
        sh_ref, sc_ref, yo_ref, h_ref, acc_ref = rest
    elif split_chunks:
        yo_ref, yos_ref, acc_ref = rest
    else:
        yo_ref, acc_ref = rest
    p = pl.program_id(0)
    tr, mc = ys_ref.shape[0], y_ref.shape[0]

    @pl.when(cf[p] == 1)
    def _():
        acc_ref[...] = jnp.zeros_like(acc_ref)

    @pl.when(cv[p] == 1)
    def _():
        cols = ct[p] * tr + lax.broadcasted_iota(jnp.int32, (mc, tr), 1)
        sel = jnp.where((p0_ref[...] == cols) | (p1_ref[...] == cols), 1.0, 0.0).astype(BF16)
        acc_ref[...] += _dot(sel, ys_ref[...])

    @pl.when(cl[p] == 1)
    def _():
        z = alpha * y_ref[...] + gate_ref[...] * acc_ref[...]
        y = _ln_epilogue(z, lng_ref, lnb_ref)
        if yos_ref is None:
            yo_ref[...] = y
        else:
            @pl.when(cc[p] < split_chunks)
            def _():
                yo_ref[...] = y

            @pl.when(cc[p] >= split_chunks)
            def _():
                yos_ref[...] = y
        if with_h:
            h_ref[...] = (y * (1.0 + sc_ref[...]) + sh_ref[...]).astype(BF16)


def _moe_combine_ln(tok, ys, plan, y, mods, layer, gate_which, ln_g, ln_b, alpha, mc, h_mod=None,
                    split_rows=0):
    m, d = y.shape
    tr = MOE_SEL_ROWS
    with_h = h_mod is not None
    col = lambda a: a.reshape(m, 1)

    def mod(l, which):
        return pl.BlockSpec((None, None, 1, d),
                            lambda p, ct, cc, cf, cl, cv: (l, tok.rid(cc[p], mc), 0, which))

    chunk = lambda w: pl.BlockSpec((mc, w), lambda p, ct, cc, cf, cl, cv: (cc[p], 0))
    ins = [ys, col(plan["pos0"]), col(plan["pos1"]), y, mods, ln_g, ln_b]
    specs = [pl.BlockSpec((tr, d), lambda p, ct, cc, cf, cl, cv: (ct[p], 0)), chunk(1), chunk(1), chunk(d),
             mod(layer, gate_which), _layer_spec(ln_g, layer), _layer_spec(ln_b, layer)]
    outs = [jax.ShapeDtypeStruct((m, d), F32)]
    ospecs = [chunk(d)]
    sc = split_rows // mc
    if sc:
        assert not with_h
        outs = [jax.ShapeDtypeStruct((split_rows, d), F32), jax.ShapeDtypeStruct((m - split_rows, d), F32)]
        ospecs = [pl.BlockSpec((mc, d), lambda p, ct, cc, cf, cl, cv: (jnp.minimum(cc[p], sc - 1), 0)),
                  pl.BlockSpec((mc, d), lambda p, ct, cc, cf, cl, cv: (jnp.maximum(cc[p] - sc, 0), 0))]
    if with_h:
        hl, hsh, hsc = h_mod
        ins += [mods, mods]
        specs += [mod(hl, hsh), mod(hl, hsc)]
        outs.append(jax.ShapeDtypeStruct((m, d), BF16))
        ospecs.append(chunk(d))
    return pl.pallas_call(
        functools.partial(_moe_combine_ln_kernel, alpha=alpha, with_h=with_h, split_chunks=sc),
        out_shape=outs,
        grid_spec=pltpu.PrefetchScalarGridSpec(
            num_scalar_prefetch=5, grid=(plan["maxp"],), in_specs=specs, out_specs=ospecs,
            scratch_shapes=[pltpu.VMEM((mc, d), F32)]),
        compiler_params=_cparams(("arbitrary",)), name="moe_combine_ln",
    )(*plan["combine"], *ins)


def _rope_tables(n_tokens, dim, pad_to):
    t = jnp.arange(n_tokens)
    row = (t // GRID_W).astype(F32)
    col = (t % GRID_W).astype(F32)
    half = dim // 2
    inv_freq = ROPE_THETA ** (-jnp.arange(0, half, 2, dtype=F32) / half)
    ar = row[:, None] * inv_freq[None, :]
    ac = col[:, None] * inv_freq[None, :]
    ang = jnp.concatenate([ar, ar, ac, ac], axis=-1)
    cos, sin = jnp.cos(ang), jnp.sin(ang)
    lo = (np.arange(dim) % (dim // 2)) < dim // 4
    sa = jnp.where(lo[None, :], -sin, 0.0)
    sb = jnp.where(lo[None, :], 0.0, sin)
    return cos, sa, sb


def _pad_lanes(x, width, fill):
    return jnp.concatenate([x, jnp.full((x.shape[0], width - x.shape[1]), fill, x.dtype)], axis=1)


def kernel(x_prompt, x_sample, cache_da_k, cache_da_v, cache_mla_ckv, cache_mla_krope, cache_na_k, cache_na_v, c, c_ctx, w_ada, b_ada, w_in, da_lq1, da_lk1, da_lq2, da_lk2, da_subln, mla_gq, mla_gkv, mla_wuq, mla_wukv, na_rpb, w_out, ln1_g, ln1_b, ln2_g, ln2_b, ffn_w1, ffn_w3, ffn_w2, moe_router, moe_w1, moe_w3, moe_w2):
    nbp, seq, d = x_prompt.shape
    nbs, s_lat, _ = x_sample.shape
    depth = w_in.shape[0]
    past = cache_da_k.shape[2]
    mp, ms = nbp * seq, nbs * s_lat
    m = mp + ms
    tok = _Tok(mp, s_lat, nbs)
    tm = 512
    assert mp % s_lat == 0 and s_lat % tm == 0 and mp % tm == 0 and seq % LANE == 0 and nbs + 1 <= COND_ROWS
    alpha = (2.0 * depth) ** 0.25
    rows_n = s_lat // GRID_W

    cond = jnp.concatenate([c_ctx[None], c, jnp.zeros((COND_ROWS - 1 - nbs, d), F32)], axis=0)
    mods = _ada(cond, w_ada, b_ada).reshape(depth, COND_ROWS, 1, 6 * d)

    cos, sa, sb = _rope_tables(s_lat, DA_QK, LANE)
    rope_d = tuple(jnp.tile(t, (1, 2)) for t in (cos, sa, sb))
    rope_m = (_pad_lanes(cos, LANE, 1.0), _pad_lanes(sa, LANE, 0.0), _pad_lanes(sb, LANE, 0.0))

    y = (x_prompt.reshape(mp, d), x_sample.reshape(ms, d))
    h = _modulate(tok, y[0], y[1], mods, 0, tm)

    split = C_CKV + MLA_KV_RANK + MLA_ROPE
    w_in_p = jnp.concatenate([w_in[:, :, :split], jnp.zeros((depth, d, C_NAQ - split), F32), w_in[:, :, split:]],
                             axis=2).astype(BF16)
    wuq = mla_wuq.reshape(depth, MLA_Q_RANK, MLA_HEADS, MLA_NOPE + MLA_ROPE)
    wuq_p = jnp.concatenate(
        [wuq, jnp.zeros((depth, MLA_Q_RANK, MLA_HEADS, MLA_QK_PAD - MLA_NOPE - MLA_ROPE), F32)],
        axis=3).reshape(depth, MLA_Q_RANK, MLA_HEADS * MLA_QK_PAD).astype(BF16)
    wukv = mla_wukv.reshape(depth, MLA_KV_RANK, MLA_HEADS, MLA_NOPE + MLA_V)
    wk = wukv[..., :MLA_NOPE].reshape(depth, MLA_KV_RANK, MLA_HEADS * MLA_NOPE).astype(BF16)
    wv = wukv[..., MLA_NOPE:].reshape(depth, MLA_KV_RANK, MLA_HEADS * MLA_V).astype(BF16)
    w_out_b = w_out.astype(BF16)
    ffn_w2_b = ffn_w2.astype(BF16)
    n_moe = moe_router.shape[0]
    wr = jnp.concatenate([moe_router, jnp.zeros((n_moe, d, LANE - N_EXPERTS), F32)], axis=2).astype(BF16)
    vec = lambda a: a.reshape(a.shape[0], 1, a.shape[1])
    gq, gkv, gsub = vec(mla_gq), vec(mla_gkv), vec(da_subln)
    lams = (vec(da_lq1), vec(da_lk1), vec(da_lq2), vec(da_lk2))
    ln1g, ln1b, ln2g, ln2b = vec(ln1_g), vec(ln1_b), vec(ln2_g), vec(ln2_b)
    cda_k = cache_da_k.reshape(nbs, depth, past, DA_HEADS * LANE)
    c_ckv = cache_mla_ckv.reshape(nbs * depth * past, MLA_KV_RANK)
    c_kr = jnp.concatenate([cache_mla_krope, jnp.zeros((nbs, depth, past, LANE - MLA_ROPE), F32)],
                           axis=-1).reshape(nbs * depth * past, LANE)
    bias = _na_bias_tables(na_rpb, rows_n)
    tm_big = 1024 if (mp % 1024 == 0 and s_lat % 1024 == 0) else tm

    st = [[] for _ in range(6)]
    for l in range(depth):
        lam_init = 0.8 - 0.6 * math.exp(-0.3 * l)
        proj = _mm(h, w_in_p, l, tm_big, 1024)

        o_p, ckvn_p = _ctx_attention(proj, l, nbp, seq, gkv, wk, wv, gq, wuq_p, lams, gsub, lam_init)

        kcat_s, vb_s, _, qcat_s, qa_s, ka_s = _mla_prep(proj, l, mp, ms, tm, gkv, wk, wv, gq, wuq_p,
                                                      rope_m=rope_m, rope_d=rope_d, s_lat=s_lat)
        kcat_c, vb_c = _mla_cache(c_ckv, c_kr, l, depth, nbs, past, wk, wv)
        tq = 512 if s_lat % 512 == 0 else s_lat
        oa_s = _da(qa_s, ka_s, proj, lams, gsub, l, lam_init, nb=nbs, sq=s_lat, sk=s_lat, tq=tq,
                   vrow0=mp, vcol0=C_DAV, kc=cda_k, vc=cache_da_v, skc=past)
        ob_s = _attn(qcat_s, kcat_s, vb_s, nb=nbs, sq=s_lat, sk=s_lat, tq=tq, heads=MLA_HEADS,
                     hp=MLA_HEADS_PER_STEP, dk=MLA_QK_PAD, dv=MLA_V, scale=MLA_SCALE, kc=kcat_c, vc=vb_c,
                     skc=past)
        oc_s = _na_latent(proj, l, mp, nbs, s_lat, cache_na_k, cache_na_v, bias)

        i = l // 2
        moe = (l % 2 == 1)
        res = _mm_ln(tok, [o_p], w_out_b, l, y, mods, l, 2, ln1g, ln1b, alpha, tm, d, h_mod=(l, 3, 4),
                     w_router=wr if moe else None, router_layer=i, xs_tail=[oa_s, ob_s, oc_s],
                     name="out_proj_ln")
        y, h2 = res[0], res[1]

        nxt = (l + 1, 0, 1) if l + 1 < depth else None
        if not moe:
            a = _ffn_up(h2, ffn_w1, ffn_w3, i, tm_big, 512)
            res = _mm_ln(tok, [a], ffn_w2_b, i, y, mods, l, 5, ln2g, ln2b, alpha, tm, FFN_DOWN_TK,
                         h_mod=nxt, name="ffn_down_ln")
        else:
            plan = _moe_plan(res[2], res[3], m, tm)
            xs, grow = _moe_gather(h2, plan, tm)
            a = _moe_up(xs, grow, moe_w1, moe_w3, i, plan, 256)
            ys = _moe_down(a, moe_w2, i, plan, 512)
            res = _moe_combine_ln(tok, ys, plan, y, mods, l, 5, ln2g, ln2b, alpha, tm, h_mod=nxt,
                                  split_rows=0 if nxt is not None else mp)
        y = res[0] if nxt is not None else tuple(res)
        if nxt is not None:
            h = res[1]

        pp = proj[:mp]
        st[0].append(pp[:, C_DAK:C_DAK + 512].reshape(nbp, seq, DA_HEADS, 2, DA_QK))
        st[1].append(pp[:, C_DAV:C_DAV + 512].reshape(nbp, seq, DA_HEADS, DA_V))
        st[2].append(ckvn_p.reshape(nbp, seq, MLA_KV_RANK))
        st[3].append(pp[:, C_CKV + MLA_KV_RANK:C_CKV + MLA_KV_RANK + MLA_ROPE].reshape(nbp, seq, MLA_ROPE))
        st[4].append(pp[:, C_NAK:C_NAK + 512].reshape(nbp, seq, NA_HEADS, NA_DIM))
        st[5].append(pp[:, C_NAV:C_NAV + 512].reshape(nbp, seq, NA_HEADS, NA_DIM))

    if not isinstance(y, tuple):
        y = (y[:mp], y[mp:])
    return (y[0].reshape(nbp, seq, d), y[1].reshape(nbs, s_lat, d)) + tuple(jnp.stack(s, axis=1) for s in st)
```

```python
import functools
import math

import numpy as np
import jax
import jax.numpy as jnp
from jax import lax
from jax.experimental import pallas as pl
from jax.experimental.pallas import tpu as pltpu

F32 = jnp.float32
BF16 = jnp.bfloat16

GRID_W = 64
DA_QK = 64
DA_V = 128
DA_HEADS = 4
MLA_Q_RANK = 512
MLA_KV_RANK = 256
MLA_NOPE = 128
MLA_ROPE = 64
MLA_V = 128
MLA_HEADS = 8
MLA_SCALE = (MLA_NOPE + MLA_ROPE) ** -0.5
NA_DIM = 128
NA_HEADS = 4
NA_KH = 8
NA_KW = 16
N_EXPERTS = 8
ROPE_THETA = 10000.0
LN_EPS = 1e-5
RMS_EPS = 1e-6
NEG_INF = -1e30
LOG2E = 1.4426950408889634
LANE = 128
COND_ROWS = 8
VMEM_LIMIT = 56 * 1024 * 1024

C_DAQ, C_DAK, C_DAV, C_CQ, C_CKV, C_NAQ, C_NAK, C_NAV = 0, 512, 1024, 1536, 2048, 2560, 3072, 3584
P_COLS = 4096
MLA_QK_PAD = 256
FFN_DOWN_TK = 1408
NA_ROWS_PER_STEP = 4
MLA_HEADS_PER_STEP = 4


def _cparams(sem):
    return pltpu.CompilerParams(dimension_semantics=sem, vmem_limit_bytes=VMEM_LIMIT)


def _dot(a, b):
    return jnp.dot(a, b, preferred_element_type=F32)


def _dot_nt(a, b):
    return lax.dot_general(a, b, (((1,), (1,)), ((), ())), preferred_element_type=F32)


def _silu(x):
    return x * (1.0 / (1.0 + jnp.exp(-x)))


def _rms(x, g):
    return x * lax.rsqrt(jnp.mean(x * x, axis=-1, keepdims=True) + RMS_EPS) * g


def _rope(x, cos, sa, sb):
    return x * cos + pltpu.roll(x, LANE - 16, 1) * sa + pltpu.roll(x, 16, 1) * sb


def _ada_kernel(c_ref, w_ref, b_ref, o_ref):
    s = _silu(c_ref[...]).astype(BF16)
    o_ref[...] = _dot(s, w_ref[...].astype(BF16)) + b_ref[...]


def _ada(cond, w_ada, b_ada):
    n_layers, d, n = w_ada.shape
    tn = 1024
    return pl.pallas_call(
        _ada_kernel,
        out_shape=jax.ShapeDtypeStruct((n_layers, COND_ROWS, n), F32),
        grid=(n_layers, n // tn),
        in_specs=[pl.BlockSpec((COND_ROWS, d), lambda l, j: (0, 0)),
                  pl.BlockSpec((None, d, tn), lambda l, j: (l, 0, j)),
                  pl.BlockSpec((None, 1, tn), lambda l, j: (l, 0, j))],
        out_specs=pl.BlockSpec((None, COND_ROWS, tn), lambda l, j: (l, 0, j)),
        compiler_params=_cparams(("arbitrary", "arbitrary")), name="ada",
    )(cond, w_ada, b_ada.reshape(n_layers, 1, n))


class _Tok:
    def __init__(self, mp, s_lat, n_lat_batches):
        self.mp, self.s_lat, self.nb = mp, s_lat, n_lat_batches
        self.m = mp + s_lat * n_lat_batches

    def rid(self, i, tm):
        r0 = i * tm
        return jnp.where(r0 < self.mp, 0, 1 + (r0 - self.mp) // self.s_lat)

    def mod_spec(self, layer, which, tm, d, grid_rank=1):
        if grid_rank == 1:
            return pl.BlockSpec((None, None, 1, d), lambda i: (layer, self.rid(i, tm), 0, which))
        return pl.BlockSpec((None, None, 1, d), lambda i, k: (layer, self.rid(i, tm), 0, which))


def _head_tail_specs(head, tail, tm, width, rank):
    ht = head.shape[0] // tm
    if rank == 1:
        return ht, [pl.BlockSpec((tm, width), lambda i: (jnp.minimum(i, ht - 1), 0)),
                    pl.BlockSpec((tm, width), lambda i: (jnp.maximum(i - ht, 0), 0))]
    return ht, [pl.BlockSpec((tm, width), lambda i, k: (jnp.minimum(i, ht - 1), 0)),
                pl.BlockSpec((tm, width), lambda i, k: (jnp.maximum(i - ht, 0), 0))]


def _modulate_kernel(xp_ref, xs_ref, sh_ref, sc_ref, o_ref, *, head_tiles):
    x = jnp.where(pl.program_id(0) < head_tiles, xp_ref[...], xs_ref[...])
    o_ref[...] = (x * (1.0 + sc_ref[...]) + sh_ref[...]).astype(BF16)


def _modulate(tok, x_p, x_s, mods, layer, tm):
    d = x_p.shape[1]
    ht, xspecs = _head_tail_specs(x_p, x_s, tm, d, 1)
    return pl.pallas_call(
        functools.partial(_modulate_kernel, head_tiles=ht),
        out_shape=jax.ShapeDtypeStruct((tok.m, d), BF16),
        grid=(tok.m // tm,),
        in_specs=xspecs + [tok.mod_spec(layer, 0, tm, d), tok.mod_spec(layer, 1, tm, d)],
        out_specs=pl.BlockSpec((tm, d), lambda i: (i, 0)),
        compiler_params=_cparams(("arbitrary",)), name="modulate",
    )(x_p, x_s, mods, mods)


def _mm_kernel(x_ref, w_ref, o_ref):
    o_ref[...] = _dot(x_ref[...], w_ref[...])


def _layer_spec(arr, layer):
    zeros = (0,) * (arr.ndim - 1)
    return pl.BlockSpec((None,) + arr.shape[1:], lambda *_: (layer,) + zeros)


def _mm(x, w, layer, tm, tn):
    m, k = x.shape
    n = w.shape[2]
    return pl.pallas_call(
        _mm_kernel,
        out_shape=jax.ShapeDtypeStruct((m, n), F32),
        grid=(n // tn, m // tm),
        in_specs=[pl.BlockSpec((tm, k), lambda j, i: (i, 0)),
                  pl.BlockSpec((None, k, tn), lambda j, i: (layer, 0, j))],
        out_specs=pl.BlockSpec((tm, tn), lambda j, i: (i, j)),
        compiler_params=_cparams(("arbitrary", "arbitrary")), name="in_proj",
    )(x, w)


def _mla_prep_kernel(*refs, norm, with_q, rope, da_rope):
    it = iter(refs)
    ckv_ref = next(it)
    cq_ref = next(it) if with_q else None
    daq_ref = next(it) if da_rope else None
    dak_ref = next(it) if da_rope else None
    gkv_ref, wk_ref, wv_ref = next(it), next(it), next(it)
    gq_ref = next(it) if with_q else None
    wuq_ref = next(it) if with_q else None
    if rope:
        cm_ref, sam_ref, sbm_ref = next(it), next(it), next(it)
    if da_rope:
        cd_ref, sad_ref, sbd_ref = next(it), next(it), next(it)
    kcat_ref, vb_ref, ckvn_ref = next(it), next(it), next(it)
    qcat_ref = next(it) if with_q else None
    qa_ref = next(it) if da_rope else None
    ka_ref = next(it) if da_rope else None

    ckv = ckv_ref[:, :MLA_KV_RANK]
    ckvn = _rms(ckv, gkv_ref[...]) if norm else ckv
    ckvn_ref[...] = ckvn
    kr = ckv_ref[:, MLA_KV_RANK:MLA_KV_RANK + LANE]
    if rope:
        kr = _rope(kr, cm_ref[...], sam_ref[...], sbm_ref[...])
    kr = kr.astype(BF16)
    cb = ckvn.astype(BF16)
    kn = _dot(cb, wk_ref[...]).astype(BF16)
    vb_ref[...] = _dot(cb, wv_ref[...]).astype(BF16)
    for h in range(MLA_HEADS):
        kcat_ref[:, h * MLA_QK_PAD:h * MLA_QK_PAD + MLA_NOPE] = kn[:, h * MLA_NOPE:(h + 1) * MLA_NOPE]
        kcat_ref[:, h * MLA_QK_PAD + MLA_NOPE:(h + 1) * MLA_QK_PAD] = kr
    if with_q:
        cqn = _rms(cq_ref[...], gq_ref[...]).astype(BF16)
        q = _dot(cqn, wuq_ref[...])
        for h in range(MLA_HEADS):
            lo = h * MLA_QK_PAD
            qcat_ref[:, lo:lo + MLA_NOPE] = q[:, lo:lo + MLA_NOPE].astype(BF16)
            qr = q[:, lo + MLA_NOPE:lo + MLA_QK_PAD]
            if rope:
                qr = _rope(qr, cm_ref[...], sam_ref[...], sbm_ref[...])
            qcat_ref[:, lo + MLA_NOPE:lo + MLA_QK_PAD] = qr.astype(BF16)
    if da_rope:
        for h in range(DA_HEADS):
            sl = slice(h * LANE, (h + 1) * LANE)
            qa_ref[:, sl] = _rope(daq_ref[:, sl], cd_ref[...], sad_ref[...], sbd_ref[...]).astype(BF16)
            ka_ref[:, sl] = _rope(dak_ref[:, sl], cd_ref[...], sad_ref[...], sbd_ref[...]).astype(BF16)


def _mla_prep(src, layer, row0, rows, tm, gkv, wk, wv, gq=None, wuq=None, rope_m=None, rope_d=None,
              norm=True, s_lat=None, row_step=1):
    with_q = wuq is not None
    rope = rope_m is not None
    da_rope = rope_d is not None
    b0 = row0 // tm
    n = rows // tm
    ckv_blk = C_CKV // 512 if src.shape[1] == P_COLS else 0
    blk = lambda c: pl.BlockSpec((tm, 512), lambda i: (b0 + i * row_step, c))
    ins, specs = [src], [blk(ckv_blk)]
    if with_q:
        ins.append(src)
        specs.append(blk(C_CQ // 512))
    if da_rope:
        ins += [src, src]
        specs += [blk(C_DAQ // 512), blk(C_DAK // 512)]
    ins += [gkv, wk, wv]
    specs += [_layer_spec(gkv, layer), _layer_spec(wk, layer), _layer_spec(wv, layer)]
    if with_q:
        ins += [gq, wuq]
        specs += [_layer_spec(gq, layer), _layer_spec(wuq, layer)]
    nt = (s_lat // tm) if rope else 1
    if rope:
        ins += list(rope_m)
        specs += [pl.BlockSpec((tm, LANE), lambda i: (i % nt, 0))] * 3
    if da_rope:
        ins += list(rope_d)
        specs += [pl.BlockSpec((tm, LANE), lambda i: (i % nt, 0))] * 3
    outs = [jax.ShapeDtypeStruct((rows, MLA_HEADS * MLA_QK_PAD), BF16),
            jax.ShapeDtypeStruct((rows, MLA_HEADS * MLA_V), BF16),
            jax.ShapeDtypeStruct((rows, MLA_KV_RANK), F32)]
    if with_q:
        outs.append(jax.ShapeDtypeStruct((rows, MLA_HEADS * MLA_QK_PAD), BF16))
    if da_rope:
        outs += [jax.ShapeDtypeStruct((rows, DA_HEADS * LANE), BF16)] * 2
    ospecs = [pl.BlockSpec((tm, o.shape[1]), lambda i: (i, 0)) for o in outs]
    return pl.pallas_call(
        functools.partial(_mla_prep_kernel, norm=norm, with_q=with_q, rope=rope, da_rope=da_rope),
        out_shape=outs, grid=(n,), in_specs=specs, out_specs=ospecs,
        compiler_params=_cparams(("arbitrary",)), name="mla_prep",
    )(*ins)


def _mla_cache_kernel(ckv_ref, kr_ref, wk_ref, wv_ref, kcat_ref, vb_ref):
    cb = ckv_ref[...].astype(BF16)
    kn = _dot(cb, wk_ref[...]).astype(BF16)
    vb_ref[...] = _dot(cb, wv_ref[...]).astype(BF16)
    kr = kr_ref[...].astype(BF16)
    for h in range(MLA_HEADS):
        kcat_ref[:, h * MLA_QK_PAD:h * MLA_QK_PAD + MLA_NOPE] = kn[:, h * MLA_NOPE:(h + 1) * MLA_NOPE]
        kcat_ref[:, h * MLA_QK_PAD + MLA_NOPE:(h + 1) * MLA_QK_PAD] = kr


def _mla_cache(ckv, kr, layer, depth, nb, past, wk, wv):
    rows = lambda w: pl.BlockSpec((past, w), lambda b: (b * depth + layer, 0))
    out = lambda w: pl.BlockSpec((past, w), lambda b: (b, 0))
    return pl.pallas_call(
        _mla_cache_kernel,
        out_shape=[jax.ShapeDtypeStruct((nb * past, MLA_HEADS * MLA_QK_PAD), BF16),
                   jax.ShapeDtypeStruct((nb * past, MLA_HEADS * MLA_V), BF16)],
        grid=(nb,),
        in_specs=[rows(MLA_KV_RANK), rows(LANE), _layer_spec(wk, layer), _layer_spec(wv, layer)],
        out_specs=[out(MLA_HEADS * MLA_QK_PAD), out(MLA_HEADS * MLA_V)],
        compiler_params=_cparams(("arbitrary",)), name="mla_cache",
    )(ckv, kr, wk, wv)


def _softmax_parts(s, s2, scale):
    c = scale * LOG2E
    m = jnp.max(s, axis=-1, keepdims=True)
    if s2 is not None:
        m = jnp.maximum(m, jnp.max(s2, axis=-1, keepdims=True))
    e = jnp.exp2((s - m) * c)
    den = jnp.sum(e, axis=-1, keepdims=True)
    e2 = None
    if s2 is not None:
        e2 = jnp.exp2((s2 - m) * c)
        den = den + jnp.sum(e2, axis=-1, keepdims=True)
    return e, e2, 1.0 / den


def _attn_head(q, k, v, scale, kc=None, vc=None):
    s = _dot_nt(q, k)
    s2 = _dot_nt(q, kc) if kc is not None else None
    e, e2, inv = _softmax_parts(s, s2, scale)
    o = _dot(e.astype(BF16), v)
    if kc is not None:
        o = o + _dot(e2.astype(BF16), vc)
    return o * inv


def _attn_kernel(*refs, heads, dk, dv, scale, has_ctx):
    if has_ctx:
        q_ref, k_ref, v_ref, kc_ref, vc_ref, o_ref = refs
    else:
        q_ref, k_ref, v_ref, o_ref = refs
    for h in range(heads):
        ks, vs = slice(h * dk, (h + 1) * dk), slice(h * dv, (h + 1) * dv)
        o = _attn_head(q_ref[:, ks].astype(BF16), k_ref[:, ks].astype(BF16), v_ref[:, vs].astype(BF16), scale,
                       kc_ref[:, ks].astype(BF16) if has_ctx else None,
                       vc_ref[:, vs].astype(BF16) if has_ctx else None)
        o_ref[:, vs] = o.astype(BF16)


def _attn(q, k, v, *, nb, sq, sk, tq, heads, hp, dk, dv, scale, qrow0=0, krow0=0, qcol0=0, kcol0=0, vcol0=0,
          kc=None, vc=None, skc=0):
    nq = sq // tq
    ng = heads // hp
    wq, wv = hp * dk, hp * dv
    has_ctx = kc is not None
    ins = [q, k, v]
    specs = [pl.BlockSpec((tq, wq), lambda b, g, i: (qrow0 // tq + b * nq + i, qcol0 // wq + g)),
             pl.BlockSpec((sk, wq), lambda b, g, i: (krow0 // sk + b, kcol0 // wq + g)),
             pl.BlockSpec((sk, wv), lambda b, g, i: (krow0 // sk + b, vcol0 // wv + g))]
    if has_ctx:
        ins += [kc, vc]
        specs += [pl.BlockSpec((skc, wq), lambda b, g, i: (b, g)),
                  pl.BlockSpec((skc, wv), lambda b, g, i: (b, g))]
    return pl.pallas_call(
        functools.partial(_attn_kernel, heads=hp, dk=dk, dv=dv, scale=scale, has_ctx=has_ctx),
        out_shape=jax.ShapeDtypeStruct((nb * sq, heads * dv), BF16),
        grid=(nb, ng, nq), in_specs=specs,
        out_specs=pl.BlockSpec((tq, wv), lambda b, g, i: (b * nq + i, g)),
        compiler_params=_cparams(("arbitrary", "arbitrary", "arbitrary")), name="attn_h%d" % heads,
    )(*ins)


def _da_lambda(lq1, lk1, lq2, lk2, lam_init):
    return (jnp.exp(jnp.sum(lq1[...] * lk1[...], axis=-1, keepdims=True))
            - jnp.exp(jnp.sum(lq2[...] * lk2[...], axis=-1, keepdims=True)) + lam_init)


def _da_head(q, k, v, lam, g, lam_init, kc=None, vc=None):
    first = lax.broadcasted_iota(jnp.int32, (1, LANE), 1) < DA_QK
    q = q.astype(F32) * (DA_QK ** -0.5)
    qs = (jnp.where(first, q, 0.0).astype(BF16), jnp.where(first, 0.0, q).astype(BF16))
    o = _attn_head(qs[0], k, v, 1.0, kc, vc) - lam * _attn_head(qs[1], k, v, 1.0, kc, vc)
    return _rms(o, g) * (1.0 - lam_init)


def _da_kernel(*refs, has_ctx, lam_init):
    if has_ctx:
        q_ref, k_ref, v_ref, kc_ref, vc_ref, lq1, lk1, lq2, lk2, g_ref, o_ref = refs
    else:
        q_ref, k_ref, v_ref, lq1, lk1, lq2, lk2, g_ref, o_ref = refs
    lam = _da_lambda(lq1, lk1, lq2, lk2, lam_init)
    for h in range(DA_HEADS):
        sl = slice(h * LANE, (h + 1) * LANE)
        o = _da_head(q_ref[:, sl], k_ref[:, sl].astype(BF16), v_ref[:, sl].astype(BF16), lam, g_ref[...],
                     lam_init, kc_ref[:, sl].astype(BF16) if has_ctx else None,
                     vc_ref[:, h, :].astype(BF16) if has_ctx else None)
        o_ref[:, sl] = o.astype(BF16)


def _ctx_kernel(p_ref, gkv_ref, wk_ref, wv_ref, gq_ref, wuq_ref, lq1, lk1, lq2, lk2, g_ref, *rest, lam_init):
    o_ref, ckvn_ref, dav_ref, nak_ref, nav_ref = rest[-5:]
    for h in range(DA_HEADS):
        dav_ref[:, h, :] = p_ref[:, C_DAV + h * DA_V:C_DAV + (h + 1) * DA_V]
    for h in range(NA_HEADS):
        nak_ref[:, h, :] = p_ref[:, C_NAK + h * NA_DIM:C_NAK + (h + 1) * NA_DIM]
        nav_ref[:, h, :] = p_ref[:, C_NAV + h * NA_DIM:C_NAV + (h + 1) * NA_DIM]
    col = lambda c0, h, w=LANE: slice(c0 + h * w, c0 + (h + 1) * w)
    lam = _da_lambda(lq1, lk1, lq2, lk2, lam_init)
    for h in range(DA_HEADS):
        o = _da_head(p_ref[:, col(C_DAQ, h)], p_ref[:, col(C_DAK, h)].astype(BF16),
                     p_ref[:, col(C_DAV, h)].astype(BF16), lam, g_ref[...], lam_init)
        o_ref[:, col(0, h)] = o.astype(BF16)
    ckvn = _rms(p_ref[:, C_CKV:C_CKV + MLA_KV_RANK], gkv_ref[...])
    ckvn_ref[...] = ckvn
    cb = ckvn.astype(BF16)
    kn = _dot(cb, wk_ref[...]).astype(BF16)
    vb = _dot(cb, wv_ref[...]).astype(BF16)
    kr = p_ref[:, C_CKV + MLA_KV_RANK:C_CKV + MLA_KV_RANK + LANE].astype(BF16)
    q = _dot(_rms(p_ref[:, C_CQ:C_CQ + MLA_Q_RANK], gq_ref[...]).astype(BF16), wuq_ref[...]).astype(BF16)
    ob0 = DA_HEADS * DA_V
    for h in range(MLA_HEADS):
        kh = jnp.concatenate([kn[:, col(0, h)], kr], axis=1)
        o = _attn_head(q[:, col(0, h, MLA_QK_PAD)], kh, vb[:, col(0, h)], MLA_SCALE)
        o_ref[:, col(ob0, h)] = o.astype(BF16)
    oc0 = ob0 + MLA_HEADS * MLA_V
    for h in range(NA_HEADS):
        o = _attn_head(p_ref[:, col(C_NAQ, h)].astype(BF16), p_ref[:, col(C_NAK, h)].astype(BF16),
                       p_ref[:, col(C_NAV, h)].astype(BF16), NA_DIM ** -0.5)
        o_ref[:, col(oc0, h)] = o.astype(BF16)


def _ctx_attention(proj, layer, depth, nb, seq, gkv, wk, wv, gq, wuq, lams, g, lam_init, stacks):
    d_out = DA_HEADS * DA_V + MLA_HEADS * MLA_V + NA_HEADS * NA_DIM
    params = [gkv, wk, wv, gq, wuq] + list(lams) + [g]
    stack_shapes = [(nb, depth, seq, MLA_KV_RANK), (nb, depth, seq, DA_HEADS, DA_V),
                    (nb, depth, seq, NA_HEADS, NA_DIM), (nb, depth, seq, NA_HEADS, NA_DIM)]
    stack_specs = [pl.BlockSpec((None, None) + s[2:], lambda b, n=len(s): (b, layer) + (0,) * (n - 2))
                   for s in stack_shapes]
    prev = list(stacks) if stacks is not None else []
    n_in = 1 + len(params)
    return pl.pallas_call(
        functools.partial(_ctx_kernel, lam_init=lam_init),
        out_shape=[jax.ShapeDtypeStruct((nb * seq, d_out), BF16)]
        + [jax.ShapeDtypeStruct(s, F32) for s in stack_shapes],
        grid=(nb,),
        in_specs=[pl.BlockSpec((seq, P_COLS), lambda b: (b, 0))] + [_layer_spec(a, layer) for a in params]
        + [pl.BlockSpec(memory_space=pl.ANY)] * len(prev),
        out_specs=[pl.BlockSpec((seq, d_out), lambda b: (b, 0))] + stack_specs,
        input_output_aliases={n_in + k: 1 + k for k in range(len(prev))},
        compiler_params=_cparams(("arbitrary",)), name="ctx_attention",
    )(proj, *params, *prev)


def _da(q, k, v, lams, g, layer, lam_init, *, nb, sq, sk, tq, qrow0=0, krow0=0, qcol0=0, kcol0=0, vcol0=0,
        vrow0=None, kc=None, vc=None, skc=0):
    nq = sq // tq
    w = DA_HEADS * LANE
    vrow0 = krow0 if vrow0 is None else vrow0
    has_ctx = kc is not None
    ins = [q, k, v]
    specs = [pl.BlockSpec((tq, w), lambda b, i: (qrow0 // tq + b * nq + i, qcol0 // w)),
             pl.BlockSpec((sk, w), lambda b, i: (krow0 // sk + b, kcol0 // w)),
             pl.BlockSpec((sk, w), lambda b, i: (vrow0 // sk + b, vcol0 // w))]
    if has_ctx:
        ins += [kc, vc]
        specs += [pl.BlockSpec((None, None, skc, w), lambda b, i: (b, layer, 0, 0)),
                  pl.BlockSpec((None, None, skc, DA_HEADS, DA_V), lambda b, i: (b, layer, 0, 0, 0))]
    ins += list(lams) + [g]
    specs += [_layer_spec(a, layer) for a in ins[-5:]]
    return pl.pallas_call(
        functools.partial(_da_kernel, has_ctx=has_ctx, lam_init=lam_init),
        out_shape=jax.ShapeDtypeStruct((nb * sq, w), BF16),
        grid=(nb, nq), in_specs=specs,
        out_specs=pl.BlockSpec((tq, w), lambda b, i: (b * nq + i, 0)),
        compiler_params=_cparams(("arbitrary", "arbitrary")), name="diff_attn",
    )(*ins)


def _na_geometry(rows_n):
    kh = min(NA_KH, rows_n)
    rb = NA_ROWS_PER_STEP if rows_n % NA_ROWS_PER_STEP == 0 else 1
    uw = min(rows_n, kh + rb - 1)
    return kh, rb, uw


def _na_kernel(pat, q_ref, k_ref, v_ref, kc_ref, vc_ref, bias_ref, o_ref, *, rows_n, kh, rb, uw):
    k0 = jnp.clip(pl.program_id(1) * rb - kh // 2, 0, rows_n - uw)
    start = pl.multiple_of(k0 * GRID_W, GRID_W)
    nwin = uw * GRID_W
    scale = NA_DIM ** -0.5
    for h in range(NA_HEADS):
        sl = slice(h * NA_DIM, (h + 1) * NA_DIM)
        q = q_ref[:, sl].astype(BF16)
        kw = k_ref[pl.ds(start, nwin), sl].astype(BF16)
        vw = v_ref[pl.ds(start, nwin), sl].astype(BF16)
        s = _dot_nt(q, kw) * scale + bias_ref[h]
        s2 = _dot_nt(q, kc_ref[:, h, :].astype(BF16)) * scale
        e, e2, inv = _softmax_parts(s, s2, 1.0)
        o = _dot(e.astype(BF16), vw) + _dot(e2.astype(BF16), vc_ref[:, h, :].astype(BF16))
        o_ref[:, sl] = (o * inv).astype(BF16)


def _na_latent(proj, layer, row0, nb, s_lat, kc, vc, bias, pattern_of_step):
    rows_n = s_lat // GRID_W
    kh, rb, uw = _na_geometry(rows_n)
    w = NA_HEADS * NA_DIM
    skc = kc.shape[2]
    tq = rb * GRID_W
    steps = rows_n // rb
    return pl.pallas_call(
        functools.partial(_na_kernel, rows_n=rows_n, kh=kh, rb=rb, uw=uw),
        out_shape=jax.ShapeDtypeStruct((nb * s_lat, w), BF16),
        grid_spec=pltpu.PrefetchScalarGridSpec(
            num_scalar_prefetch=1, grid=(nb, steps),
            in_specs=[pl.BlockSpec((tq, w), lambda b, r, pat: (row0 // tq + b * steps + r, C_NAQ // w)),
                      pl.BlockSpec((s_lat, w), lambda b, r, pat: (row0 // s_lat + b, C_NAK // w)),
                      pl.BlockSpec((s_lat, w), lambda b, r, pat: (row0 // s_lat + b, C_NAV // w)),
                      pl.BlockSpec((None, None, skc, NA_HEADS, NA_DIM), lambda b, r, pat: (b, layer, 0, 0, 0)),
                      pl.BlockSpec((None, None, skc, NA_HEADS, NA_DIM), lambda b, r, pat: (b, layer, 0, 0, 0)),
                      pl.BlockSpec((None, NA_HEADS, None, tq, uw * GRID_W),
                                   lambda b, r, pat: (layer, 0, pat[r], 0, 0))],
            out_specs=pl.BlockSpec((tq, w), lambda b, r, pat: (b * steps + r, 0))),
        compiler_params=_cparams(("arbitrary", "arbitrary")), name="na_latent",
    )(pattern_of_step, proj, proj, proj, kc, vc, bias)


def _na_bias_tables(rpb, rows_n):
    kh = min(NA_KH, rows_n)
    qc = np.arange(GRID_W)[:, None]
    kc = np.arange(GRID_W)[None, :]
    col_start = np.clip(qc - NA_KW // 2, 0, GRID_W - NA_KW)
    valid = (kc >= col_start) & (kc < col_start + NA_KW)
    coff = np.clip(kc - qc, -(NA_KW - 1), NA_KW - 1) + (NA_KW - 1)
    onehot = (coff.reshape(-1)[None, :] == np.arange(2 * NA_KW - 1)[:, None]).astype(np.float32)
    n_l = rpb.shape[0]
    t = jnp.einsum('lhrc,cx->lhrx', rpb.astype(F32), jnp.asarray(onehot), precision=lax.Precision.HIGHEST)
    t = t.reshape(n_l, NA_HEADS, 2 * NA_KH - 1, GRID_W, GRID_W)
    t = jnp.where(jnp.asarray(valid)[None, None, None], t, NEG_INF)
    strips = []
    for v in range(kh):
        lo = NA_KH - 1 - v
        s = t[:, :, lo:lo + kh]
        strips.append(jnp.transpose(s, (0, 1, 3, 2, 4)).reshape(n_l, NA_HEADS, GRID_W, kh * GRID_W))
    _, rb, uw = _na_geometry(rows_n)
    patterns, pattern_of_step = [], []
    for t0 in range(rows_n // rb):
        k0 = int(np.clip(t0 * rb - kh // 2, 0, rows_n - uw))
        key = []
        for i in range(rb):
            r = t0 * rb + i
            rs = int(np.clip(r - kh // 2, 0, rows_n - kh))
            key.append((rs - k0, r - rs))
        key = tuple(key)
        if key not in patterns:
            patterns.append(key)
        pattern_of_step.append(patterns.index(key))
    blocks = []
    for key in patterns:
        rows = [jnp.pad(strips[v], ((0, 0), (0, 0), (0, 0), (off * GRID_W, (uw - kh - off) * GRID_W)),
                        constant_values=NEG_INF) for off, v in key]
        blocks.append(jnp.concatenate(rows, axis=2))
    return jnp.stack(blocks, axis=2), jnp.asarray(pattern_of_step, jnp.int32)


def _ln_epilogue(z, g_ref, b_ref):
    mu = jnp.mean(z, axis=-1, keepdims=True)
    zc = z - mu
    var = jnp.mean(zc * zc, axis=-1, keepdims=True)
    return zc * lax.rsqrt(var + LN_EPS) * g_ref[...] + b_ref[...]


def _router_info(h, wr_ref, cnt_ref):
    n = h.shape[0]
    logits = _dot(h, wr_ref[...])
    lane = lax.broadcasted_iota(jnp.int32, logits.shape, 1)
    lg = jnp.where(lane < N_EXPERTS, logits, -jnp.inf)
    m1 = jnp.max(lg, axis=-1, keepdims=True)
    i1 = jnp.min(jnp.where(lg == m1, lane, LANE), axis=-1, keepdims=True)
    lg2 = jnp.where(lane == i1, -jnp.inf, lg)
    m2 = jnp.max(lg2, axis=-1, keepdims=True)
    i2 = jnp.min(jnp.where(lg2 == m2, lane, LANE), axis=-1, keepdims=True)
    e2 = jnp.exp(m2 - m1)
    inv = 1.0 / (1.0 + e2)
    oh1, oh2 = lane == i1, lane == i2
    o1, o2 = jnp.where(oh1, 1.0, 0.0), jnp.where(oh2, 1.0, 0.0)
    below = lax.broadcasted_iota(jnp.int32, (n, n), 1) < lax.broadcasted_iota(jnp.int32, (n, n), 0)
    tri = jnp.where(below, 1.0, 0.0).astype(BF16)
    p1 = _dot(tri, o1.astype(BF16))
    p2 = _dot(tri, o2.astype(BF16))
    tot1 = jnp.sum(o1, axis=0, keepdims=True)
    tot2 = jnp.sum(o2, axis=0, keepdims=True)
    cnt = cnt_ref[...]
    rank0 = jnp.sum(jnp.where(oh1, cnt + p1, 0.0), axis=-1, keepdims=True)
    rank1 = jnp.sum(jnp.where(oh2, cnt + tot1 + p2, 0.0), axis=-1, keepdims=True)
    cnt_ref[...] = cnt + tot1 + tot2
    cols = (i1.astype(F32), i2.astype(F32), inv, e2 * inv, rank0, rank1)
    info = jnp.zeros(logits.shape, F32)
    for c, v in enumerate(cols):
        info = jnp.where(lane == c, v, info)
    return info


def _mm_ln_kernel(*refs, n_x, n_tail, head_tiles, y_head_tiles, k_total, tk, alpha, with_h, with_router,
                  mask_k):
    it = iter(refs)
    x_refs = [next(it) for _ in range(n_x)]
    tail_refs = [next(it) for _ in range(n_tail)]
    w_ref, y_ref = next(it), next(it)
    ys_ref = next(it) if y_head_tiles else None
    gate_ref, lng_ref, lnb_ref = next(it), next(it), next(it)
    sh_ref = next(it) if with_h else None
    sc_ref = next(it) if with_h else None
    wr_ref = next(it) if with_router else None
    yo_ref = next(it)
    h_ref = next(it) if with_h else None
    go_ref = next(it) if with_router else None
    ca_ref = next(it) if with_router else None
    acc_ref = next(it)
    cnt_ref = next(it) if with_router else None
    k = pl.program_id(1)
    nk = pl.num_programs(1)
    if with_router:
        @pl.when((pl.program_id(0) == 0) & (k == 0))
        def _():
            cnt_ref[...] = jnp.zeros_like(cnt_ref)
    cat = lambda rs: rs[0][...] if len(rs) == 1 else jnp.concatenate([r[...] for r in rs], axis=1)
    x = cat(x_refs)
    if n_tail:
        x = jnp.where(pl.program_id(0) < head_tiles, x, cat(tail_refs))
    w = w_ref[...]
    if mask_k:
        lim = k_total - k * tk
        x = jnp.where(lax.broadcasted_iota(jnp.int32, x.shape, 1) < lim, x, jnp.zeros_like(x))
        w = jnp.where(lax.broadcasted_iota(jnp.int32, w.shape, 0) < lim, w, jnp.zeros_like(w))
    part = _dot(x, w.astype(BF16))

    @pl.when(k == 0)
    def _():
        acc_ref[...] = part

    @pl.when(k > 0)
    def _():
        acc_ref[...] += part

    @pl.when(k == nk - 1)
    def _():
        y_in = y_ref[...]
        if y_head_tiles:
            y_in = jnp.where(pl.program_id(0) < y_head_tiles, y_in, ys_ref[...])
        z = alpha * y_in + gate_ref[...] * acc_ref[...]
        y = _ln_epilogue(z, lng_ref, lnb_ref)
        yo_ref[...] = y
        if with_h:
            h = (y * (1.0 + sc_ref[...]) + sh_ref[...]).astype(BF16)
            h_ref[...] = h
            if with_router:
                go_ref[...] = _router_info(h, wr_ref, cnt_ref)
                ca_ref[...] = cnt_ref[...]


def _mm_ln(tok, xs, w, w_layer, y, mods, layer, gate_which, ln_g, ln_b, alpha, tm, tk, h_mod=None,
           w_router=None, router_layer=0, xs_tail=None, name="mm_ln"):
    m, d = tok.m, w.shape[2]
    k_total = w.shape[1]
    nk = pl.cdiv(k_total, tk)
    with_h = h_mod is not None
    with_router = w_router is not None
    ins, specs = [], []
    xs_tail = xs_tail or []
    head_tiles = xs[0].shape[0] // tm if xs_tail else 0
    for x in xs:
        wx = x.shape[1] if (len(xs) > 1 or xs_tail) else tk
        ins.append(x)
        if xs_tail:
            specs.append(pl.BlockSpec((tm, wx), lambda i, k: (jnp.minimum(i, head_tiles - 1), k)))
        else:
            specs.append(pl.BlockSpec((tm, wx), lambda i, k: (i, k)))
    for x in xs_tail:
        ins.append(x)
        specs.append(pl.BlockSpec((tm, x.shape[1]), lambda i, k: (jnp.maximum(i - head_tiles, 0), k)))
    ins.append(w)
    specs.append(pl.BlockSpec((None, tk, d), lambda i, k: (w_layer, k, 0)))
    y_head_tiles = 0
    if isinstance(y, tuple):
        y_head_tiles, yspecs = _head_tail_specs(y[0], y[1], tm, d, 2)
        ins += list(y)
        specs += yspecs
    else:
        ins.append(y)
        specs.append(pl.BlockSpec((tm, d), lambda i, k: (i, 0)))
    ins += [mods, ln_g, ln_b]
    specs += [tok.mod_spec(layer, gate_which, tm, d, 2),
              _layer_spec(ln_g, layer), _layer_spec(ln_b, layer)]
    outs = [jax.ShapeDtypeStruct((m, d), F32)]
    ospecs = [pl.BlockSpec((tm, d), lambda i, k: (i, 0))]
    if with_h:
        hl, hsh, hsc = h_mod
        ins += [mods, mods]
        specs += [tok.mod_spec(hl, hsh, tm, d, 2), tok.mod_spec(hl, hsc, tm, d, 2)]
        outs.append(jax.ShapeDtypeStruct((m, d), BF16))
        ospecs.append(pl.BlockSpec((tm, d), lambda i, k: (i, 0)))
    if with_router:
        ins.append(w_router)
        specs.append(_layer_spec(w_router, router_layer))
        outs += [jax.ShapeDtypeStruct((m, LANE), F32), jax.ShapeDtypeStruct((m // tm, 1, LANE), F32)]
        ospecs += [pl.BlockSpec((tm, LANE), lambda i, k: (i, 0)),
                   pl.BlockSpec((None, 1, LANE), lambda i, k: (i, 0, 0))]
    scratch = [pltpu.VMEM((tm, d), F32)]
    if with_router:
        scratch.append(pltpu.VMEM((1, LANE), F32))
    res = pl.pallas_call(
        functools.partial(_mm_ln_kernel, n_x=len(xs), n_tail=len(xs_tail), head_tiles=head_tiles,
                          y_head_tiles=y_head_tiles, k_total=k_total, tk=tk, alpha=alpha, with_h=with_h,
                          with_router=with_router, mask_k=(k_total % tk != 0)),
        out_shape=outs, grid=(m // tm, nk), in_specs=specs, out_specs=ospecs,
        scratch_shapes=scratch,
        compiler_params=_cparams(("arbitrary", "arbitrary")), name=name,
    )(*ins)
    return res


def _ffn_up_kernel(x_ref, w1_ref, w3_ref, o_ref):
    x = x_ref[...]
    a = _silu(_dot(x, w1_ref[...].astype(BF16))) * _dot(x, w3_ref[...].astype(BF16))
    o_ref[...] = a.astype(BF16)


def _ffn_up(x, w1, w3, layer, tm, tf):
    m, d = x.shape
    f = w1.shape[2]
    wspec = pl.BlockSpec((None, d, tf), lambda j, i: (layer, 0, j))
    return pl.pallas_call(
        _ffn_up_kernel,
        out_shape=jax.ShapeDtypeStruct((m, f), BF16),
        grid=(pl.cdiv(f, tf), m // tm),
        in_specs=[pl.BlockSpec((tm, d), lambda j, i: (i, 0)), wspec, wspec],
        out_specs=pl.BlockSpec((tm, tf), lambda j, i: (i, j)),
        compiler_params=_cparams(("arbitrary", "arbitrary")), name="ffn_up",
    )(x, w1, w3)


MOE_ROWS = 1024
MOE_SEL_ROWS = 256


def _moe_plan(info, c_after, m, mc):
    tr, ts = MOE_ROWS, MOE_SEL_ROWS
    e_n = N_EXPERTS
    n_chunks = m // mc
    n_tiles = (2 * m) // tr + e_n
    n_blocks = n_tiles * (tr // ts)
    maxp = n_blocks + e_n * n_chunks
    i32 = jnp.int32
    i1, i2 = info[:, 0].astype(i32), info[:, 1].astype(i32)
    r0, r1 = info[:, 4].astype(i32), info[:, 5].astype(i32)
    ca = c_after[:, 0, :e_n].astype(i32)
    cb = jnp.concatenate([jnp.zeros((1, e_n), i32), ca[:-1]], axis=0)
    counts = ca[-1]
    padded = ((counts + tr - 1) // tr) * tr
    start = jnp.cumsum(padded) - padded
    eid = jnp.arange(e_n, dtype=i32)

    def pick(idx, table):
        return jnp.sum(jnp.where(idx[:, None] == eid[None, :], table[None, :], 0), axis=1)

    pos0 = pick(i1, start) + r0
    pos1 = pick(i2, start) + r1
    def expert_of(row0):
        return jnp.minimum(jnp.sum((row0[:, None] >= (start + padded)[None, :]).astype(i32), axis=1), e_n - 1)

    trow0 = jnp.arange(n_tiles, dtype=i32) * tr
    te = expert_of(trow0)
    tv = trow0 < jnp.sum(padded)
    row0 = jnp.arange(n_blocks, dtype=i32) * ts
    be = expert_of(row0)
    k0 = row0 - pick(be, start)
    k1 = jnp.minimum(k0 + ts, pick(be, counts))
    sel = (be[:, None] == eid[None, :])
    cb_t = jnp.sum(jnp.where(sel[:, None, :], cb[None], 0), axis=2)
    ca_t = jnp.sum(jnp.where(sel[:, None, :], ca[None], 0), axis=2)
    ov = (row0 < jnp.sum(padded))[:, None] & (cb_t < k1[:, None]) & (ca_t > k0[:, None])
    first_chunk = (jnp.arange(n_chunks) == 0)[None, :]
    ov_g = ov | (~jnp.any(ov, axis=1, keepdims=True) & first_chunk)

    def pairs(mask2d, inner):
        flat = mask2d.reshape(-1)
        n = jnp.sum(flat.astype(i32))
        idx = jnp.nonzero(flat, size=maxp, fill_value=0)[0].astype(i32)
        p = jnp.arange(maxp, dtype=i32)
        valid = p < n
        idx = jnp.where(valid, idx, jnp.max(jnp.where(valid, idx, 0)))
        outer, inn = idx // inner, idx % inner
        prev = jnp.concatenate([jnp.full((1,), -1, i32), outer[:-1]])
        nxt = jnp.concatenate([outer[1:], jnp.full((1,), -1, i32)])
        first = valid & (outer != prev)
        last = valid & ((outer != nxt) | (p == n - 1))
        return outer, inn, first.astype(i32), last.astype(i32), valid.astype(i32)

    g_tile, g_chunk, g_first, _, g_valid = pairs(ov_g, n_chunks)
    c_chunk, c_tile, c_first, c_last, c_valid = pairs(ov.T, n_blocks)
    ti = jnp.minimum(jnp.arange(n_tiles, dtype=i32), jnp.sum(tv.astype(i32)) - 1)
    return dict(pos0=pos0, pos1=pos1, g1=info[:, 2], g2=info[:, 3], te=te, tv=tv.astype(i32), ti=ti,
                gather=(g_tile, g_chunk, g_first, g_valid),
                combine=(c_tile, c_chunk, c_first, c_last, c_valid), n_tiles=n_tiles, maxp=maxp)


def _moe_gather_kernel(pt, pc, pf, pv, h_ref, p0_ref, p1_ref, g0_ref, g1_ref, xs_ref, gr_ref):
    p = pl.program_id(0)
    tr, mc = xs_ref.shape[0], h_ref.shape[0]

    @pl.when(pf[p] == 1)
    def _():
        xs_ref[...] = jnp.zeros_like(xs_ref)
        gr_ref[...] = jnp.zeros_like(gr_ref)

    @pl.when(pv[p] == 1)
    def _():
        rows = pt[p] * tr + lax.broadcasted_iota(jnp.int32, (tr, mc), 0)
        m0 = p0_ref[...] == rows
        m1 = p1_ref[...] == rows
        sel = jnp.where(m0 | m1, 1.0, 0.0).astype(BF16)
        xs_ref[...] = (xs_ref[...].astype(F32) + _dot(sel, h_ref[...])).astype(BF16)
        gr_ref[...] += jnp.sum(jnp.where(m0, g0_ref[...], 0.0) + jnp.where(m1, g1_ref[...], 0.0),
                               axis=1, keepdims=True)


def _moe_gather(h, plan, mc):
    m, d = h.shape
    tr = MOE_SEL_ROWS
    rows = plan["n_tiles"] * MOE_ROWS
    row = lambda a: a.reshape(1, m)
    tok_spec = lambda: pl.BlockSpec((1, mc), lambda p, pt, pc, pf, pv: (0, pc[p]))
    return pl.pallas_call(
        _moe_gather_kernel,
        out_shape=[jax.ShapeDtypeStruct((rows, d), BF16), jax.ShapeDtypeStruct((rows, 1), F32)],
        grid_spec=pltpu.PrefetchScalarGridSpec(
            num_scalar_prefetch=4, grid=(plan["maxp"],),
            in_specs=[pl.BlockSpec((mc, d), lambda p, pt, pc, pf, pv: (pc[p], 0)),
                      tok_spec(), tok_spec(), tok_spec(), tok_spec()],
            out_specs=[pl.BlockSpec((tr, d), lambda p, pt, pc, pf, pv: (pt[p], 0)),
                       pl.BlockSpec((tr, 1), lambda p, pt, pc, pf, pv: (pt[p], 0))]),
        compiler_params=_cparams(("arbitrary",)), name="moe_gather",
    )(*plan["gather"], h, row(plan["pos0"]), row(plan["pos1"]), row(plan["g1"]), row(plan["g2"]))


def _moe_up_kernel(te, tv, ti, x_ref, w1_ref, w3_ref, g_ref, o_ref):
    i = pl.program_id(1)

    @pl.when(tv[i] == 1)
    def _():
        x = x_ref[...]
        a = _silu(_dot(x, w1_ref[...].astype(BF16))) * _dot(x, w3_ref[...].astype(BF16))
        o_ref[...] = (a * g_ref[...]).astype(BF16)

    @pl.when(tv[i] == 0)
    def _():
        o_ref[...] = jnp.zeros_like(o_ref)


def _moe_up(xs, grow, w1, w3, layer, plan, tf):
    rows, d = xs.shape
    tr = MOE_ROWS
    fe = w1.shape[3]
    wspec = pl.BlockSpec((None, None, d, tf), lambda j, i, te, tv, ti: (layer, te[i], 0, j))
    return pl.pallas_call(
        _moe_up_kernel,
        out_shape=jax.ShapeDtypeStruct((rows, fe), BF16),
        grid_spec=pltpu.PrefetchScalarGridSpec(
            num_scalar_prefetch=3, grid=(fe // tf, rows // tr),
            in_specs=[pl.BlockSpec((tr, d), lambda j, i, te, tv, ti: (ti[i], 0)), wspec, wspec,
                      pl.BlockSpec((tr, 1), lambda j, i, te, tv, ti: (ti[i], 0))],
            out_specs=pl.BlockSpec((tr, tf), lambda j, i, te, tv, ti: (i, j))),
        compiler_params=_cparams(("arbitrary", "arbitrary")), name="moe_up",
    )(plan["te"], plan["tv"], plan["ti"], xs, w1, w3, grow)


def _moe_down_kernel(te, tv, ti, a_ref, w_ref, o_ref):
    i = pl.program_id(1)

    @pl.when(tv[i] == 1)
    def _():
        o_ref[...] = _dot(a_ref[...], w_ref[...].astype(BF16)).astype(BF16)

    @pl.when(tv[i] == 0)
    def _():
        o_ref[...] = jnp.zeros_like(o_ref)


def _moe_down(a, w2, layer, plan, tn):
    rows, fe = a.shape
    tr = MOE_ROWS
    d = w2.shape[3]
    return pl.pallas_call(
        _moe_down_kernel,
        out_shape=jax.ShapeDtypeStruct((rows, d), BF16),
        grid_spec=pltpu.PrefetchScalarGridSpec(
            num_scalar_prefetch=3, grid=(d // tn, rows // tr),
            in_specs=[pl.BlockSpec((tr, fe), lambda n, i, te, tv, ti: (ti[i], 0)),
                      pl.BlockSpec((None, None, fe, tn), lambda n, i, te, tv, ti: (layer, te[i], 0, n))],
            out_specs=pl.BlockSpec((tr, tn), lambda n, i, te, tv, ti: (i, n))),
        compiler_params=_cparams(("arbitrary", "arbitrary")), name="moe_down",
    )(plan["te"], plan["tv"], plan["ti"], a, w2)


def _moe_combine_ln_kernel(ct, cc, cf, cl, cv, ys_ref, p0_ref, p1_ref, y_ref, gate_ref, lng_ref, lnb_ref,
                           *rest, alpha, with_h, split_chunks):
    yos_ref = None
    if with_h:
        sh_ref, sc_ref, yo_ref, h_ref, acc_ref = rest
    elif split_chunks:
        yo_ref, yos_ref, acc_ref = rest
    else:
        yo_ref, acc_ref = rest
    p = pl.program_id(0)
    tr, mc = ys_ref.shape[0], y_ref.shape[0]

    @pl.when(cf[p] == 1)
    def _():
        acc_ref[...] = jnp.zeros_like(acc_ref)

    @pl.when(cv[p] == 1)
    def _():
        cols = ct[p] * tr + lax.broadcasted_iota(jnp.int32, (mc, tr), 1)
        sel = jnp.where((p0_ref[...] == cols) | (p1_ref[...] == cols), 1.0, 0.0).astype(BF16)
        acc_ref[...] += _dot(sel, ys_ref[...])

    @pl.when(cl[p] == 1)
    def _():
        z = alpha * y_ref[...] + gate_ref[...] * acc_ref[...]
        y = _ln_epilogue(z, lng_ref, lnb_ref)
        if yos_ref is None:
            yo_ref[...] = y
        else:
            @pl.when(cc[p] < split_chunks)
            def _():
                yo_ref[...] = y

            @pl.when(cc[p] >= split_chunks)
            def _():
                yos_ref[...] = y
        if with_h:
            h_ref[...] = (y * (1.0 + sc_ref[...]) + sh_ref[...]).astype(BF16)


def _moe_combine_ln(tok, ys, plan, y, mods, layer, gate_which, ln_g, ln_b, alpha, mc, h_mod=None,
                    split_rows=0):
    m, d = y.shape
    tr = MOE_SEL_ROWS
    with_h = h_mod is not None
    col = lambda a: a.reshape(m, 1)

    def mod(l, which):
        return pl.BlockSpec((None, None, 1, d),
                            lambda p, ct, cc, cf, cl, cv: (l, tok.rid(cc[p], mc), 0, which))

    chunk = lambda w: pl.BlockSpec((mc, w), lambda p, ct, cc, cf, cl, cv: (cc[p], 0))
    ins = [ys, col(plan["pos0"]), col(plan["pos1"]), y, mods, ln_g, ln_b]
    specs = [pl.BlockSpec((tr, d), lambda p, ct, cc, cf, cl, cv: (ct[p], 0)), chunk(1), chunk(1), chunk(d),
             mod(layer, gate_which), _layer_spec(ln_g, layer), _layer_spec(ln_b, layer)]
    outs = [jax.ShapeDtypeStruct((m, d), F32)]
    ospecs = [chunk(d)]
    sc = split_rows // mc
    if sc:
        assert not with_h
        outs = [jax.ShapeDtypeStruct((split_rows, d), F32), jax.ShapeDtypeStruct((m - split_rows, d), F32)]
        ospecs = [pl.BlockSpec((mc, d), lambda p, ct, cc, cf, cl, cv: (jnp.minimum(cc[p], sc - 1), 0)),
                  pl.BlockSpec((mc, d), lambda p, ct, cc, cf, cl, cv: (jnp.maximum(cc[p] - sc, 0), 0))]
    if with_h:
        hl, hsh, hsc = h_mod
        ins += [mods, mods]
        specs += [mod(hl, hsh), mod(hl, hsc)]
        outs.append(jax.ShapeDtypeStruct((m, d), BF16))
        ospecs.append(chunk(d))
    return pl.pallas_call(
        functools.partial(_moe_combine_ln_kernel, alpha=alpha, with_h=with_h, split_chunks=sc),
        out_shape=outs,
        grid_spec=pltpu.PrefetchScalarGridSpec(
            num_scalar_prefetch=5, grid=(plan["maxp"],), in_specs=specs, out_specs=ospecs,
            scratch_shapes=[pltpu.VMEM((mc, d), F32)]),
        compiler_params=_cparams(("arbitrary",)), name="moe_combine_ln",
    )(*plan["combine"], *ins)


def _rope_tables(n_tokens, dim, pad_to):
    t = jnp.arange(n_tokens)
    row = (t // GRID_W).astype(F32)
    col = (t % GRID_W).astype(F32)
    half = dim // 2
    inv_freq = ROPE_THETA ** (-jnp.arange(0, half, 2, dtype=F32) / half)
    ar = row[:, None] * inv_freq[None, :]
    ac = col[:, None] * inv_freq[None, :]
    ang = jnp.concatenate([ar, ar, ac, ac], axis=-1)
    cos, sin = jnp.cos(ang), jnp.sin(ang)
    lo = (np.arange(dim) % (dim // 2)) < dim // 4
    sa = jnp.where(lo[None, :], -sin, 0.0)
    sb = jnp.where(lo[None, :], 0.0, sin)
    return cos, sa, sb


def _pad_lanes(x, width, fill):
    return jnp.concatenate([x, jnp.full((x.shape[0], width - x.shape[1]), fill, x.dtype)], axis=1)


def kernel(x_prompt, x_sample, cache_da_k, cache_da_v, cache_mla_ckv, cache_mla_krope, cache_na_k, cache_na_v, c, c_ctx, w_ada, b_ada, w_in, da_lq1, da_lk1, da_lq2, da_lk2, da_subln, mla_gq, mla_gkv, mla_wuq, mla_wukv, na_rpb, w_out, ln1_g, ln1_b, ln2_g, ln2_b, ffn_w1, ffn_w3, ffn_w2, moe_router, moe_w1, moe_w3, moe_w2):
    nbp, seq, d = x_prompt.shape
    nbs, s_lat, _ = x_sample.shape
    depth = w_in.shape[0]
    past = cache_da_k.shape[2]
    mp, ms = nbp * seq, nbs * s_lat
    m = mp + ms
    tok = _Tok(mp, s_lat, nbs)
    tm = 512
    assert mp % s_lat == 0 and s_lat % tm == 0 and mp % tm == 0 and seq % LANE == 0 and nbs + 1 <= COND_ROWS
    alpha = (2.0 * depth) ** 0.25
    rows_n = s_lat // GRID_W

    cond = jnp.concatenate([c_ctx[None], c, jnp.zeros((COND_ROWS - 1 - nbs, d), F32)], axis=0)
    mods = _ada(cond, w_ada, b_ada).reshape(depth, COND_ROWS, 1, 6 * d)

    cos, sa, sb = _rope_tables(s_lat, DA_QK, LANE)
    rope_d = tuple(jnp.tile(t, (1, 2)) for t in (cos, sa, sb))
    rope_m = (_pad_lanes(cos, LANE, 1.0), _pad_lanes(sa, LANE, 0.0), _pad_lanes(sb, LANE, 0.0))

    y = (x_prompt.reshape(mp, d), x_sample.reshape(ms, d))
    h = _modulate(tok, y[0], y[1], mods, 0, tm)

    split = C_CKV + MLA_KV_RANK + MLA_ROPE
    w_in_p = jnp.concatenate([w_in[:, :, :split], jnp.zeros((depth, d, C_NAQ - split), F32), w_in[:, :, split:]],
                             axis=2).astype(BF16)
    wuq = mla_wuq.reshape(depth, MLA_Q_RANK, MLA_HEADS, MLA_NOPE + MLA_ROPE)
    wuq_p = jnp.concatenate(
        [wuq, jnp.zeros((depth, MLA_Q_RANK, MLA_HEADS, MLA_QK_PAD - MLA_NOPE - MLA_ROPE), F32)],
        axis=3).reshape(depth, MLA_Q_RANK, MLA_HEADS * MLA_QK_PAD).astype(BF16)
    wukv = mla_wukv.reshape(depth, MLA_KV_RANK, MLA_HEADS, MLA_NOPE + MLA_V)
    wk = wukv[..., :MLA_NOPE].reshape(depth, MLA_KV_RANK, MLA_HEADS * MLA_NOPE).astype(BF16)
    wv = wukv[..., MLA_NOPE:].reshape(depth, MLA_KV_RANK, MLA_HEADS * MLA_V).astype(BF16)
    w_out_b = w_out.astype(BF16)
    ffn_w2_b = ffn_w2.astype(BF16)
    n_moe = moe_router.shape[0]
    wr = jnp.concatenate([moe_router, jnp.zeros((n_moe, d, LANE - N_EXPERTS), F32)], axis=2).astype(BF16)
    vec = lambda a: a.reshape(a.shape[0], 1, a.shape[1])
    gq, gkv, gsub = vec(mla_gq), vec(mla_gkv), vec(da_subln)
    lams = (vec(da_lq1), vec(da_lk1), vec(da_lq2), vec(da_lk2))
    ln1g, ln1b, ln2g, ln2b = vec(ln1_g), vec(ln1_b), vec(ln2_g), vec(ln2_b)
    cda_k = cache_da_k.reshape(nbs, depth, past, DA_HEADS * LANE)
    c_ckv = cache_mla_ckv.reshape(nbs * depth * past, MLA_KV_RANK)
    c_kr = jnp.concatenate([cache_mla_krope, jnp.zeros((nbs, depth, past, LANE - MLA_ROPE), F32)],
                           axis=-1).reshape(nbs * depth * past, LANE)
    bias, na_pat = _na_bias_tables(na_rpb, rows_n)
    stacks = None
    tm_big = 1024 if (mp % 1024 == 0 and s_lat % 1024 == 0) else tm

    st = [[] for _ in range(6)]
    for l in range(depth):
        lam_init = 0.8 - 0.6 * math.exp(-0.3 * l)
        proj = _mm(h, w_in_p, l, tm_big, 1024)

        o_p, *stacks = _ctx_attention(proj, l, depth, nbp, seq, gkv, wk, wv, gq, wuq_p, lams, gsub, lam_init,
                                      stacks)

        kcat_s, vb_s, _, qcat_s, qa_s, ka_s = _mla_prep(proj, l, mp, ms, tm, gkv, wk, wv, gq, wuq_p,
                                                      rope_m=rope_m, rope_d=rope_d, s_lat=s_lat)
        kcat_c, vb_c = _mla_cache(c_ckv, c_kr, l, depth, nbs, past, wk, wv)
        tq = 512 if s_lat % 512 == 0 else s_lat
        oa_s = _da(qa_s, ka_s, proj, lams, gsub, l, lam_init, nb=nbs, sq=s_lat, sk=s_lat, tq=tq,
                   vrow0=mp, vcol0=C_DAV, kc=cda_k, vc=cache_da_v, skc=past)
        ob_s = _attn(qcat_s, kcat_s, vb_s, nb=nbs, sq=s_lat, sk=s_lat, tq=tq, heads=MLA_HEADS,
                     hp=MLA_HEADS_PER_STEP, dk=MLA_QK_PAD, dv=MLA_V, scale=MLA_SCALE, kc=kcat_c, vc=vb_c,
                     skc=past)
        oc_s = _na_latent(proj, l, mp, nbs, s_lat, cache_na_k, cache_na_v, bias, na_pat)

        i = l // 2
        moe = (l % 2 == 1)
        res = _mm_ln(tok, [o_p], w_out_b, l, y, mods, l, 2, ln1g, ln1b, alpha, tm, d, h_mod=(l, 3, 4),
                     w_router=wr if moe else None, router_layer=i, xs_tail=[oa_s, ob_s, oc_s],
                     name="out_proj_ln")
        y, h2 = res[0], res[1]

        nxt = (l + 1, 0, 1) if l + 1 < depth else None
        if not moe:
            a = _ffn_up(h2, ffn_w1, ffn_w3, i, tm_big, 512)
            res = _mm_ln(tok, [a], ffn_w2_b, i, y, mods, l, 5, ln2g, ln2b, alpha, tm, FFN_DOWN_TK,
                         h_mod=nxt, name="ffn_down_ln")
        else:
            plan = _moe_plan(res[2], res[3], m, tm)
            xs, grow = _moe_gather(h2, plan, tm)
            a = _moe_up(xs, grow, moe_w1, moe_w3, i, plan, 256)
            ys = _moe_down(a, moe_w2, i, plan, 512)
            res = _moe_combine_ln(tok, ys, plan, y, mods, l, 5, ln2g, ln2b, alpha, tm, h_mod=nxt,
                                  split_rows=0 if nxt is not None else mp)
        y = res[0] if nxt is not None else tuple(res)
        if nxt is not None:
            h = res[1]

        pp = proj[:mp]
        st[0].append(pp[:, C_DAK:C_DAK + 512].reshape(nbp, seq, DA_HEADS, 2, DA_QK))
        st[3].append(pp[:, C_CKV + MLA_KV_RANK:C_CKV + MLA_KV_RANK + MLA_ROPE].reshape(nbp, seq, MLA_ROPE))

    if not isinstance(y, tuple):
        y = (y[:mp], y[mp:])
    new_ckv, new_da_v, new_na_k, new_na_v = stacks
    return (y[0].reshape(nbp, seq, d), y[1].reshape(nbs, s_lat, d), jnp.stack(st[0], axis=1), new_da_v,
            new_ckv, jnp.stack(st[3], axis=1), new_na_k, new_na_v)
```

```python
import functools
import math

import numpy as np
import jax
import jax.numpy as jnp
from jax import lax
from jax.experimental import pallas as pl
from jax.experimental.pallas import tpu as pltpu

F32 = jnp.float32
BF16 = jnp.bfloat16

GRID_W = 64
DA_QK = 64
DA_V = 128
DA_HEADS = 4
MLA_Q_RANK = 512
MLA_KV_RANK = 256
MLA_NOPE = 128
MLA_ROPE = 64
MLA_V = 128
MLA_HEADS = 8
MLA_SCALE = (MLA_NOPE + MLA_ROPE) ** -0.5
NA_DIM = 128
NA_HEADS = 4
NA_KH = 8
NA_KW = 16
N_EXPERTS = 8
ROPE_THETA = 10000.0
LN_EPS = 1e-5
RMS_EPS = 1e-6
NEG_INF = -1e30
LOG2E = 1.4426950408889634
LANE = 128
COND_ROWS = 8
VMEM_LIMIT = 56 * 1024 * 1024

C_DAQ, C_DAK, C_DAV, C_CQ, C_CKV, C_NAQ, C_NAK, C_NAV = 0, 512, 1024, 1536, 2048, 2560, 3072, 3584
P_COLS = 4096
MLA_QK_PAD = 256
FFN_DOWN_TK = 1408
NA_ROWS_PER_STEP = 4
MLA_HEADS_PER_STEP = 4


def _cparams(sem):
    return pltpu.CompilerParams(dimension_semantics=sem, vmem_limit_bytes=VMEM_LIMIT)


def _dot(a, b):
    return jnp.dot(a, b, preferred_element_type=F32)


def _dot_nt(a, b):
    return lax.dot_general(a, b, (((1,), (1,)), ((), ())), preferred_element_type=F32)


def _silu(x):
    return x * (1.0 / (1.0 + jnp.exp(-x)))


def _rms(x, g):
    return x * lax.rsqrt(jnp.mean(x * x, axis=-1, keepdims=True) + RMS_EPS) * g


def _rope(x, cos, sa, sb):
    return x * cos + pltpu.roll(x, LANE - 16, 1) * sa + pltpu.roll(x, 16, 1) * sb


def _ada_kernel(c_ref, w_ref, b_ref, o_ref):
    s = _silu(c_ref[...]).astype(BF16)
    o_ref[...] = _dot(s, w_ref[...].astype(BF16)) + b_ref[...]


def _ada(cond, w_ada, b_ada):
    n_layers, d, n = w_ada.shape
    tn = 1024
    return pl.pallas_call(
        _ada_kernel,
        out_shape=jax.ShapeDtypeStruct((n_layers, COND_ROWS, n), F32),
        grid=(n_layers, n // tn),
        in_specs=[pl.BlockSpec((COND_ROWS, d), lambda l, j: (0, 0)),
                  pl.BlockSpec((None, d, tn), lambda l, j: (l, 0, j)),
                  pl.BlockSpec((None, 1, tn), lambda l, j: (l, 0, j))],
        out_specs=pl.BlockSpec((None, COND_ROWS, tn), lambda l, j: (l, 0, j)),
        compiler_params=_cparams(("arbitrary", "arbitrary")), name="ada",
    )(cond, w_ada, b_ada.reshape(n_layers, 1, n))


class _Tok:
    def __init__(self, mp, s_lat, n_lat_batches):
        self.mp, self.s_lat, self.nb = mp, s_lat, n_lat_batches
        self.m = mp + s_lat * n_lat_batches

    def rid(self, i, tm):
        r0 = i * tm
        return jnp.where(r0 < self.mp, 0, 1 + (r0 - self.mp) // self.s_lat)

    def mod_spec(self, layer, which, tm, d, grid_rank=1):
        if grid_rank == 1:
            return pl.BlockSpec((None, None, 1, d), lambda i: (layer, self.rid(i, tm), 0, which))
        return pl.BlockSpec((None, None, 1, d), lambda i, k: (layer, self.rid(i, tm), 0, which))


def _head_tail_specs(head, tail, tm, width, rank):
    ht = head.shape[0] // tm
    if rank == 1:
        return ht, [pl.BlockSpec((tm, width), lambda i: (jnp.minimum(i, ht - 1), 0)),
                    pl.BlockSpec((tm, width), lambda i: (jnp.maximum(i - ht, 0), 0))]
    return ht, [pl.BlockSpec((tm, width), lambda i, k: (jnp.minimum(i, ht - 1), 0)),
                pl.BlockSpec((tm, width), lambda i, k: (jnp.maximum(i - ht, 0), 0))]


def _modulate_kernel(xp_ref, xs_ref, sh_ref, sc_ref, o_ref, *, head_tiles):
    x = jnp.where(pl.program_id(0) < head_tiles, xp_ref[...], xs_ref[...])
    o_ref[...] = (x * (1.0 + sc_ref[...]) + sh_ref[...]).astype(BF16)


def _modulate(tok, x_p, x_s, mods, layer, tm):
    d = x_p.shape[1]
    ht, xspecs = _head_tail_specs(x_p, x_s, tm, d, 1)
    return pl.pallas_call(
        functools.partial(_modulate_kernel, head_tiles=ht),
        out_shape=jax.ShapeDtypeStruct((tok.m, d), BF16),
        grid=(tok.m // tm,),
        in_specs=xspecs + [tok.mod_spec(layer, 0, tm, d), tok.mod_spec(layer, 1, tm, d)],
        out_specs=pl.BlockSpec((tm, d), lambda i: (i, 0)),
        compiler_params=_cparams(("arbitrary",)), name="modulate",
    )(x_p, x_s, mods, mods)


def _mm_kernel(x_ref, w_ref, o_ref):
    o_ref[...] = _dot(x_ref[...], w_ref[...])


def _layer_spec(arr, layer):
    zeros = (0,) * (arr.ndim - 1)
    return pl.BlockSpec((None,) + arr.shape[1:], lambda *_: (layer,) + zeros)


def _mm(x, w, layer, tm, tn):
    m, k = x.shape
    n = w.shape[2]
    return pl.pallas_call(
        _mm_kernel,
        out_shape=jax.ShapeDtypeStruct((m, n), F32),
        grid=(n // tn, m // tm),
        in_specs=[pl.BlockSpec((tm, k), lambda j, i: (i, 0)),
                  pl.BlockSpec((None, k, tn), lambda j, i: (layer, 0, j))],
        out_specs=pl.BlockSpec((tm, tn), lambda j, i: (i, j)),
        compiler_params=_cparams(("arbitrary", "arbitrary")), name="in_proj",
    )(x, w)


def _lat_prep_kernel(ckv_ref, cq_ref, daq_ref, dak_ref, gkv_ref, wk_ref, wv_ref, gq_ref, wuq_ref,
                     cm_ref, sam_ref, sbm_ref, cd_ref, sad_ref, sbd_ref,
                     kcat_ref, vb_ref, qcat_ref, qa_ref, ka_ref):
    cb = _rms(ckv_ref[:, :MLA_KV_RANK], gkv_ref[...]).astype(BF16)
    kr = ckv_ref[:, MLA_KV_RANK:MLA_KV_RANK + LANE]
    kr = _rope(kr, cm_ref[...], sam_ref[...], sbm_ref[...]).astype(BF16)
    kn = _dot(cb, wk_ref[...]).astype(BF16)
    vb_ref[...] = _dot(cb, wv_ref[...]).astype(BF16)
    q = _dot(_rms(cq_ref[...], gq_ref[...]).astype(BF16), wuq_ref[...])
    for h in range(MLA_HEADS):
        lo = h * MLA_QK_PAD
        kcat_ref[:, lo:lo + MLA_NOPE] = kn[:, h * MLA_NOPE:(h + 1) * MLA_NOPE]
        kcat_ref[:, lo + MLA_NOPE:lo + MLA_QK_PAD] = kr
        qcat_ref[:, lo:lo + MLA_NOPE] = q[:, lo:lo + MLA_NOPE].astype(BF16)
        qr = _rope(q[:, lo + MLA_NOPE:lo + MLA_QK_PAD], cm_ref[...], sam_ref[...], sbm_ref[...])
        qcat_ref[:, lo + MLA_NOPE:lo + MLA_QK_PAD] = qr.astype(BF16)
    for h in range(DA_HEADS):
        sl = slice(h * LANE, (h + 1) * LANE)
        qa_ref[:, sl] = _rope(daq_ref[:, sl], cd_ref[...], sad_ref[...], sbd_ref[...]).astype(BF16)
        ka_ref[:, sl] = _rope(dak_ref[:, sl], cd_ref[...], sad_ref[...], sbd_ref[...]).astype(BF16)


def _lat_prep(proj, layer, row0, rows, s_lat, tm, gkv, wk, wv, gq, wuq, rope_m, rope_d):
    b0 = row0 // tm
    nt = s_lat // tm
    blk = lambda c: pl.BlockSpec((tm, 512), lambda i: (b0 + i, c // 512))
    table = pl.BlockSpec((tm, LANE), lambda i: (i % nt, 0))
    params = [gkv, wk, wv, gq, wuq]
    widths = [MLA_HEADS * MLA_QK_PAD, MLA_HEADS * MLA_V, MLA_HEADS * MLA_QK_PAD, DA_HEADS * LANE,
              DA_HEADS * LANE]
    return pl.pallas_call(
        _lat_prep_kernel,
        out_shape=[jax.ShapeDtypeStruct((rows, w), BF16) for w in widths],
        grid=(rows // tm,),
        in_specs=[blk(C_CKV), blk(C_CQ), blk(C_DAQ), blk(C_DAK)] + [_layer_spec(a, layer) for a in params]
        + [table] * 6,
        out_specs=[pl.BlockSpec((tm, w), lambda i: (i, 0)) for w in widths],
        compiler_params=_cparams(("arbitrary",)), name="lat_prep",
    )(proj, proj, proj, proj, *params, *rope_m, *rope_d)


def _mla_cache_kernel(ckv_ref, kr_ref, wk_ref, wv_ref, kcat_ref, vb_ref):
    cb = ckv_ref[...].astype(BF16)
    kn = _dot(cb, wk_ref[...]).astype(BF16)
    vb_ref[...] = _dot(cb, wv_ref[...]).astype(BF16)
    kr = kr_ref[...].astype(BF16)
    for h in range(MLA_HEADS):
        kcat_ref[:, h * MLA_QK_PAD:h * MLA_QK_PAD + MLA_NOPE] = kn[:, h * MLA_NOPE:(h + 1) * MLA_NOPE]
        kcat_ref[:, h * MLA_QK_PAD + MLA_NOPE:(h + 1) * MLA_QK_PAD] = kr


def _mla_cache(ckv, kr, layer, depth, nb, past, wk, wv):
    rows = lambda w: pl.BlockSpec((past, w), lambda b: (b * depth + layer, 0))
    out = lambda w: pl.BlockSpec((past, w), lambda b: (b, 0))
    return pl.pallas_call(
        _mla_cache_kernel,
        out_shape=[jax.ShapeDtypeStruct((nb * past, MLA_HEADS * MLA_QK_PAD), BF16),
                   jax.ShapeDtypeStruct((nb * past, MLA_HEADS * MLA_V), BF16)],
        grid=(nb,),
        in_specs=[rows(MLA_KV_RANK), rows(LANE), _layer_spec(wk, layer), _layer_spec(wv, layer)],
        out_specs=[out(MLA_HEADS * MLA_QK_PAD), out(MLA_HEADS * MLA_V)],
        compiler_params=_cparams(("arbitrary",)), name="mla_cache",
    )(ckv, kr, wk, wv)


def _softmax_parts(s, s2, scale):
    c = scale * LOG2E
    m = jnp.max(s, axis=-1, keepdims=True)
    if s2 is not None:
        m = jnp.maximum(m, jnp.max(s2, axis=-1, keepdims=True))
    e = jnp.exp2((s - m) * c)
    den = jnp.sum(e, axis=-1, keepdims=True)
    e2 = None
    if s2 is not None:
        e2 = jnp.exp2((s2 - m) * c)
        den = den + jnp.sum(e2, axis=-1, keepdims=True)
    return e, e2, 1.0 / den


def _attn_head(q, k, v, scale, kc=None, vc=None):
    s = _dot_nt(q, k)
    s2 = _dot_nt(q, kc) if kc is not None else None
    e, e2, inv = _softmax_parts(s, s2, scale)
    o = _dot(e.astype(BF16), v)
    if kc is not None:
        o = o + _dot(e2.astype(BF16), vc)
    return o * inv


def _attn_kernel(*refs, heads, dk, dv, scale, has_ctx):
    if has_ctx:
        q_ref, k_ref, v_ref, kc_ref, vc_ref, o_ref = refs
    else:
        q_ref, k_ref, v_ref, o_ref = refs
    for h in range(heads):
        ks, vs = slice(h * dk, (h + 1) * dk), slice(h * dv, (h + 1) * dv)
        o = _attn_head(q_ref[:, ks].astype(BF16), k_ref[:, ks].astype(BF16), v_ref[:, vs].astype(BF16), scale,
                       kc_ref[:, ks].astype(BF16) if has_ctx else None,
                       vc_ref[:, vs].astype(BF16) if has_ctx else None)
        o_ref[:, vs] = o.astype(BF16)


def _attn(q, k, v, *, nb, sq, sk, tq, heads, hp, dk, dv, scale, qrow0=0, krow0=0, qcol0=0, kcol0=0, vcol0=0,
          kc=None, vc=None, skc=0):
    nq = sq // tq
    ng = heads // hp
    wq, wv = hp * dk, hp * dv
    has_ctx = kc is not None
    ins = [q, k, v]
    specs = [pl.BlockSpec((tq, wq), lambda b, g, i: (qrow0 // tq + b * nq + i, qcol0 // wq + g)),
             pl.BlockSpec((sk, wq), lambda b, g, i: (krow0 // sk + b, kcol0 // wq + g)),
             pl.BlockSpec((sk, wv), lambda b, g, i: (krow0 // sk + b, vcol0 // wv + g))]
    if has_ctx:
        ins += [kc, vc]
        specs += [pl.BlockSpec((skc, wq), lambda b, g, i: (b, g)),
                  pl.BlockSpec((skc, wv), lambda b, g, i: (b, g))]
    return pl.pallas_call(
        functools.partial(_attn_kernel, heads=hp, dk=dk, dv=dv, scale=scale, has_ctx=has_ctx),
        out_shape=jax.ShapeDtypeStruct((nb * sq, heads * dv), BF16),
        grid=(nb, ng, nq), in_specs=specs,
        out_specs=pl.BlockSpec((tq, wv), lambda b, g, i: (b * nq + i, g)),
        compiler_params=_cparams(("arbitrary", "arbitrary", "arbitrary")), name="attn_h%d" % heads,
    )(*ins)


def _da_lambda(lq1, lk1, lq2, lk2, lam_init):
    return (jnp.exp(jnp.sum(lq1[...] * lk1[...], axis=-1, keepdims=True))
            - jnp.exp(jnp.sum(lq2[...] * lk2[...], axis=-1, keepdims=True)) + lam_init)


def _da_head(q, k, v, lam, g, lam_init, kc=None, vc=None):
    first = lax.broadcasted_iota(jnp.int32, (1, LANE), 1) < DA_QK
    q = q.astype(F32) * (DA_QK ** -0.5)
    qs = (jnp.where(first, q, 0.0).astype(BF16), jnp.where(first, 0.0, q).astype(BF16))
    o = _attn_head(qs[0], k, v, 1.0, kc, vc) - lam * _attn_head(qs[1], k, v, 1.0, kc, vc)
    return _rms(o, g) * (1.0 - lam_init)


def _da_kernel(*refs, has_ctx, lam_init):
    if has_ctx:
        q_ref, k_ref, v_ref, kc_ref, vc_ref, lq1, lk1, lq2, lk2, g_ref, o_ref = refs
    else:
        q_ref, k_ref, v_ref, lq1, lk1, lq2, lk2, g_ref, o_ref = refs
    lam = _da_lambda(lq1, lk1, lq2, lk2, lam_init)
    for h in range(DA_HEADS):
        sl = slice(h * LANE, (h + 1) * LANE)
        o = _da_head(q_ref[:, sl], k_ref[:, sl].astype(BF16), v_ref[:, sl].astype(BF16), lam, g_ref[...],
                     lam_init, kc_ref[:, sl].astype(BF16) if has_ctx else None,
                     vc_ref[:, h, :].astype(BF16) if has_ctx else None)
        o_ref[:, sl] = o.astype(BF16)


def _ctx_kernel(p_ref, gkv_ref, wk_ref, wv_ref, gq_ref, wuq_ref, lq1, lk1, lq2, lk2, g_ref, *rest, lam_init):
    o_ref, ckvn_ref, dav_ref, nak_ref, nav_ref = rest[-5:]
    for h in range(DA_HEADS):
        dav_ref[:, h, :] = p_ref[:, C_DAV + h * DA_V:C_DAV + (h + 1) * DA_V]
    for h in range(NA_HEADS):
        nak_ref[:, h, :] = p_ref[:, C_NAK + h * NA_DIM:C_NAK + (h + 1) * NA_DIM]
        nav_ref[:, h, :] = p_ref[:, C_NAV + h * NA_DIM:C_NAV + (h + 1) * NA_DIM]
    col = lambda c0, h, w=LANE: slice(c0 + h * w, c0 + (h + 1) * w)
    lam = _da_lambda(lq1, lk1, lq2, lk2, lam_init)
    for h in range(DA_HEADS):
        o = _da_head(p_ref[:, col(C_DAQ, h)], p_ref[:, col(C_DAK, h)].astype(BF16),
                     p_ref[:, col(C_DAV, h)].astype(BF16), lam, g_ref[...], lam_init)
        o_ref[:, col(0, h)] = o.astype(BF16)
    ckvn = _rms(p_ref[:, C_CKV:C_CKV + MLA_KV_RANK], gkv_ref[...])
    ckvn_ref[...] = ckvn
    cb = ckvn.astype(BF16)
    kn = _dot(cb, wk_ref[...]).astype(BF16)
    vb = _dot(cb, wv_ref[...]).astype(BF16)
    kr = p_ref[:, C_CKV + MLA_KV_RANK:C_CKV + MLA_KV_RANK + LANE].astype(BF16)
    q = _dot(_rms(p_ref[:, C_CQ:C_CQ + MLA_Q_RANK], gq_ref[...]).astype(BF16), wuq_ref[...]).astype(BF16)
    ob0 = DA_HEADS * DA_V
    for h in range(MLA_HEADS):
        kh = jnp.concatenate([kn[:, col(0, h)], kr], axis=1)
        o = _attn_head(q[:, col(0, h, MLA_QK_PAD)], kh, vb[:, col(0, h)], MLA_SCALE)
        o_ref[:, col(ob0, h)] = o.astype(BF16)
    oc0 = ob0 + MLA_HEADS * MLA_V
    for h in range(NA_HEADS):
        o = _attn_head(p_ref[:, col(C_NAQ, h)].astype(BF16), p_ref[:, col(C_NAK, h)].astype(BF16),
                       p_ref[:, col(C_NAV, h)].astype(BF16), NA_DIM ** -0.5)
        o_ref[:, col(oc0, h)] = o.astype(BF16)


def _ctx_attention(proj, layer, depth, nb, seq, gkv, wk, wv, gq, wuq, lams, g, lam_init, stacks):
    d_out = DA_HEADS * DA_V + MLA_HEADS * MLA_V + NA_HEADS * NA_DIM
    params = [gkv, wk, wv, gq, wuq] + list(lams) + [g]
    stack_shapes = [(nb, depth, seq, MLA_KV_RANK), (nb, depth, seq, DA_HEADS, DA_V),
                    (nb, depth, seq, NA_HEADS, NA_DIM), (nb, depth, seq, NA_HEADS, NA_DIM)]
    stack_specs = [pl.BlockSpec((None, None) + s[2:], lambda b, n=len(s): (b, layer) + (0,) * (n - 2))
                   for s in stack_shapes]
    prev = list(stacks) if stacks is not None else []
    n_in = 1 + len(params)
    return pl.pallas_call(
        functools.partial(_ctx_kernel, lam_init=lam_init),
        out_shape=[jax.ShapeDtypeStruct((nb * seq, d_out), BF16)]
        + [jax.ShapeDtypeStruct(s, F32) for s in stack_shapes],
        grid=(nb,),
        in_specs=[pl.BlockSpec((seq, P_COLS), lambda b: (b, 0))] + [_layer_spec(a, layer) for a in params]
        + [pl.BlockSpec(memory_space=pl.ANY)] * len(prev),
        out_specs=[pl.BlockSpec((seq, d_out), lambda b: (b, 0))] + stack_specs,
        input_output_aliases={n_in + k: 1 + k for k in range(len(prev))},
        compiler_params=_cparams(("arbitrary",)), name="ctx_attention",
    )(proj, *params, *prev)


def _da(q, k, v, lams, g, layer, lam_init, *, nb, sq, sk, tq, qrow0=0, krow0=0, qcol0=0, kcol0=0, vcol0=0,
        vrow0=None, kc=None, vc=None, skc=0):
    nq = sq // tq
    w = DA_HEADS * LANE
    vrow0 = krow0 if vrow0 is None else vrow0
    has_ctx = kc is not None
    ins = [q, k, v]
    specs = [pl.BlockSpec((tq, w), lambda b, i: (qrow0 // tq + b * nq + i, qcol0 // w)),
             pl.BlockSpec((sk, w), lambda b, i: (krow0 // sk + b, kcol0 // w)),
             pl.BlockSpec((sk, w), lambda b, i: (vrow0 // sk + b, vcol0 // w))]
    if has_ctx:
        ins += [kc, vc]
        specs += [pl.BlockSpec((None, None, skc, w), lambda b, i: (b, layer, 0, 0)),
                  pl.BlockSpec((None, None, skc, DA_HEADS, DA_V), lambda b, i: (b, layer, 0, 0, 0))]
    ins += list(lams) + [g]
    specs += [_layer_spec(a, layer) for a in ins[-5:]]
    return pl.pallas_call(
        functools.partial(_da_kernel, has_ctx=has_ctx, lam_init=lam_init),
        out_shape=jax.ShapeDtypeStruct((nb * sq, w), BF16),
        grid=(nb, nq), in_specs=specs,
        out_specs=pl.BlockSpec((tq, w), lambda b, i: (b * nq + i, 0)),
        compiler_params=_cparams(("arbitrary", "arbitrary")), name="diff_attn",
    )(*ins)


def _na_geometry(rows_n):
    kh = min(NA_KH, rows_n)
    rb = NA_ROWS_PER_STEP if rows_n % NA_ROWS_PER_STEP == 0 else 1
    uw = min(rows_n, kh + rb - 1)
    return kh, rb, uw


def _na_kernel(pat, q_ref, k_ref, v_ref, kc_ref, vc_ref, bias_ref, o_ref, *, rows_n, kh, rb, uw):
    k0 = jnp.clip(pl.program_id(1) * rb - kh // 2, 0, rows_n - uw)
    start = pl.multiple_of(k0 * GRID_W, GRID_W)
    nwin = uw * GRID_W
    scale = NA_DIM ** -0.5
    for h in range(NA_HEADS):
        sl = slice(h * NA_DIM, (h + 1) * NA_DIM)
        q = q_ref[:, sl].astype(BF16)
        kw = k_ref[pl.ds(start, nwin), sl].astype(BF16)
        vw = v_ref[pl.ds(start, nwin), sl].astype(BF16)
        s = _dot_nt(q, kw) * scale + bias_ref[h]
        s2 = _dot_nt(q, kc_ref[:, h, :].astype(BF16)) * scale
        e, e2, inv = _softmax_parts(s, s2, 1.0)
        o = _dot(e.astype(BF16), vw) + _dot(e2.astype(BF16), vc_ref[:, h, :].astype(BF16))
        o_ref[:, sl] = (o * inv).astype(BF16)


def _na_latent(proj, layer, row0, nb, s_lat, kc, vc, bias, pattern_of_step):
    rows_n = s_lat // GRID_W
    kh, rb, uw = _na_geometry(rows_n)
    w = NA_HEADS * NA_DIM
    skc = kc.shape[2]
    tq = rb * GRID_W
    steps = rows_n // rb
    return pl.pallas_call(
        functools.partial(_na_kernel, rows_n=rows_n, kh=kh, rb=rb, uw=uw),
        out_shape=jax.ShapeDtypeStruct((nb * s_lat, w), BF16),
        grid_spec=pltpu.PrefetchScalarGridSpec(
            num_scalar_prefetch=1, grid=(nb, steps),
            in_specs=[pl.BlockSpec((tq, w), lambda b, r, pat: (row0 // tq + b * steps + r, C_NAQ // w)),
                      pl.BlockSpec((s_lat, w), lambda b, r, pat: (row0 // s_lat + b, C_NAK // w)),
                      pl.BlockSpec((s_lat, w), lambda b, r, pat: (row0 // s_lat + b, C_NAV // w)),
                      pl.BlockSpec((None, None, skc, NA_HEADS, NA_DIM), lambda b, r, pat: (b, layer, 0, 0, 0)),
                      pl.BlockSpec((None, None, skc, NA_HEADS, NA_DIM), lambda b, r, pat: (b, layer, 0, 0, 0)),
                      pl.BlockSpec((None, NA_HEADS, None, tq, uw * GRID_W),
                                   lambda b, r, pat: (layer, 0, pat[r], 0, 0))],
            out_specs=pl.BlockSpec((tq, w), lambda b, r, pat: (b * steps + r, 0))),
        compiler_params=_cparams(("arbitrary", "arbitrary")), name="na_latent",
    )(pattern_of_step, proj, proj, proj, kc, vc, bias)


def _na_bias_tables(rpb, rows_n):
    kh = min(NA_KH, rows_n)
    qc = np.arange(GRID_W)[:, None]
    kc = np.arange(GRID_W)[None, :]
    col_start = np.clip(qc - NA_KW // 2, 0, GRID_W - NA_KW)
    valid = (kc >= col_start) & (kc < col_start + NA_KW)
    coff = np.clip(kc - qc, -(NA_KW - 1), NA_KW - 1) + (NA_KW - 1)
    onehot = (coff.reshape(-1)[None, :] == np.arange(2 * NA_KW - 1)[:, None]).astype(np.float32)
    n_l = rpb.shape[0]
    t = jnp.einsum('lhrc,cx->lhrx', rpb.astype(F32), jnp.asarray(onehot), precision=lax.Precision.HIGHEST)
    t = t.reshape(n_l, NA_HEADS, 2 * NA_KH - 1, GRID_W, GRID_W)
    t = jnp.where(jnp.asarray(valid)[None, None, None], t, NEG_INF)
    strips = []
    for v in range(kh):
        lo = NA_KH - 1 - v
        s = t[:, :, lo:lo + kh]
        strips.append(jnp.transpose(s, (0, 1, 3, 2, 4)).reshape(n_l, NA_HEADS, GRID_W, kh * GRID_W))
    _, rb, uw = _na_geometry(rows_n)
    patterns, pattern_of_step = [], []
    for t0 in range(rows_n // rb):
        k0 = int(np.clip(t0 * rb - kh // 2, 0, rows_n - uw))
        key = []
        for i in range(rb):
            r = t0 * rb + i
            rs = int(np.clip(r - kh // 2, 0, rows_n - kh))
            key.append((rs - k0, r - rs))
        key = tuple(key)
        if key not in patterns:
            patterns.append(key)
        pattern_of_step.append(patterns.index(key))
    blocks = []
    for key in patterns:
        rows = [jnp.pad(strips[v], ((0, 0), (0, 0), (0, 0), (off * GRID_W, (uw - kh - off) * GRID_W)),
                        constant_values=NEG_INF) for off, v in key]
        blocks.append(jnp.concatenate(rows, axis=2))
    return jnp.stack(blocks, axis=2), jnp.asarray(pattern_of_step, jnp.int32)


def _ln_epilogue(z, g_ref, b_ref):
    mu = jnp.mean(z, axis=-1, keepdims=True)
    zc = z - mu
    var = jnp.mean(zc * zc, axis=-1, keepdims=True)
    return zc * lax.rsqrt(var + LN_EPS) * g_ref[...] + b_ref[...]


def _router_info(h, wr_ref, cnt_ref):
    n = h.shape[0]
    logits = _dot(h, wr_ref[...])
    lane = lax.broadcasted_iota(jnp.int32, logits.shape, 1)
    lg = jnp.where(lane < N_EXPERTS, logits, -jnp.inf)
    m1 = jnp.max(lg, axis=-1, keepdims=True)
    i1 = jnp.min(jnp.where(lg == m1, lane, LANE), axis=-1, keepdims=True)
    lg2 = jnp.where(lane == i1, -jnp.inf, lg)
    m2 = jnp.max(lg2, axis=-1, keepdims=True)
    i2 = jnp.min(jnp.where(lg2 == m2, lane, LANE), axis=-1, keepdims=True)
    e2 = jnp.exp(m2 - m1)
    inv = 1.0 / (1.0 + e2)
    oh1, oh2 = lane == i1, lane == i2
    o1, o2 = jnp.where(oh1, 1.0, 0.0), jnp.where(oh2, 1.0, 0.0)
    below = lax.broadcasted_iota(jnp.int32, (n, n), 1) < lax.broadcasted_iota(jnp.int32, (n, n), 0)
    tri = jnp.where(below, 1.0, 0.0).astype(BF16)
    p1 = _dot(tri, o1.astype(BF16))
    p2 = _dot(tri, o2.astype(BF16))
    tot1 = jnp.sum(o1, axis=0, keepdims=True)
    tot2 = jnp.sum(o2, axis=0, keepdims=True)
    cnt = cnt_ref[...]
    rank0 = jnp.sum(jnp.where(oh1, cnt + p1, 0.0), axis=-1, keepdims=True)
    rank1 = jnp.sum(jnp.where(oh2, cnt + tot1 + p2, 0.0), axis=-1, keepdims=True)
    cnt_ref[...] = cnt + tot1 + tot2
    cols = (i1.astype(F32), i2.astype(F32), inv, e2 * inv, rank0, rank1)
    info = jnp.zeros(logits.shape, F32)
    for c, v in enumerate(cols):
        info = jnp.where(lane == c, v, info)
    return info


def _mm_ln_kernel(*refs, n_x, n_tail, head_tiles, y_head_tiles, k_total, tk, alpha, with_h, with_router,
                  mask_k):
    it = iter(refs)
    x_refs = [next(it) for _ in range(n_x)]
    tail_refs = [next(it) for _ in range(n_tail)]
    w_ref, y_ref = next(it), next(it)
    ys_ref = next(it) if y_head_tiles else None
    gate_ref, lng_ref, lnb_ref = next(it), next(it), next(it)
    sh_ref = next(it) if with_h else None
    sc_ref = next(it) if with_h else None
    wr_ref = next(it) if with_router else None
    yo_ref = next(it)
    h_ref = next(it) if with_h else None
    go_ref = next(it) if with_router else None
    ca_ref = next(it) if with_router else None
    acc_ref = next(it)
    cnt_ref = next(it) if with_router else None
    k = pl.program_id(1)
    nk = pl.num_programs(1)
    if with_router:
        @pl.when((pl.program_id(0) == 0) & (k == 0))
        def _():
            cnt_ref[...] = jnp.zeros_like(cnt_ref)
    cat = lambda rs: rs[0][...] if len(rs) == 1 else jnp.concatenate([r[...] for r in rs], axis=1)
    x = cat(x_refs)
    if n_tail:
        x = jnp.where(pl.program_id(0) < head_tiles, x, cat(tail_refs))
    w = w_ref[...]
    if mask_k:
        lim = k_total - k * tk
        x = jnp.where(lax.broadcasted_iota(jnp.int32, x.shape, 1) < lim, x, jnp.zeros_like(x))
        w = jnp.where(lax.broadcasted_iota(jnp.int32, w.shape, 0) < lim, w, jnp.zeros_like(w))
    part = _dot(x, w.astype(BF16))

    @pl.when(k == 0)
    def _():
        acc_ref[...] = part

    @pl.when(k > 0)
    def _():
        acc_ref[...] += part

    @pl.when(k == nk - 1)
    def _():
        y_in = y_ref[...]
        if y_head_tiles:
            y_in = jnp.where(pl.program_id(0) < y_head_tiles, y_in, ys_ref[...])
        z = alpha * y_in + gate_ref[...] * acc_ref[...]
        y = _ln_epilogue(z, lng_ref, lnb_ref)
        yo_ref[...] = y
        if with_h:
            h = (y * (1.0 + sc_ref[...]) + sh_ref[...]).astype(BF16)
            h_ref[...] = h
            if with_router:
                go_ref[...] = _router_info(h, wr_ref, cnt_ref)
                ca_ref[...] = cnt_ref[...]


def _mm_ln(tok, xs, w, w_layer, y, mods, layer, gate_which, ln_g, ln_b, alpha, tm, tk, h_mod=None,
           w_router=None, router_layer=0, xs_tail=None, name="mm_ln"):
    m, d = tok.m, w.shape[2]
    k_total = w.shape[1]
    nk = pl.cdiv(k_total, tk)
    with_h = h_mod is not None
    with_router = w_router is not None
    ins, specs = [], []
    xs_tail = xs_tail or []
    head_tiles = xs[0].shape[0] // tm if xs_tail else 0
    for x in xs:
        wx = x.shape[1] if (len(xs) > 1 or xs_tail) else tk
        ins.append(x)
        if xs_tail:
            specs.append(pl.BlockSpec((tm, wx), lambda i, k: (jnp.minimum(i, head_tiles - 1), k)))
        else:
            specs.append(pl.BlockSpec((tm, wx), lambda i, k: (i, k)))
    for x in xs_tail:
        ins.append(x)
        specs.append(pl.BlockSpec((tm, x.shape[1]), lambda i, k: (jnp.maximum(i - head_tiles, 0), k)))
    ins.append(w)
    specs.append(pl.BlockSpec((None, tk, d), lambda i, k: (w_layer, k, 0)))
    y_head_tiles = 0
    if isinstance(y, tuple):
        y_head_tiles, yspecs = _head_tail_specs(y[0], y[1], tm, d, 2)
        ins += list(y)
        specs += yspecs
    else:
        ins.append(y)
        specs.append(pl.BlockSpec((tm, d), lambda i, k: (i, 0)))
    ins += [mods, ln_g, ln_b]
    specs += [tok.mod_spec(layer, gate_which, tm, d, 2),
              _layer_spec(ln_g, layer), _layer_spec(ln_b, layer)]
    outs = [jax.ShapeDtypeStruct((m, d), F32)]
    ospecs = [pl.BlockSpec((tm, d), lambda i, k: (i, 0))]
    if with_h:
        hl, hsh, hsc = h_mod
        ins += [mods, mods]
        specs += [tok.mod_spec(hl, hsh, tm, d, 2), tok.mod_spec(hl, hsc, tm, d, 2)]
        outs.append(jax.ShapeDtypeStruct((m, d), BF16))
        ospecs.append(pl.BlockSpec((tm, d), lambda i, k: (i, 0)))
    if with_router:
        ins.append(w_router)
        specs.append(_layer_spec(w_router, router_layer))
        outs += [jax.ShapeDtypeStruct((m, LANE), F32), jax.ShapeDtypeStruct((m // tm, 1, LANE), F32)]
        ospecs += [pl.BlockSpec((tm, LANE), lambda i, k: (i, 0)),
                   pl.BlockSpec((None, 1, LANE), lambda i, k: (i, 0, 0))]
    scratch = [pltpu.VMEM((tm, d), F32)]
    if with_router:
        scratch.append(pltpu.VMEM((1, LANE), F32))
    res = pl.pallas_call(
        functools.partial(_mm_ln_kernel, n_x=len(xs), n_tail=len(xs_tail), head_tiles=head_tiles,
                          y_head_tiles=y_head_tiles, k_total=k_total, tk=tk, alpha=alpha, with_h=with_h,
                          with_router=with_router, mask_k=(k_total % tk != 0)),
        out_shape=outs, grid=(m // tm, nk), in_specs=specs, out_specs=ospecs,
        scratch_shapes=scratch,
        compiler_params=_cparams(("arbitrary", "arbitrary")), name=name,
    )(*ins)
    return res


def _ffn_up_kernel(x_ref, w1_ref, w3_ref, o_ref):
    x = x_ref[...]
    a = _silu(_dot(x, w1_ref[...].astype(BF16))) * _dot(x, w3_ref[...].astype(BF16))
    o_ref[...] = a.astype(BF16)


def _ffn_up(x, w1, w3, layer, tm, tf):
    m, d = x.shape
    f = w1.shape[2]
    wspec = pl.BlockSpec((None, d, tf), lambda j, i: (layer, 0, j))
    return pl.pallas_call(
        _ffn_up_kernel,
        out_shape=jax.ShapeDtypeStruct((m, f), BF16),
        grid=(pl.cdiv(f, tf), m // tm),
        in_specs=[pl.BlockSpec((tm, d), lambda j, i: (i, 0)), wspec, wspec],
        out_specs=pl.BlockSpec((tm, tf), lambda j, i: (i, j)),
        compiler_params=_cparams(("arbitrary", "arbitrary")), name="ffn_up",
    )(x, w1, w3)


MOE_ROWS = 1024
MOE_SEL_ROWS = 256


def _moe_plan(info, c_after, m, mc):
    tr, ts = MOE_ROWS, MOE_SEL_ROWS
    e_n = N_EXPERTS
    n_chunks = m // mc
    n_tiles = (2 * m) // tr + e_n
    n_blocks = n_tiles * (tr // ts)
    maxp = n_blocks + e_n * n_chunks
    i32 = jnp.int32
    i1, i2 = info[:, 0].astype(i32), info[:, 1].astype(i32)
    r0, r1 = info[:, 4].astype(i32), info[:, 5].astype(i32)
    ca = c_after[:, 0, :e_n].astype(i32)
    cb = jnp.concatenate([jnp.zeros((1, e_n), i32), ca[:-1]], axis=0)
    counts = ca[-1]
    padded = ((counts + tr - 1) // tr) * tr
    start = jnp.cumsum(padded) - padded
    eid = jnp.arange(e_n, dtype=i32)

    def pick(idx, table):
        return jnp.sum(jnp.where(idx[:, None] == eid[None, :], table[None, :], 0), axis=1)

    pos0 = pick(i1, start) + r0
    pos1 = pick(i2, start) + r1
    def expert_of(row0):
        return jnp.minimum(jnp.sum((row0[:, None] >= (start + padded)[None, :]).astype(i32), axis=1), e_n - 1)

    trow0 = jnp.arange(n_tiles, dtype=i32) * tr
    te = expert_of(trow0)
    tv = trow0 < jnp.sum(padded)
    row0 = jnp.arange(n_blocks, dtype=i32) * ts
    be = expert_of(row0)
    k0 = row0 - pick(be, start)
    k1 = jnp.minimum(k0 + ts, pick(be, counts))
    sel = (be[:, None] == eid[None, :])
    cb_t = jnp.sum(jnp.where(sel[:, None, :], cb[None], 0), axis=2)
    ca_t = jnp.sum(jnp.where(sel[:, None, :], ca[None], 0), axis=2)
    ov = (row0 < jnp.sum(padded))[:, None] & (cb_t < k1[:, None]) & (ca_t > k0[:, None])
    first_chunk = (jnp.arange(n_chunks) == 0)[None, :]
    ov_g = ov | (~jnp.any(ov, axis=1, keepdims=True) & first_chunk)

    def pairs(mask2d, inner):
        flat = mask2d.reshape(-1)
        n = jnp.sum(flat.astype(i32))
        idx = jnp.nonzero(flat, size=maxp, fill_value=0)[0].astype(i32)
        p = jnp.arange(maxp, dtype=i32)
        valid = p < n
        idx = jnp.where(valid, idx, jnp.max(jnp.where(valid, idx, 0)))
        outer, inn = idx // inner, idx % inner
        prev = jnp.concatenate([jnp.full((1,), -1, i32), outer[:-1]])
        nxt = jnp.concatenate([outer[1:], jnp.full((1,), -1, i32)])
        first = valid & (outer != prev)
        last = valid & ((outer != nxt) | (p == n - 1))
        return outer, inn, first.astype(i32), last.astype(i32), valid.astype(i32)

    g_tile, g_chunk, g_first, _, g_valid = pairs(ov_g, n_chunks)
    c_chunk, c_tile, c_first, c_last, c_valid = pairs(ov.T, n_blocks)
    ti = jnp.minimum(jnp.arange(n_tiles, dtype=i32), jnp.sum(tv.astype(i32)) - 1)
    return dict(pos0=pos0, pos1=pos1, g1=info[:, 2], g2=info[:, 3], te=te, tv=tv.astype(i32), ti=ti,
                gather=(g_tile, g_chunk, g_first, g_valid),
                combine=(c_tile, c_chunk, c_first, c_last, c_valid), n_tiles=n_tiles, maxp=maxp)


def _moe_gather_kernel(pt, pc, pf, pv, h_ref, p0_ref, p1_ref, g0_ref, g1_ref, xs_ref, gr_ref):
    p = pl.program_id(0)
    tr, mc = xs_ref.shape[0], h_ref.shape[0]

    @pl.when(pf[p] == 1)
    def _():
        xs_ref[...] = jnp.zeros_like(xs_ref)
        gr_ref[...] = jnp.zeros_like(gr_ref)

    @pl.when(pv[p] == 1)
    def _():
        rows = pt[p] * tr + lax.broadcasted_iota(jnp.int32, (tr, mc), 0)
        m0 = p0_ref[...] == rows
        m1 = p1_ref[...] == rows
        sel = jnp.where(m0 | m1, 1.0, 0.0).astype(BF16)
        xs_ref[...] = (xs_ref[...].astype(F32) + _dot(sel, h_ref[...])).astype(BF16)
        gr_ref[...] += jnp.sum(jnp.where(m0, g0_ref[...], 0.0) + jnp.where(m1, g1_ref[...], 0.0),
                               axis=1, keepdims=True)


def _moe_gather(h, plan, mc):
    m, d = h.shape
    tr = MOE_SEL_ROWS
    rows = plan["n_tiles"] * MOE_ROWS
    row = lambda a: a.reshape(1, m)
    tok_spec = lambda: pl.BlockSpec((1, mc), lambda p, pt, pc, pf, pv: (0, pc[p]))
    return pl.pallas_call(
        _moe_gather_kernel,
        out_shape=[jax.ShapeDtypeStruct((rows, d), BF16), jax.ShapeDtypeStruct((rows, 1), F32)],
        grid_spec=pltpu.PrefetchScalarGridSpec(
            num_scalar_prefetch=4, grid=(plan["maxp"],),
            in_specs=[pl.BlockSpec((mc, d), lambda p, pt, pc, pf, pv: (pc[p], 0)),
                      tok_spec(), tok_spec(), tok_spec(), tok_spec()],
            out_specs=[pl.BlockSpec((tr, d), lambda p, pt, pc, pf, pv: (pt[p], 0)),
                       pl.BlockSpec((tr, 1), lambda p, pt, pc, pf, pv: (pt[p], 0))]),
        compiler_params=_cparams(("arbitrary",)), name="moe_gather",
    )(*plan["gather"], h, row(plan["pos0"]), row(plan["pos1"]), row(plan["g1"]), row(plan["g2"]))


def _moe_up_kernel(te, tv, ti, x_ref, w1_ref, w3_ref, g_ref, o_ref):
    i = pl.program_id(1)

    @pl.when(tv[i] == 1)
    def _():
        x = x_ref[...]
        a = _silu(_dot(x, w1_ref[...].astype(BF16))) * _dot(x, w3_ref[...].astype(BF16))
        o_ref[...] = (a * g_ref[...]).astype(BF16)

    @pl.when(tv[i] == 0)
    def _():
        o_ref[...] = jnp.zeros_like(o_ref)


def _moe_up(xs, grow, w1, w3, layer, plan, tf):
    rows, d = xs.shape
    tr = MOE_ROWS
    fe = w1.shape[3]
    wspec = pl.BlockSpec((None, None, d, tf), lambda j, i, te, tv, ti: (layer, te[i], 0, j))
    return pl.pallas_call(
        _moe_up_kernel,
        out_shape=jax.ShapeDtypeStruct((rows, fe), BF16),
        grid_spec=pltpu.PrefetchScalarGridSpec(
            num_scalar_prefetch=3, grid=(fe // tf, rows // tr),
            in_specs=[pl.BlockSpec((tr, d), lambda j, i, te, tv, ti: (ti[i], 0)), wspec, wspec,
                      pl.BlockSpec((tr, 1), lambda j, i, te, tv, ti: (ti[i], 0))],
            out_specs=pl.BlockSpec((tr, tf), lambda j, i, te, tv, ti: (i, j))),
        compiler_params=_cparams(("arbitrary", "arbitrary")), name="moe_up",
    )(plan["te"], plan["tv"], plan["ti"], xs, w1, w3, grow)


def _moe_down_kernel(te, tv, ti, a_ref, w_ref, o_ref):
    i = pl.program_id(1)

    @pl.when(tv[i] == 1)
    def _():
        o_ref[...] = _dot(a_ref[...], w_ref[...].astype(BF16)).astype(BF16)

    @pl.when(tv[i] == 0)
    def _():
        o_ref[...] = jnp.zeros_like(o_ref)


def _moe_down(a, w2, layer, plan, tn):
    rows, fe = a.shape
    tr = MOE_ROWS
    d = w2.shape[3]
    return pl.pallas_call(
        _moe_down_kernel,
        out_shape=jax.ShapeDtypeStruct((rows, d), BF16),
        grid_spec=pltpu.PrefetchScalarGridSpec(
            num_scalar_prefetch=3, grid=(d // tn, rows // tr),
            in_specs=[pl.BlockSpec((tr, fe), lambda n, i, te, tv, ti: (ti[i], 0)),
                      pl.BlockSpec((None, None, fe, tn), lambda n, i, te, tv, ti: (layer, te[i], 0, n))],
            out_specs=pl.BlockSpec((tr, tn), lambda n, i, te, tv, ti: (i, n))),
        compiler_params=_cparams(("arbitrary", "arbitrary")), name="moe_down",
    )(plan["te"], plan["tv"], plan["ti"], a, w2)


def _moe_combine_ln_kernel(ct, cc, cf, cl, cv, ys_ref, p0_ref, p1_ref, y_ref, gate_ref, lng_ref, lnb_ref,
                           *rest, alpha, with_h, split_chunks):
    yos_ref = None
    if with_h:
        sh_ref, sc_ref, yo_ref, h_ref, acc_ref = rest
    elif split_chunks:
        yo_ref, yos_ref, acc_ref = rest
    else:
        yo_ref, acc_ref = rest
    p = pl.program_id(0)
    tr, mc = ys_ref.shape[0], y_ref.shape[0]

    @pl.when(cf[p] == 1)
    def _():
        acc_ref[...] = jnp.zeros_like(acc_ref)

    @pl.when(cv[p] == 1)
    def _():
        cols = ct[p] * tr + lax.broadcasted_iota(jnp.int32, (mc, tr), 1)
        sel = jnp.where((p0_ref[...] == cols) | (p1_ref[...] == cols), 1.0, 0.0).astype(BF16)
        acc_ref[...] += _dot(sel, ys_ref[...])

    @pl.when(cl[p] == 1)
    def _():
        z = alpha * y_ref[...] + gate_ref[...] * acc_ref[...]
        y = _ln_epilogue(z, lng_ref, lnb_ref)
        if yos_ref is None:
            yo_ref[...] = y
        else:
            @pl.when(cc[p] < split_chunks)
            def _():
                yo_ref[...] = y

            @pl.when(cc[p] >= split_chunks)
            def _():
                yos_ref[...] = y
        if with_h:
            h_ref[...] = (y * (1.0 + sc_ref[...]) + sh_ref[...]).astype(BF16)


def _moe_combine_ln(tok, ys, plan, y, mods, layer, gate_which, ln_g, ln_b, alpha, mc, h_mod=None,
                    split_rows=0):
    m, d = y.shape
    tr = MOE_SEL_ROWS
    with_h = h_mod is not None
    col = lambda a: a.reshape(m, 1)

    def mod(l, which):
        return pl.BlockSpec((None, None, 1, d),
                            lambda p, ct, cc, cf, cl, cv: (l, tok.rid(cc[p], mc), 0, which))

    chunk = lambda w: pl.BlockSpec((mc, w), lambda p, ct, cc, cf, cl, cv: (cc[p], 0))
    ins = [ys, col(plan["pos0"]), col(plan["pos1"]), y, mods, ln_g, ln_b]
    specs = [pl.BlockSpec((tr, d), lambda p, ct, cc, cf, cl, cv: (ct[p], 0)), chunk(1), chunk(1), chunk(d),
             mod(layer, gate_which), _layer_spec(ln_g, layer), _layer_spec(ln_b, layer)]
    outs = [jax.ShapeDtypeStruct((m, d), F32)]
    ospecs = [chunk(d)]
    sc = split_rows // mc
    if sc:
        assert not with_h
        outs = [jax.ShapeDtypeStruct((split_rows, d), F32), jax.ShapeDtypeStruct((m - split_rows, d), F32)]
        ospecs = [pl.BlockSpec((mc, d), lambda p, ct, cc, cf, cl, cv: (jnp.minimum(cc[p], sc - 1), 0)),
                  pl.BlockSpec((mc, d), lambda p, ct, cc, cf, cl, cv: (jnp.maximum(cc[p] - sc, 0), 0))]
    if with_h:
        hl, hsh, hsc = h_mod
        ins += [mods, mods]
        specs += [mod(hl, hsh), mod(hl, hsc)]
        outs.append(jax.ShapeDtypeStruct((m, d), BF16))
        ospecs.append(chunk(d))
    return pl.pallas_call(
        functools.partial(_moe_combine_ln_kernel, alpha=alpha, with_h=with_h, split_chunks=sc),
        out_shape=outs,
        grid_spec=pltpu.PrefetchScalarGridSpec(
            num_scalar_prefetch=5, grid=(plan["maxp"],), in_specs=specs, out_specs=ospecs,
            scratch_shapes=[pltpu.VMEM((mc, d), F32)]),
        compiler_params=_cparams(("arbitrary",)), name="moe_combine_ln",
    )(*plan["combine"], *ins)


def _rope_tables(n_tokens, dim, pad_to):
    t = jnp.arange(n_tokens)
    row = (t // GRID_W).astype(F32)
    col = (t % GRID_W).astype(F32)
    half = dim // 2
    inv_freq = ROPE_THETA ** (-jnp.arange(0, half, 2, dtype=F32) / half)
    ar = row[:, None] * inv_freq[None, :]
    ac = col[:, None] * inv_freq[None, :]
    ang = jnp.concatenate([ar, ar, ac, ac], axis=-1)
    cos, sin = jnp.cos(ang), jnp.sin(ang)
    lo = (np.arange(dim) % (dim // 2)) < dim // 4
    sa = jnp.where(lo[None, :], -sin, 0.0)
    sb = jnp.where(lo[None, :], 0.0, sin)
    return cos, sa, sb


def _pad_lanes(x, width, fill):
    return jnp.concatenate([x, jnp.full((x.shape[0], width - x.shape[1]), fill, x.dtype)], axis=1)


def kernel(x_prompt, x_sample, cache_da_k, cache_da_v, cache_mla_ckv, cache_mla_krope, cache_na_k, cache_na_v, c, c_ctx, w_ada, b_ada, w_in, da_lq1, da_lk1, da_lq2, da_lk2, da_subln, mla_gq, mla_gkv, mla_wuq, mla_wukv, na_rpb, w_out, ln1_g, ln1_b, ln2_g, ln2_b, ffn_w1, ffn_w3, ffn_w2, moe_router, moe_w1, moe_w3, moe_w2):
    nbp, seq, d = x_prompt.shape
    nbs, s_lat, _ = x_sample.shape
    depth = w_in.shape[0]
    past = cache_da_k.shape[2]
    mp, ms = nbp * seq, nbs * s_lat
    m = mp + ms
    tok = _Tok(mp, s_lat, nbs)
    tm = 512
    assert mp % s_lat == 0 and s_lat % tm == 0 and mp % tm == 0 and seq % LANE == 0 and nbs + 1 <= COND_ROWS
    alpha = (2.0 * depth) ** 0.25
    rows_n = s_lat // GRID_W

    cond = jnp.concatenate([c_ctx[None], c, jnp.zeros((COND_ROWS - 1 - nbs, d), F32)], axis=0)
    mods = _ada(cond, w_ada, b_ada).reshape(depth, COND_ROWS, 1, 6 * d)

    cos, sa, sb = _rope_tables(s_lat, DA_QK, LANE)
    rope_d = tuple(jnp.tile(t, (1, 2)) for t in (cos, sa, sb))
    rope_m = (_pad_lanes(cos, LANE, 1.0), _pad_lanes(sa, LANE, 0.0), _pad_lanes(sb, LANE, 0.0))

    y = (x_prompt.reshape(mp, d), x_sample.reshape(ms, d))
    h = _modulate(tok, y[0], y[1], mods, 0, tm)

    split = C_CKV + MLA_KV_RANK + MLA_ROPE
    w_in_b = w_in.astype(BF16)
    w_in_p = jnp.concatenate([w_in_b[:, :, :split], jnp.zeros((depth, d, C_NAQ - split), BF16),
                              w_in_b[:, :, split:]], axis=2)
    wuq = mla_wuq.reshape(depth, MLA_Q_RANK, MLA_HEADS, MLA_NOPE + MLA_ROPE)
    wuq_p = jnp.concatenate(
        [wuq, jnp.zeros((depth, MLA_Q_RANK, MLA_HEADS, MLA_QK_PAD - MLA_NOPE - MLA_ROPE), F32)],
        axis=3).reshape(depth, MLA_Q_RANK, MLA_HEADS * MLA_QK_PAD).astype(BF16)
    wukv = mla_wukv.reshape(depth, MLA_KV_RANK, MLA_HEADS, MLA_NOPE + MLA_V)
    wk = wukv[..., :MLA_NOPE].reshape(depth, MLA_KV_RANK, MLA_HEADS * MLA_NOPE).astype(BF16)
    wv = wukv[..., MLA_NOPE:].reshape(depth, MLA_KV_RANK, MLA_HEADS * MLA_V).astype(BF16)
    w_out_b = w_out.astype(BF16)
    ffn_w2_b = ffn_w2.astype(BF16)
    n_moe = moe_router.shape[0]
    wr = jnp.concatenate([moe_router, jnp.zeros((n_moe, d, LANE - N_EXPERTS), F32)], axis=2).astype(BF16)
    vec = lambda a: a.reshape(a.shape[0], 1, a.shape[1])
    gq, gkv, gsub = vec(mla_gq), vec(mla_gkv), vec(da_subln)
    lams = (vec(da_lq1), vec(da_lk1), vec(da_lq2), vec(da_lk2))
    ln1g, ln1b, ln2g, ln2b = vec(ln1_g), vec(ln1_b), vec(ln2_g), vec(ln2_b)
    cda_k = cache_da_k.reshape(nbs, depth, past, DA_HEADS * LANE)
    c_ckv = cache_mla_ckv.reshape(nbs * depth * past, MLA_KV_RANK)
    c_kr = jnp.concatenate([cache_mla_krope, jnp.zeros((nbs, depth, past, LANE - MLA_ROPE), F32)],
                           axis=-1).reshape(nbs * depth * past, LANE)
    bias, na_pat = _na_bias_tables(na_rpb, rows_n)
    stacks = None
    tm_big = 1024 if (mp % 1024 == 0 and s_lat % 1024 == 0) else tm

    st = [[] for _ in range(6)]
    for l in range(depth):
        lam_init = 0.8 - 0.6 * math.exp(-0.3 * l)
        proj = _mm(h, w_in_p, l, tm_big, 1024)

        o_p, *stacks = _ctx_attention(proj, l, depth, nbp, seq, gkv, wk, wv, gq, wuq_p, lams, gsub, lam_init,
                                      stacks)

        kcat_s, vb_s, qcat_s, qa_s, ka_s = _lat_prep(proj, l, mp, ms, s_lat, tm, gkv, wk, wv, gq, wuq_p,
                                                     rope_m, rope_d)
        kcat_c, vb_c = _mla_cache(c_ckv, c_kr, l, depth, nbs, past, wk, wv)
        tq = 512 if s_lat % 512 == 0 else s_lat
        oa_s = _da(qa_s, ka_s, proj, lams, gsub, l, lam_init, nb=nbs, sq=s_lat, sk=s_lat, tq=tq,
                   vrow0=mp, vcol0=C_DAV, kc=cda_k, vc=cache_da_v, skc=past)
        ob_s = _attn(qcat_s, kcat_s, vb_s, nb=nbs, sq=s_lat, sk=s_lat, tq=tq, heads=MLA_HEADS,
                     hp=MLA_HEADS_PER_STEP, dk=MLA_QK_PAD, dv=MLA_V, scale=MLA_SCALE, kc=kcat_c, vc=vb_c,
                     skc=past)
        oc_s = _na_latent(proj, l, mp, nbs, s_lat, cache_na_k, cache_na_v, bias, na_pat)

        i = l // 2
        moe = (l % 2 == 1)
        res = _mm_ln(tok, [o_p], w_out_b, l, y, mods, l, 2, ln1g, ln1b, alpha, tm, d, h_mod=(l, 3, 4),
                     w_router=wr if moe else None, router_layer=i, xs_tail=[oa_s, ob_s, oc_s],
                     name="out_proj_ln")
        y, h2 = res[0], res[1]

        nxt = (l + 1, 0, 1) if l + 1 < depth else None
        if not moe:
            a = _ffn_up(h2, ffn_w1, ffn_w3, i, tm_big, 512)
            res = _mm_ln(tok, [a], ffn_w2_b, i, y, mods, l, 5, ln2g, ln2b, alpha, tm, FFN_DOWN_TK,
                         h_mod=nxt, name="ffn_down_ln")
        else:
            plan = _moe_plan(res[2], res[3], m, tm)
            xs, grow = _moe_gather(h2, plan, tm)
            a = _moe_up(xs, grow, moe_w1, moe_w3, i, plan, 256)
            ys = _moe_down(a, moe_w2, i, plan, 512)
            res = _moe_combine_ln(tok, ys, plan, y, mods, l, 5, ln2g, ln2b, alpha, tm, h_mod=nxt,
                                  split_rows=0 if nxt is not None else mp)
        y = res[0] if nxt is not None else tuple(res)
        if nxt is not None:
            h = res[1]

        pp = proj[:mp]
        st[0].append(pp[:, C_DAK:C_DAK + 512].reshape(nbp, seq, DA_HEADS, 2, DA_QK))
        st[3].append(pp[:, C_CKV + MLA_KV_RANK:C_CKV + MLA_KV_RANK + MLA_ROPE].reshape(nbp, seq, MLA_ROPE))

    if not isinstance(y, tuple):
        y = (y[:mp], y[mp:])
    new_ckv, new_da_v, new_na_k, new_na_v = stacks
    return (y[0].reshape(nbp, seq, d), y[1].reshape(nbs, s_lat, d), jnp.stack(st[0], axis=1), new_da_v,
            new_ckv, jnp.stack(st[3], axis=1), new_na_k, new_na_v)
```

```python
import functools
import math

import numpy as np
import jax
import jax.numpy as jnp
from jax import lax
from jax.experimental import pallas as pl
from jax.experimental.pallas import tpu as pltpu

F32 = jnp.float32
BF16 = jnp.bfloat16

GRID_W = 64
DA_QK = 64
DA_V = 128
DA_HEADS = 4
MLA_Q_RANK = 512
MLA_KV_RANK = 256
MLA_NOPE = 128
MLA_ROPE = 64
MLA_V = 128
MLA_HEADS = 8
MLA_SCALE = (MLA_NOPE + MLA_ROPE) ** -0.5
NA_DIM = 128
NA_HEADS = 4
NA_KH = 8
NA_KW = 16
N_EXPERTS = 8
ROPE_THETA = 10000.0
LN_EPS = 1e-5
RMS_EPS = 1e-6
NEG_INF = -1e30
LOG2E = 1.4426950408889634
LANE = 128
COND_ROWS = 8
VMEM_LIMIT = 56 * 1024 * 1024

C_DAQ, C_DAK, C_DAV, C_CQ, C_CKV, C_NAQ, C_NAK, C_NAV = 0, 512, 1024, 1536, 2048, 2560, 3072, 3584
P_COLS = 4096
MLA_QK_PAD = 256
FFN_DOWN_TK = 1408
IN_PROJ_TN = 2048
FFN_UP_TF = 1024
NA_ROWS_PER_STEP = 4
MLA_HEADS_PER_STEP = 4


def _cparams(sem):
    return pltpu.CompilerParams(dimension_semantics=sem, vmem_limit_bytes=VMEM_LIMIT)


def _dot(a, b):
    return jnp.dot(a, b, preferred_element_type=F32)


def _dot_nt(a, b):
    return lax.dot_general(a, b, (((1,), (1,)), ((), ())), preferred_element_type=F32)


def _silu(x):
    return x * (1.0 / (1.0 + jnp.exp(-x)))


def _rms(x, g):
    return x * lax.rsqrt(jnp.mean(x * x, axis=-1, keepdims=True) + RMS_EPS) * g


def _rope(x, cos, sa, sb):
    return x * cos + pltpu.roll(x, LANE - 16, 1) * sa + pltpu.roll(x, 16, 1) * sb


def _ada_kernel(c_ref, w_ref, b_ref, o_ref):
    s = _silu(c_ref[...]).astype(BF16)
    o_ref[...] = _dot(s, w_ref[...].astype(BF16)) + b_ref[...]


def _ada(cond, w_ada, b_ada):
    n_layers, d, n = w_ada.shape
    tn = 1024
    return pl.pallas_call(
        _ada_kernel,
        out_shape=jax.ShapeDtypeStruct((n_layers, COND_ROWS, n), F32),
        grid=(n_layers, n // tn),
        in_specs=[pl.BlockSpec((COND_ROWS, d), lambda l, j: (0, 0)),
                  pl.BlockSpec((None, d, tn), lambda l, j: (l, 0, j)),
                  pl.BlockSpec((None, 1, tn), lambda l, j: (l, 0, j))],
        out_specs=pl.BlockSpec((None, COND_ROWS, tn), lambda l, j: (l, 0, j)),
        compiler_params=_cparams(("arbitrary", "arbitrary")), name="ada",
    )(cond, w_ada, b_ada.reshape(n_layers, 1, n))


class _Tok:
    def __init__(self, mp, s_lat, n_lat_batches):
        self.mp, self.s_lat, self.nb = mp, s_lat, n_lat_batches
        self.m = mp + s_lat * n_lat_batches

    def rid(self, i, tm):
        r0 = i * tm
        return jnp.where(r0 < self.mp, 0, 1 + (r0 - self.mp) // self.s_lat)

    def mod_spec(self, layer, which, tm, d, grid_rank=1):
        if grid_rank == 1:
            return pl.BlockSpec((None, None, 1, d), lambda i: (layer, self.rid(i, tm), 0, which))
        return pl.BlockSpec((None, None, 1, d), lambda i, k: (layer, self.rid(i, tm), 0, which))


def _head_tail_specs(head, tail, tm, width, rank):
    ht = head.shape[0] // tm
    if rank == 1:
        return ht, [pl.BlockSpec((tm, width), lambda i: (jnp.minimum(i, ht - 1), 0)),
                    pl.BlockSpec((tm, width), lambda i: (jnp.maximum(i - ht, 0), 0))]
    return ht, [pl.BlockSpec((tm, width), lambda i, k: (jnp.minimum(i, ht - 1), 0)),
                pl.BlockSpec((tm, width), lambda i, k: (jnp.maximum(i - ht, 0), 0))]


def _modulate_kernel(xp_ref, xs_ref, sh_ref, sc_ref, o_ref, *, head_tiles):
    x = jnp.where(pl.program_id(0) < head_tiles, xp_ref[...], xs_ref[...])
    o_ref[...] = (x * (1.0 + sc_ref[...]) + sh_ref[...]).astype(BF16)


def _modulate(tok, x_p, x_s, mods, layer, tm):
    d = x_p.shape[1]
    ht, xspecs = _head_tail_specs(x_p, x_s, tm, d, 1)
    return pl.pallas_call(
        functools.partial(_modulate_kernel, head_tiles=ht),
        out_shape=jax.ShapeDtypeStruct((tok.m, d), BF16),
        grid=(tok.m // tm,),
        in_specs=xspecs + [tok.mod_spec(layer, 0, tm, d), tok.mod_spec(layer, 1, tm, d)],
        out_specs=pl.BlockSpec((tm, d), lambda i: (i, 0)),
        compiler_params=_cparams(("arbitrary",)), name="modulate",
    )(x_p, x_s, mods, mods)


def _mm_kernel(x_ref, w_ref, o_ref):
    o_ref[...] = _dot(x_ref[...], w_ref[...])


def _layer_spec(arr, layer):
    zeros = (0,) * (arr.ndim - 1)
    return pl.BlockSpec((None,) + arr.shape[1:], lambda *_: (layer,) + zeros)


def _mm(x, w, layer, tm, tn):
    m, k = x.shape
    n = w.shape[2]
    return pl.pallas_call(
        _mm_kernel,
        out_shape=jax.ShapeDtypeStruct((m, n), F32),
        grid=(n // tn, m // tm),
        in_specs=[pl.BlockSpec((tm, k), lambda j, i: (i, 0)),
                  pl.BlockSpec((None, k, tn), lambda j, i: (layer, 0, j))],
        out_specs=pl.BlockSpec((tm, tn), lambda j, i: (i, j)),
        compiler_params=_cparams(("arbitrary", "arbitrary")), name="in_proj",
    )(x, w)


def _lat_prep_kernel(ckv_ref, cq_ref, daq_ref, dak_ref, gkv_ref, wk_ref, wv_ref, gq_ref, wuq_ref,
                     cm_ref, sam_ref, sbm_ref, cd_ref, sad_ref, sbd_ref,
                     kcat_ref, vb_ref, qcat_ref, qa_ref, ka_ref):
    cb = _rms(ckv_ref[:, :MLA_KV_RANK], gkv_ref[...]).astype(BF16)
    kr = ckv_ref[:, MLA_KV_RANK:MLA_KV_RANK + LANE]
    kr = _rope(kr, cm_ref[...], sam_ref[...], sbm_ref[...]).astype(BF16)
    kn = _dot(cb, wk_ref[...]).astype(BF16)
    vb_ref[...] = _dot(cb, wv_ref[...]).astype(BF16)
    q = _dot(_rms(cq_ref[...], gq_ref[...]).astype(BF16), wuq_ref[...])
    for h in range(MLA_HEADS):
        lo = h * MLA_QK_PAD
        kcat_ref[:, lo:lo + MLA_NOPE] = kn[:, h * MLA_NOPE:(h + 1) * MLA_NOPE]
        kcat_ref[:, lo + MLA_NOPE:lo + MLA_QK_PAD] = kr
        qcat_ref[:, lo:lo + MLA_NOPE] = q[:, lo:lo + MLA_NOPE].astype(BF16)
        qr = _rope(q[:, lo + MLA_NOPE:lo + MLA_QK_PAD], cm_ref[...], sam_ref[...], sbm_ref[...])
        qcat_ref[:, lo + MLA_NOPE:lo + MLA_QK_PAD] = qr.astype(BF16)
    for h in range(DA_HEADS):
        sl = slice(h * LANE, (h + 1) * LANE)
        qa_ref[:, sl] = _rope(daq_ref[:, sl], cd_ref[...], sad_ref[...], sbd_ref[...]).astype(BF16)
        ka_ref[:, sl] = _rope(dak_ref[:, sl], cd_ref[...], sad_ref[...], sbd_ref[...]).astype(BF16)


def _lat_prep(proj, layer, row0, rows, s_lat, tm, gkv, wk, wv, gq, wuq, rope_m, rope_d):
    b0 = row0 // tm
    nt = s_lat // tm
    blk = lambda c: pl.BlockSpec((tm, 512), lambda i: (b0 + i, c // 512))
    table = pl.BlockSpec((tm, LANE), lambda i: (i % nt, 0))
    params = [gkv, wk, wv, gq, wuq]
    widths = [MLA_HEADS * MLA_QK_PAD, MLA_HEADS * MLA_V, MLA_HEADS * MLA_QK_PAD, DA_HEADS * LANE,
              DA_HEADS * LANE]
    return pl.pallas_call(
        _lat_prep_kernel,
        out_shape=[jax.ShapeDtypeStruct((rows, w), BF16) for w in widths],
        grid=(rows // tm,),
        in_specs=[blk(C_CKV), blk(C_CQ), blk(C_DAQ), blk(C_DAK)] + [_layer_spec(a, layer) for a in params]
        + [table] * 6,
        out_specs=[pl.BlockSpec((tm, w), lambda i: (i, 0)) for w in widths],
        compiler_params=_cparams(("arbitrary",)), name="lat_prep",
    )(proj, proj, proj, proj, *params, *rope_m, *rope_d)


def _mla_cache_kernel(ckv_ref, kr_ref, wk_ref, wv_ref, kcat_ref, vb_ref):
    cb = ckv_ref[...].astype(BF16)
    kn = _dot(cb, wk_ref[...]).astype(BF16)
    vb_ref[...] = _dot(cb, wv_ref[...]).astype(BF16)
    kr = kr_ref[...].astype(BF16)
    for h in range(MLA_HEADS):
        kcat_ref[:, h * MLA_QK_PAD:h * MLA_QK_PAD + MLA_NOPE] = kn[:, h * MLA_NOPE:(h + 1) * MLA_NOPE]
        kcat_ref[:, h * MLA_QK_PAD + MLA_NOPE:(h + 1) * MLA_QK_PAD] = kr


def _mla_cache(ckv, kr, layer, depth, nb, past, wk, wv):
    rows = lambda w: pl.BlockSpec((past, w), lambda b: (b * depth + layer, 0))
    out = lambda w: pl.BlockSpec((past, w), lambda b: (b, 0))
    return pl.pallas_call(
        _mla_cache_kernel,
        out_shape=[jax.ShapeDtypeStruct((nb * past, MLA_HEADS * MLA_QK_PAD), BF16),
                   jax.ShapeDtypeStruct((nb * past, MLA_HEADS * MLA_V), BF16)],
        grid=(nb,),
        in_specs=[rows(MLA_KV_RANK), rows(LANE), _layer_spec(wk, layer), _layer_spec(wv, layer)],
        out_specs=[out(MLA_HEADS * MLA_QK_PAD), out(MLA_HEADS * MLA_V)],
        compiler_params=_cparams(("arbitrary",)), name="mla_cache",
    )(ckv, kr, wk, wv)


def _softmax_parts(s, s2, scale):
    c = scale * LOG2E
    m = jnp.max(s, axis=-1, keepdims=True)
    if s2 is not None:
        m = jnp.maximum(m, jnp.max(s2, axis=-1, keepdims=True))
    e = jnp.exp2((s - m) * c)
    den = jnp.sum(e, axis=-1, keepdims=True)
    e2 = None
    if s2 is not None:
        e2 = jnp.exp2((s2 - m) * c)
        den = den + jnp.sum(e2, axis=-1, keepdims=True)
    return e, e2, 1.0 / den


def _attn_head(q, k, v, scale, kc=None, vc=None):
    s = _dot_nt(q, k)
    s2 = _dot_nt(q, kc) if kc is not None else None
    e, e2, inv = _softmax_parts(s, s2, scale)
    o = _dot(e.astype(BF16), v)
    if kc is not None:
        o = o + _dot(e2.astype(BF16), vc)
    return o * inv


def _attn_kernel(*refs, heads, dk, dv, scale, has_ctx):
    if has_ctx:
        q_ref, k_ref, v_ref, kc_ref, vc_ref, o_ref = refs
    else:
        q_ref, k_ref, v_ref, o_ref = refs
    for h in range(heads):
        ks, vs = slice(h * dk, (h + 1) * dk), slice(h * dv, (h + 1) * dv)
        o = _attn_head(q_ref[:, ks].astype(BF16), k_ref[:, ks].astype(BF16), v_ref[:, vs].astype(BF16), scale,
                       kc_ref[:, ks].astype(BF16) if has_ctx else None,
                       vc_ref[:, vs].astype(BF16) if has_ctx else None)
        o_ref[:, vs] = o.astype(BF16)


def _attn(q, k, v, *, nb, sq, sk, tq, heads, hp, dk, dv, scale, qrow0=0, krow0=0, qcol0=0, kcol0=0, vcol0=0,
          kc=None, vc=None, skc=0):
    nq = sq // tq
    ng = heads // hp
    wq, wv = hp * dk, hp * dv
    has_ctx = kc is not None
    ins = [q, k, v]
    specs = [pl.BlockSpec((tq, wq), lambda b, g, i: (qrow0 // tq + b * nq + i, qcol0 // wq + g)),
             pl.BlockSpec((sk, wq), lambda b, g, i: (krow0 // sk + b, kcol0 // wq + g)),
             pl.BlockSpec((sk, wv), lambda b, g, i: (krow0 // sk + b, vcol0 // wv + g))]
    if has_ctx:
        ins += [kc, vc]
        specs += [pl.BlockSpec((skc, wq), lambda b, g, i: (b, g)),
                  pl.BlockSpec((skc, wv), lambda b, g, i: (b, g))]
    return pl.pallas_call(
        functools.partial(_attn_kernel, heads=hp, dk=dk, dv=dv, scale=scale, has_ctx=has_ctx),
        out_shape=jax.ShapeDtypeStruct((nb * sq, heads * dv), BF16),
        grid=(nb, ng, nq), in_specs=specs,
        out_specs=pl.BlockSpec((tq, wv), lambda b, g, i: (b * nq + i, g)),
        compiler_params=_cparams(("arbitrary", "arbitrary", "arbitrary")), name="attn_h%d" % heads,
    )(*ins)


def _da_lambda(lq1, lk1, lq2, lk2, lam_init):
    return (jnp.exp(jnp.sum(lq1[...] * lk1[...], axis=-1, keepdims=True))
            - jnp.exp(jnp.sum(lq2[...] * lk2[...], axis=-1, keepdims=True)) + lam_init)


def _da_head(q, k, v, lam, g, lam_init, kc=None, vc=None):
    first = lax.broadcasted_iota(jnp.int32, (1, LANE), 1) < DA_QK
    q = q.astype(F32) * (DA_QK ** -0.5)
    qs = (jnp.where(first, q, 0.0).astype(BF16), jnp.where(first, 0.0, q).astype(BF16))
    o = _attn_head(qs[0], k, v, 1.0, kc, vc) - lam * _attn_head(qs[1], k, v, 1.0, kc, vc)
    return _rms(o, g) * (1.0 - lam_init)


def _da_kernel(*refs, has_ctx, lam_init):
    if has_ctx:
        q_ref, k_ref, v_ref, kc_ref, vc_ref, lq1, lk1, lq2, lk2, g_ref, o_ref = refs
    else:
        q_ref, k_ref, v_ref, lq1, lk1, lq2, lk2, g_ref, o_ref = refs
    lam = _da_lambda(lq1, lk1, lq2, lk2, lam_init)
    for h in range(DA_HEADS):
        sl = slice(h * LANE, (h + 1) * LANE)
        o = _da_head(q_ref[:, sl], k_ref[:, sl].astype(BF16), v_ref[:, sl].astype(BF16), lam, g_ref[...],
                     lam_init, kc_ref[:, sl].astype(BF16) if has_ctx else None,
                     vc_ref[:, h, :].astype(BF16) if has_ctx else None)
        o_ref[:, sl] = o.astype(BF16)


def _ctx_kernel(p_ref, gkv_ref, wk_ref, wv_ref, gq_ref, wuq_ref, lq1, lk1, lq2, lk2, g_ref, *rest, lam_init):
    o_ref, ckvn_ref, dav_ref, nak_ref, nav_ref = rest[-5:]
    for h in range(DA_HEADS):
        dav_ref[:, h, :] = p_ref[:, C_DAV + h * DA_V:C_DAV + (h + 1) * DA_V]
    for h in range(NA_HEADS):
        nak_ref[:, h, :] = p_ref[:, C_NAK + h * NA_DIM:C_NAK + (h + 1) * NA_DIM]
        nav_ref[:, h, :] = p_ref[:, C_NAV + h * NA_DIM:C_NAV + (h + 1) * NA_DIM]
    col = lambda c0, h, w=LANE: slice(c0 + h * w, c0 + (h + 1) * w)
    lam = _da_lambda(lq1, lk1, lq2, lk2, lam_init)
    for h in range(DA_HEADS):
        o = _da_head(p_ref[:, col(C_DAQ, h)], p_ref[:, col(C_DAK, h)].astype(BF16),
                     p_ref[:, col(C_DAV, h)].astype(BF16), lam, g_ref[...], lam_init)
        o_ref[:, col(0, h)] = o.astype(BF16)
    ckvn = _rms(p_ref[:, C_CKV:C_CKV + MLA_KV_RANK], gkv_ref[...])
    ckvn_ref[...] = ckvn
    cb = ckvn.astype(BF16)
    kn = _dot(cb, wk_ref[...]).astype(BF16)
    vb = _dot(cb, wv_ref[...]).astype(BF16)
    kr = p_ref[:, C_CKV + MLA_KV_RANK:C_CKV + MLA_KV_RANK + LANE].astype(BF16)
    q = _dot(_rms(p_ref[:, C_CQ:C_CQ + MLA_Q_RANK], gq_ref[...]).astype(BF16), wuq_ref[...]).astype(BF16)
    ob0 = DA_HEADS * DA_V
    for h in range(MLA_HEADS):
        kh = jnp.concatenate([kn[:, col(0, h)], kr], axis=1)
        o = _attn_head(q[:, col(0, h, MLA_QK_PAD)], kh, vb[:, col(0, h)], MLA_SCALE)
        o_ref[:, col(ob0, h)] = o.astype(BF16)
    oc0 = ob0 + MLA_HEADS * MLA_V
    for h in range(NA_HEADS):
        o = _attn_head(p_ref[:, col(C_NAQ, h)].astype(BF16), p_ref[:, col(C_NAK, h)].astype(BF16),
                       p_ref[:, col(C_NAV, h)].astype(BF16), NA_DIM ** -0.5)
        o_ref[:, col(oc0, h)] = o.astype(BF16)


def _ctx_attention(proj, layer, depth, nb, seq, gkv, wk, wv, gq, wuq, lams, g, lam_init, stacks):
    d_out = DA_HEADS * DA_V + MLA_HEADS * MLA_V + NA_HEADS * NA_DIM
    params = [gkv, wk, wv, gq, wuq] + list(lams) + [g]
    stack_shapes = [(nb, depth, seq, MLA_KV_RANK), (nb, depth, seq, DA_HEADS, DA_V),
                    (nb, depth, seq, NA_HEADS, NA_DIM), (nb, depth, seq, NA_HEADS, NA_DIM)]
    stack_specs = [pl.BlockSpec((None, None) + s[2:], lambda b, n=len(s): (b, layer) + (0,) * (n - 2))
                   for s in stack_shapes]
    prev = list(stacks) if stacks is not None else []
    n_in = 1 + len(params)
    return pl.pallas_call(
        functools.partial(_ctx_kernel, lam_init=lam_init),
        out_shape=[jax.ShapeDtypeStruct((nb * seq, d_out), BF16)]
        + [jax.ShapeDtypeStruct(s, F32) for s in stack_shapes],
        grid=(nb,),
        in_specs=[pl.BlockSpec((seq, P_COLS), lambda b: (b, 0))] + [_layer_spec(a, layer) for a in params]
        + [pl.BlockSpec(memory_space=pl.ANY)] * len(prev),
        out_specs=[pl.BlockSpec((seq, d_out), lambda b: (b, 0))] + stack_specs,
        input_output_aliases={n_in + k: 1 + k for k in range(len(prev))},
        compiler_params=_cparams(("arbitrary",)), name="ctx_attention",
    )(proj, *params, *prev)


def _da(q, k, v, lams, g, layer, lam_init, *, nb, sq, sk, tq, qrow0=0, krow0=0, qcol0=0, kcol0=0, vcol0=0,
        vrow0=None, kc=None, vc=None, skc=0):
    nq = sq // tq
    w = DA_HEADS * LANE
    vrow0 = krow0 if vrow0 is None else vrow0
    has_ctx = kc is not None
    ins = [q, k, v]
    specs = [pl.BlockSpec((tq, w), lambda b, i: (qrow0 // tq + b * nq + i, qcol0 // w)),
             pl.BlockSpec((sk, w), lambda b, i: (krow0 // sk + b, kcol0 // w)),
             pl.BlockSpec((sk, w), lambda b, i: (vrow0 // sk + b, vcol0 // w))]
    if has_ctx:
        ins += [kc, vc]
        specs += [pl.BlockSpec((None, None, skc, w), lambda b, i: (b, layer, 0, 0)),
                  pl.BlockSpec((None, None, skc, DA_HEADS, DA_V), lambda b, i: (b, layer, 0, 0, 0))]
    ins += list(lams) + [g]
    specs += [_layer_spec(a, layer) for a in ins[-5:]]
    return pl.pallas_call(
        functools.partial(_da_kernel, has_ctx=has_ctx, lam_init=lam_init),
        out_shape=jax.ShapeDtypeStruct((nb * sq, w), BF16),
        grid=(nb, nq), in_specs=specs,
        out_specs=pl.BlockSpec((tq, w), lambda b, i: (b * nq + i, 0)),
        compiler_params=_cparams(("arbitrary", "arbitrary")), name="diff_attn",
    )(*ins)


def _na_geometry(rows_n):
    kh = min(NA_KH, rows_n)
    rb = NA_ROWS_PER_STEP if rows_n % NA_ROWS_PER_STEP == 0 else 1
    uw = min(rows_n, kh + rb - 1)
    return kh, rb, uw


def _na_kernel(pat, q_ref, k_ref, v_ref, kc_ref, vc_ref, bias_ref, o_ref, *, rows_n, kh, rb, uw):
    k0 = jnp.clip(pl.program_id(1) * rb - kh // 2, 0, rows_n - uw)
    start = pl.multiple_of(k0 * GRID_W, GRID_W)
    nwin = uw * GRID_W
    scale = NA_DIM ** -0.5
    for h in range(NA_HEADS):
        sl = slice(h * NA_DIM, (h + 1) * NA_DIM)
        q = q_ref[:, sl].astype(BF16)
        kw = k_ref[pl.ds(start, nwin), sl].astype(BF16)
        vw = v_ref[pl.ds(start, nwin), sl].astype(BF16)
        s = _dot_nt(q, kw) * scale + bias_ref[h]
        s2 = _dot_nt(q, kc_ref[:, h, :].astype(BF16)) * scale
        e, e2, inv = _softmax_parts(s, s2, 1.0)
        o = _dot(e.astype(BF16), vw) + _dot(e2.astype(BF16), vc_ref[:, h, :].astype(BF16))
        o_ref[:, sl] = (o * inv).astype(BF16)


def _na_latent(proj, layer, row0, nb, s_lat, kc, vc, bias, pattern_of_step):
    rows_n = s_lat // GRID_W
    kh, rb, uw = _na_geometry(rows_n)
    w = NA_HEADS * NA_DIM
    skc = kc.shape[2]
    tq = rb * GRID_W
    steps = rows_n // rb
    return pl.pallas_call(
        functools.partial(_na_kernel, rows_n=rows_n, kh=kh, rb=rb, uw=uw),
        out_shape=jax.ShapeDtypeStruct((nb * s_lat, w), BF16),
        grid_spec=pltpu.PrefetchScalarGridSpec(
            num_scalar_prefetch=1, grid=(nb, steps),
            in_specs=[pl.BlockSpec((tq, w), lambda b, r, pat: (row0 // tq + b * steps + r, C_NAQ // w)),
                      pl.BlockSpec((s_lat, w), lambda b, r, pat: (row0 // s_lat + b, C_NAK // w)),
                      pl.BlockSpec((s_lat, w), lambda b, r, pat: (row0 // s_lat + b, C_NAV // w)),
                      pl.BlockSpec((None, None, skc, NA_HEADS, NA_DIM), lambda b, r, pat: (b, layer, 0, 0, 0)),
                      pl.BlockSpec((None, None, skc, NA_HEADS, NA_DIM), lambda b, r, pat: (b, layer, 0, 0, 0)),
                      pl.BlockSpec((None, NA_HEADS, None, tq, uw * GRID_W),
                                   lambda b, r, pat: (layer, 0, pat[r], 0, 0))],
            out_specs=pl.BlockSpec((tq, w), lambda b, r, pat: (b * steps + r, 0))),
        compiler_params=_cparams(("arbitrary", "arbitrary")), name="na_latent",
    )(pattern_of_step, proj, proj, proj, kc, vc, bias)


def _na_bias_tables(rpb, rows_n):
    kh = min(NA_KH, rows_n)
    qc = np.arange(GRID_W)[:, None]
    kc = np.arange(GRID_W)[None, :]
    col_start = np.clip(qc - NA_KW // 2, 0, GRID_W - NA_KW)
    valid = (kc >= col_start) & (kc < col_start + NA_KW)
    coff = np.clip(kc - qc, -(NA_KW - 1), NA_KW - 1) + (NA_KW - 1)
    onehot = (coff.reshape(-1)[None, :] == np.arange(2 * NA_KW - 1)[:, None]).astype(np.float32)
    n_l = rpb.shape[0]
    t = jnp.einsum('lhrc,cx->lhrx', rpb.astype(F32), jnp.asarray(onehot), precision=lax.Precision.HIGHEST)
    t = t.reshape(n_l, NA_HEADS, 2 * NA_KH - 1, GRID_W, GRID_W)
    t = jnp.where(jnp.asarray(valid)[None, None, None], t, NEG_INF)
    strips = []
    for v in range(kh):
        lo = NA_KH - 1 - v
        s = t[:, :, lo:lo + kh]
        strips.append(jnp.transpose(s, (0, 1, 3, 2, 4)).reshape(n_l, NA_HEADS, GRID_W, kh * GRID_W))
    _, rb, uw = _na_geometry(rows_n)
    patterns, pattern_of_step = [], []
    for t0 in range(rows_n // rb):
        k0 = int(np.clip(t0 * rb - kh // 2, 0, rows_n - uw))
        key = []
        for i in range(rb):
            r = t0 * rb + i
            rs = int(np.clip(r - kh // 2, 0, rows_n - kh))
            key.append((rs - k0, r - rs))
        key = tuple(key)
        if key not in patterns:
            patterns.append(key)
        pattern_of_step.append(patterns.index(key))
    blocks = []
    for key in patterns:
        rows = [jnp.pad(strips[v], ((0, 0), (0, 0), (0, 0), (off * GRID_W, (uw - kh - off) * GRID_W)),
                        constant_values=NEG_INF) for off, v in key]
        blocks.append(jnp.concatenate(rows, axis=2))
    return jnp.stack(blocks, axis=2), jnp.asarray(pattern_of_step, jnp.int32)


def _ln_epilogue(z, g_ref, b_ref):
    mu = jnp.mean(z, axis=-1, keepdims=True)
    zc = z - mu
    var = jnp.mean(zc * zc, axis=-1, keepdims=True)
    return zc * lax.rsqrt(var + LN_EPS) * g_ref[...] + b_ref[...]


def _router_info(h, wr_ref, cnt_ref):
    n = h.shape[0]
    logits = _dot(h, wr_ref[...])
    lane = lax.broadcasted_iota(jnp.int32, logits.shape, 1)
    lg = jnp.where(lane < N_EXPERTS, logits, -jnp.inf)
    m1 = jnp.max(lg, axis=-1, keepdims=True)
    i1 = jnp.min(jnp.where(lg == m1, lane, LANE), axis=-1, keepdims=True)
    lg2 = jnp.where(lane == i1, -jnp.inf, lg)
    m2 = jnp.max(lg2, axis=-1, keepdims=True)
    i2 = jnp.min(jnp.where(lg2 == m2, lane, LANE), axis=-1, keepdims=True)
    e2 = jnp.exp(m2 - m1)
    inv = 1.0 / (1.0 + e2)
    oh1, oh2 = lane == i1, lane == i2
    o1, o2 = jnp.where(oh1, 1.0, 0.0), jnp.where(oh2, 1.0, 0.0)
    below = lax.broadcasted_iota(jnp.int32, (n, n), 1) < lax.broadcasted_iota(jnp.int32, (n, n), 0)
    tri = jnp.where(below, 1.0, 0.0).astype(BF16)
    p1 = _dot(tri, o1.astype(BF16))
    p2 = _dot(tri, o2.astype(BF16))
    tot1 = jnp.sum(o1, axis=0, keepdims=True)
    tot2 = jnp.sum(o2, axis=0, keepdims=True)
    cnt = cnt_ref[...]
    rank0 = jnp.sum(jnp.where(oh1, cnt + p1, 0.0), axis=-1, keepdims=True)
    rank1 = jnp.sum(jnp.where(oh2, cnt + tot1 + p2, 0.0), axis=-1, keepdims=True)
    cnt_ref[...] = cnt + tot1 + tot2
    cols = (i1.astype(F32), i2.astype(F32), inv, e2 * inv, rank0, rank1)
    info = jnp.zeros(logits.shape, F32)
    for c, v in enumerate(cols):
        info = jnp.where(lane == c, v, info)
    return info


def _mm_ln_kernel(*refs, n_x, n_tail, head_tiles, y_head_tiles, k_total, tk, alpha, with_h, with_router,
                  mask_k):
    it = iter(refs)
    x_refs = [next(it) for _ in range(n_x)]
    tail_refs = [next(it) for _ in range(n_tail)]
    w_ref, y_ref = next(it), next(it)
    ys_ref = next(it) if y_head_tiles else None
    gate_ref, lng_ref, lnb_ref = next(it), next(it), next(it)
    sh_ref = next(it) if with_h else None
    sc_ref = next(it) if with_h else None
    wr_ref = next(it) if with_router else None
    yo_ref = next(it)
    h_ref = next(it) if with_h else None
    go_ref = next(it) if with_router else None
    ca_ref = next(it) if with_router else None
    acc_ref = next(it)
    cnt_ref = next(it) if with_router else None
    k = pl.program_id(1)
    nk = pl.num_programs(1)
    if with_router:
        @pl.when((pl.program_id(0) == 0) & (k == 0))
        def _():
            cnt_ref[...] = jnp.zeros_like(cnt_ref)
    cat = lambda rs: rs[0][...] if len(rs) == 1 else jnp.concatenate([r[...] for r in rs], axis=1)
    x = cat(x_refs)
    if n_tail:
        x = jnp.where(pl.program_id(0) < head_tiles, x, cat(tail_refs))
    w = w_ref[...]
    if mask_k:
        lim = k_total - k * tk
        x = jnp.where(lax.broadcasted_iota(jnp.int32, x.shape, 1) < lim, x, jnp.zeros_like(x))
        w = jnp.where(lax.broadcasted_iota(jnp.int32, w.shape, 0) < lim, w, jnp.zeros_like(w))
    part = _dot(x, w.astype(BF16))

    @pl.when(k == 0)
    def _():
        acc_ref[...] = part

    @pl.when(k > 0)
    def _():
        acc_ref[...] += part

    @pl.when(k == nk - 1)
    def _():
        y_in = y_ref[...]
        if y_head_tiles:
            y_in = jnp.where(pl.program_id(0) < y_head_tiles, y_in, ys_ref[...])
        z = alpha * y_in + gate_ref[...] * acc_ref[...]
        y = _ln_epilogue(z, lng_ref, lnb_ref)
        yo_ref[...] = y
        if with_h:
            h = (y * (1.0 + sc_ref[...]) + sh_ref[...]).astype(BF16)
            h_ref[...] = h
            if with_router:
                go_ref[...] = _router_info(h, wr_ref, cnt_ref)
                ca_ref[...] = cnt_ref[...]


def _mm_ln(tok, xs, w, w_layer, y, mods, layer, gate_which, ln_g, ln_b, alpha, tm, tk, h_mod=None,
           w_router=None, router_layer=0, xs_tail=None, name="mm_ln"):
    m, d = tok.m, w.shape[2]
    k_total = w.shape[1]
    nk = pl.cdiv(k_total, tk)
    with_h = h_mod is not None
    with_router = w_router is not None
    ins, specs = [], []
    xs_tail = xs_tail or []
    head_tiles = xs[0].shape[0] // tm if xs_tail else 0
    for x in xs:
        wx = x.shape[1] if (len(xs) > 1 or xs_tail) else tk
        ins.append(x)
        if xs_tail:
            specs.append(pl.BlockSpec((tm, wx), lambda i, k: (jnp.minimum(i, head_tiles - 1), k)))
        else:
            specs.append(pl.BlockSpec((tm, wx), lambda i, k: (i, k)))
    for x in xs_tail:
        ins.append(x)
        specs.append(pl.BlockSpec((tm, x.shape[1]), lambda i, k: (jnp.maximum(i - head_tiles, 0), k)))
    ins.append(w)
    specs.append(pl.BlockSpec((None, tk, d), lambda i, k: (w_layer, k, 0)))
    y_head_tiles = 0
    if isinstance(y, tuple):
        y_head_tiles, yspecs = _head_tail_specs(y[0], y[1], tm, d, 2)
        ins += list(y)
        specs += yspecs
    else:
        ins.append(y)
        specs.append(pl.BlockSpec((tm, d), lambda i, k: (i, 0)))
    ins += [mods, ln_g, ln_b]
    specs += [tok.mod_spec(layer, gate_which, tm, d, 2),
              _layer_spec(ln_g, layer), _layer_spec(ln_b, layer)]
    outs = [jax.ShapeDtypeStruct((m, d), F32)]
    ospecs = [pl.BlockSpec((tm, d), lambda i, k: (i, 0))]
    if with_h:
        hl, hsh, hsc = h_mod
        ins += [mods, mods]
        specs += [tok.mod_spec(hl, hsh, tm, d, 2), tok.mod_spec(hl, hsc, tm, d, 2)]
        outs.append(jax.ShapeDtypeStruct((m, d), BF16))
        ospecs.append(pl.BlockSpec((tm, d), lambda i, k: (i, 0)))
    if with_router:
        ins.append(w_router)
        specs.append(_layer_spec(w_router, router_layer))
        outs += [jax.ShapeDtypeStruct((m, LANE), F32), jax.ShapeDtypeStruct((m // tm, 1, LANE), F32)]
        ospecs += [pl.BlockSpec((tm, LANE), lambda i, k: (i, 0)),
                   pl.BlockSpec((None, 1, LANE), lambda i, k: (i, 0, 0))]
    scratch = [pltpu.VMEM((tm, d), F32)]
    if with_router:
        scratch.append(pltpu.VMEM((1, LANE), F32))
    res = pl.pallas_call(
        functools.partial(_mm_ln_kernel, n_x=len(xs), n_tail=len(xs_tail), head_tiles=head_tiles,
                          y_head_tiles=y_head_tiles, k_total=k_total, tk=tk, alpha=alpha, with_h=with_h,
                          with_router=with_router, mask_k=(k_total % tk != 0)),
        out_shape=outs, grid=(m // tm, nk), in_specs=specs, out_specs=ospecs,
        scratch_shapes=scratch,
        compiler_params=_cparams(("arbitrary", "arbitrary")), name=name,
    )(*ins)
    return res


def _ffn_up_kernel(x_ref, w1_ref, w3_ref, o_ref):
    x = x_ref[...]
    a = _silu(_dot(x, w1_ref[...].astype(BF16))) * _dot(x, w3_ref[...].astype(BF16))
    o_ref[...] = a.astype(BF16)


def _ffn_up(x, w1, w3, layer, tm, tf):
    m, d = x.shape
    f = w1.shape[2]
    wspec = pl.BlockSpec((None, d, tf), lambda j, i: (layer, 0, j))
    return pl.pallas_call(
        _ffn_up_kernel,
        out_shape=jax.ShapeDtypeStruct((m, f), BF16),
        grid=(pl.cdiv(f, tf), m // tm),
        in_specs=[pl.BlockSpec((tm, d), lambda j, i: (i, 0)), wspec, wspec],
        out_specs=pl.BlockSpec((tm, tf), lambda j, i: (i, j)),
        compiler_params=_cparams(("arbitrary", "arbitrary")), name="ffn_up",
    )(x, w1, w3)


MOE_ROWS = 1024
MOE_SEL_ROWS = 256


def _moe_plan(info, c_after, m, mc):
    tr, ts = MOE_ROWS, MOE_SEL_ROWS
    e_n = N_EXPERTS
    n_chunks = m // mc
    n_tiles = (2 * m) // tr + e_n
    n_blocks = n_tiles * (tr // ts)
    maxp = n_blocks + e_n * n_chunks
    i32 = jnp.int32
    i1, i2 = info[:, 0].astype(i32), info[:, 1].astype(i32)
    r0, r1 = info[:, 4].astype(i32), info[:, 5].astype(i32)
    ca = c_after[:, 0, :e_n].astype(i32)
    cb = jnp.concatenate([jnp.zeros((1, e_n), i32), ca[:-1]], axis=0)
    counts = ca[-1]
    padded = ((counts + tr - 1) // tr) * tr
    start = jnp.cumsum(padded) - padded
    eid = jnp.arange(e_n, dtype=i32)

    def pick(idx, table):
        return jnp.sum(jnp.where(idx[:, None] == eid[None, :], table[None, :], 0), axis=1)

    pos0 = pick(i1, start) + r0
    pos1 = pick(i2, start) + r1
    def expert_of(row0):
        return jnp.minimum(jnp.sum((row0[:, None] >= (start + padded)[None, :]).astype(i32), axis=1), e_n - 1)

    trow0 = jnp.arange(n_tiles, dtype=i32) * tr
    te = expert_of(trow0)
    tv = trow0 < jnp.sum(padded)
    row0 = jnp.arange(n_blocks, dtype=i32) * ts
    be = expert_of(row0)
    k0 = row0 - pick(be, start)
    k1 = jnp.minimum(k0 + ts, pick(be, counts))
    sel = (be[:, None] == eid[None, :])
    cb_t = jnp.sum(jnp.where(sel[:, None, :], cb[None], 0), axis=2)
    ca_t = jnp.sum(jnp.where(sel[:, None, :], ca[None], 0), axis=2)
    ov = (row0 < jnp.sum(padded))[:, None] & (cb_t < k1[:, None]) & (ca_t > k0[:, None])
    first_chunk = (jnp.arange(n_chunks) == 0)[None, :]
    ov_g = ov | (~jnp.any(ov, axis=1, keepdims=True) & first_chunk)

    def pairs(mask2d, inner):
        flat = mask2d.reshape(-1)
        n = jnp.sum(flat.astype(i32))
        idx = jnp.nonzero(flat, size=maxp, fill_value=0)[0].astype(i32)
        p = jnp.arange(maxp, dtype=i32)
        valid = p < n
        idx = jnp.where(valid, idx, jnp.max(jnp.where(valid, idx, 0)))
        outer, inn = idx // inner, idx % inner
        prev = jnp.concatenate([jnp.full((1,), -1, i32), outer[:-1]])
        nxt = jnp.concatenate([outer[1:], jnp.full((1,), -1, i32)])
        first = valid & (outer != prev)
        last = valid & ((outer != nxt) | (p == n - 1))
        return outer, inn, first.astype(i32), last.astype(i32), valid.astype(i32)

    g_tile, g_chunk, g_first, _, g_valid = pairs(ov_g, n_chunks)
    c_chunk, c_tile, c_first, c_last, c_valid = pairs(ov.T, n_blocks)
    ti = jnp.minimum(jnp.arange(n_tiles, dtype=i32), jnp.sum(tv.astype(i32)) - 1)
    return dict(pos0=pos0, pos1=pos1, g1=info[:, 2], g2=info[:, 3], te=te, tv=tv.astype(i32), ti=ti,
                gather=(g_tile, g_chunk, g_first, g_valid),
                combine=(c_tile, c_chunk, c_first, c_last, c_valid), n_tiles=n_tiles, maxp=maxp)


def _moe_gather_kernel(pt, pc, pf, pv, h_ref, p0_ref, p1_ref, g0_ref, g1_ref, xs_ref, gr_ref):
    p = pl.program_id(0)
    tr, mc = xs_ref.shape[0], h_ref.shape[0]

    @pl.when(pf[p] == 1)
    def _():
        xs_ref[...] = jnp.zeros_like(xs_ref)
        gr_ref[...] = jnp.zeros_like(gr_ref)

    @pl.when(pv[p] == 1)
    def _():
        rows = pt[p] * tr + lax.broadcasted_iota(jnp.int32, (tr, mc), 0)
        m0 = p0_ref[...] == rows
        m1 = p1_ref[...] == rows
        sel = jnp.where(m0 | m1, 1.0, 0.0).astype(BF16)
        xs_ref[...] = (xs_ref[...].astype(F32) + _dot(sel, h_ref[...])).astype(BF16)
        gr_ref[...] += jnp.sum(jnp.where(m0, g0_ref[...], 0.0) + jnp.where(m1, g1_ref[...], 0.0),
                               axis=1, keepdims=True)


def _moe_gather(h, plan, mc):
    m, d = h.shape
    tr = MOE_SEL_ROWS
    rows = plan["n_tiles"] * MOE_ROWS
    row = lambda a: a.reshape(1, m)
    tok_spec = lambda: pl.BlockSpec((1, mc), lambda p, pt, pc, pf, pv: (0, pc[p]))
    return pl.pallas_call(
        _moe_gather_kernel,
        out_shape=[jax.ShapeDtypeStruct((rows, d), BF16), jax.ShapeDtypeStruct((rows, 1), F32)],
        grid_spec=pltpu.PrefetchScalarGridSpec(
            num_scalar_prefetch=4, grid=(plan["maxp"],),
            in_specs=[pl.BlockSpec((mc, d), lambda p, pt, pc, pf, pv: (pc[p], 0)),
                      tok_spec(), tok_spec(), tok_spec(), tok_spec()],
            out_specs=[pl.BlockSpec((tr, d), lambda p, pt, pc, pf, pv: (pt[p], 0)),
                       pl.BlockSpec((tr, 1), lambda p, pt, pc, pf, pv: (pt[p], 0))]),
        compiler_params=_cparams(("arbitrary",)), name="moe_gather",
    )(*plan["gather"], h, row(plan["pos0"]), row(plan["pos1"]), row(plan["g1"]), row(plan["g2"]))


def _moe_up_kernel(te, tv, ti, x_ref, w1_ref, w3_ref, g_ref, o_ref):
    i = pl.program_id(1)

    @pl.when(tv[i] == 1)
    def _():
        x = x_ref[...]
        a = _silu(_dot(x, w1_ref[...].astype(BF16))) * _dot(x, w3_ref[...].astype(BF16))
        o_ref[...] = (a * g_ref[...]).astype(BF16)

    @pl.when(tv[i] == 0)
    def _():
        o_ref[...] = jnp.zeros_like(o_ref)


def _moe_up(xs, grow, w1, w3, layer, plan, tf):
    rows, d = xs.shape
    tr = MOE_ROWS
    fe = w1.shape[3]
    wspec = pl.BlockSpec((None, None, d, tf), lambda j, i, te, tv, ti: (layer, te[i], 0, j))
    return pl.pallas_call(
        _moe_up_kernel,
        out_shape=jax.ShapeDtypeStruct((rows, fe), BF16),
        grid_spec=pltpu.PrefetchScalarGridSpec(
            num_scalar_prefetch=3, grid=(fe // tf, rows // tr),
            in_specs=[pl.BlockSpec((tr, d), lambda j, i, te, tv, ti: (ti[i], 0)), wspec, wspec,
                      pl.BlockSpec((tr, 1), lambda j, i, te, tv, ti: (ti[i], 0))],
            out_specs=pl.BlockSpec((tr, tf), lambda j, i, te, tv, ti: (i, j))),
        compiler_params=_cparams(("arbitrary", "arbitrary")), name="moe_up",
    )(plan["te"], plan["tv"], plan["ti"], xs, w1, w3, grow)


def _moe_down_kernel(te, tv, ti, a_ref, w_ref, o_ref):
    i = pl.program_id(1)

    @pl.when(tv[i] == 1)
    def _():
        o_ref[...] = _dot(a_ref[...], w_ref[...].astype(BF16)).astype(BF16)

    @pl.when(tv[i] == 0)
    def _():
        o_ref[...] = jnp.zeros_like(o_ref)


def _moe_down(a, w2, layer, plan, tn):
    rows, fe = a.shape
    tr = MOE_ROWS
    d = w2.shape[3]
    return pl.pallas_call(
        _moe_down_kernel,
        out_shape=jax.ShapeDtypeStruct((rows, d), BF16),
        grid_spec=pltpu.PrefetchScalarGridSpec(
            num_scalar_prefetch=3, grid=(d // tn, rows // tr),
            in_specs=[pl.BlockSpec((tr, fe), lambda n, i, te, tv, ti: (ti[i], 0)),
                      pl.BlockSpec((None, None, fe, tn), lambda n, i, te, tv, ti: (layer, te[i], 0, n))],
            out_specs=pl.BlockSpec((tr, tn), lambda n, i, te, tv, ti: (i, n))),
        compiler_params=_cparams(("arbitrary", "arbitrary")), name="moe_down",
    )(plan["te"], plan["tv"], plan["ti"], a, w2)


def _moe_combine_ln_kernel(ct, cc, cf, cl, cv, ys_ref, p0_ref, p1_ref, y_ref, gate_ref, lng_ref, lnb_ref,
                           *rest, alpha, with_h, split_chunks):
    yos_ref = None
    if with_h:
        sh_ref, sc_ref, yo_ref, h_ref, acc_ref = rest
    elif split_chunks:
        yo_ref, yos_ref, acc_ref = rest
    else:
        yo_ref, acc_ref = rest
    p = pl.program_id(0)
    tr, mc = ys_ref.shape[0], y_ref.shape[0]

    @pl.when(cf[p] == 1)
    def _():
        acc_ref[...] = jnp.zeros_like(acc_ref)

    @pl.when(cv[p] == 1)
    def _():
        cols = ct[p] * tr + lax.broadcasted_iota(jnp.int32, (mc, tr), 1)
        sel = jnp.where((p0_ref[...] == cols) | (p1_ref[...] == cols), 1.0, 0.0).astype(BF16)
        acc_ref[...] += _dot(sel, ys_ref[...])

    @pl.when(cl[p] == 1)
    def _():
        z = alpha * y_ref[...] + gate_ref[...] * acc_ref[...]
        y = _ln_epilogue(z, lng_ref, lnb_ref)
        if yos_ref is None:
            yo_ref[...] = y
        else:
            @pl.when(cc[p] < split_chunks)
            def _():
                yo_ref[...] = y

            @pl.when(cc[p] >= split_chunks)
            def _():
                yos_ref[...] = y
        if with_h:
            h_ref[...] = (y * (1.0 + sc_ref[...]) + sh_ref[...]).astype(BF16)


def _moe_combine_ln(tok, ys, plan, y, mods, layer, gate_which, ln_g, ln_b, alpha, mc, h_mod=None,
                    split_rows=0):
    m, d = y.shape
    tr = MOE_SEL_ROWS
    with_h = h_mod is not None
    col = lambda a: a.reshape(m, 1)

    def mod(l, which):
        return pl.BlockSpec((None, None, 1, d),
                            lambda p, ct, cc, cf, cl, cv: (l, tok.rid(cc[p], mc), 0, which))

    chunk = lambda w: pl.BlockSpec((mc, w), lambda p, ct, cc, cf, cl, cv: (cc[p], 0))
    ins = [ys, col(plan["pos0"]), col(plan["pos1"]), y, mods, ln_g, ln_b]
    specs = [pl.BlockSpec((tr, d), lambda p, ct, cc, cf, cl, cv: (ct[p], 0)), chunk(1), chunk(1), chunk(d),
             mod(layer, gate_which), _layer_spec(ln_g, layer), _layer_spec(ln_b, layer)]
    outs = [jax.ShapeDtypeStruct((m, d), F32)]
    ospecs = [chunk(d)]
    sc = split_rows // mc
    if sc:
        assert not with_h
        outs = [jax.ShapeDtypeStruct((split_rows, d), F32), jax.ShapeDtypeStruct((m - split_rows, d), F32)]
        ospecs = [pl.BlockSpec((mc, d), lambda p, ct, cc, cf, cl, cv: (jnp.minimum(cc[p], sc - 1), 0)),
                  pl.BlockSpec((mc, d), lambda p, ct, cc, cf, cl, cv: (jnp.maximum(cc[p] - sc, 0), 0))]
    if with_h:
        hl, hsh, hsc = h_mod
        ins += [mods, mods]
        specs += [mod(hl, hsh), mod(hl, hsc)]
        outs.append(jax.ShapeDtypeStruct((m, d), BF16))
        ospecs.append(chunk(d))
    return pl.pallas_call(
        functools.partial(_moe_combine_ln_kernel, alpha=alpha, with_h=with_h, split_chunks=sc),
        out_shape=outs,
        grid_spec=pltpu.PrefetchScalarGridSpec(
            num_scalar_prefetch=5, grid=(plan["maxp"],), in_specs=specs, out_specs=ospecs,
            scratch_shapes=[pltpu.VMEM((mc, d), F32)]),
        compiler_params=_cparams(("arbitrary",)), name="moe_combine_ln",
    )(*plan["combine"], *ins)


def _rope_tables(n_tokens, dim, pad_to):
    t = jnp.arange(n_tokens)
    row = (t // GRID_W).astype(F32)
    col = (t % GRID_W).astype(F32)
    half = dim // 2
    inv_freq = ROPE_THETA ** (-jnp.arange(0, half, 2, dtype=F32) / half)
    ar = row[:, None] * inv_freq[None, :]
    ac = col[:, None] * inv_freq[None, :]
    ang = jnp.concatenate([ar, ar, ac, ac], axis=-1)
    cos, sin = jnp.cos(ang), jnp.sin(ang)
    lo = (np.arange(dim) % (dim // 2)) < dim // 4
    sa = jnp.where(lo[None, :], -sin, 0.0)
    sb = jnp.where(lo[None, :], 0.0, sin)
    return cos, sa, sb


def _pad_lanes(x, width, fill):
    return jnp.concatenate([x, jnp.full((x.shape[0], width - x.shape[1]), fill, x.dtype)], axis=1)


def kernel(x_prompt, x_sample, cache_da_k, cache_da_v, cache_mla_ckv, cache_mla_krope, cache_na_k, cache_na_v, c, c_ctx, w_ada, b_ada, w_in, da_lq1, da_lk1, da_lq2, da_lk2, da_subln, mla_gq, mla_gkv, mla_wuq, mla_wukv, na_rpb, w_out, ln1_g, ln1_b, ln2_g, ln2_b, ffn_w1, ffn_w3, ffn_w2, moe_router, moe_w1, moe_w3, moe_w2):
    nbp, seq, d = x_prompt.shape
    nbs, s_lat, _ = x_sample.shape
    depth = w_in.shape[0]
    past = cache_da_k.shape[2]
    mp, ms = nbp * seq, nbs * s_lat
    m = mp + ms
    tok = _Tok(mp, s_lat, nbs)
    tm = 512
    assert mp % s_lat == 0 and s_lat % tm == 0 and mp % tm == 0 and seq % LANE == 0 and nbs + 1 <= COND_ROWS
    alpha = (2.0 * depth) ** 0.25
    rows_n = s_lat // GRID_W

    cond = jnp.concatenate([c_ctx[None], c, jnp.zeros((COND_ROWS - 1 - nbs, d), F32)], axis=0)
    mods = _ada(cond, w_ada, b_ada).reshape(depth, COND_ROWS, 1, 6 * d)

    cos, sa, sb = _rope_tables(s_lat, DA_QK, LANE)
    rope_d = tuple(jnp.tile(t, (1, 2)) for t in (cos, sa, sb))
    rope_m = (_pad_lanes(cos, LANE, 1.0), _pad_lanes(sa, LANE, 0.0), _pad_lanes(sb, LANE, 0.0))

    y = (x_prompt.reshape(mp, d), x_sample.reshape(ms, d))
    h = _modulate(tok, y[0], y[1], mods, 0, tm)

    split = C_CKV + MLA_KV_RANK + MLA_ROPE
    w_in_b = w_in.astype(BF16)
    w_in_p = jnp.concatenate([w_in_b[:, :, :split], jnp.zeros((depth, d, C_NAQ - split), BF16),
                              w_in_b[:, :, split:]], axis=2)
    wuq = mla_wuq.reshape(depth, MLA_Q_RANK, MLA_HEADS, MLA_NOPE + MLA_ROPE)
    wuq_p = jnp.concatenate(
        [wuq, jnp.zeros((depth, MLA_Q_RANK, MLA_HEADS, MLA_QK_PAD - MLA_NOPE - MLA_ROPE), F32)],
        axis=3).reshape(depth, MLA_Q_RANK, MLA_HEADS * MLA_QK_PAD).astype(BF16)
    wukv = mla_wukv.reshape(depth, MLA_KV_RANK, MLA_HEADS, MLA_NOPE + MLA_V)
    wk = wukv[..., :MLA_NOPE].reshape(depth, MLA_KV_RANK, MLA_HEADS * MLA_NOPE).astype(BF16)
    wv = wukv[..., MLA_NOPE:].reshape(depth, MLA_KV_RANK, MLA_HEADS * MLA_V).astype(BF16)
    w_out_b = w_out.astype(BF16)
    ffn_w2_b = ffn_w2.astype(BF16)
    n_moe = moe_router.shape[0]
    wr = jnp.concatenate([moe_router, jnp.zeros((n_moe, d, LANE - N_EXPERTS), F32)], axis=2).astype(BF16)
    vec = lambda a: a.reshape(a.shape[0], 1, a.shape[1])
    gq, gkv, gsub = vec(mla_gq), vec(mla_gkv), vec(da_subln)
    lams = (vec(da_lq1), vec(da_lk1), vec(da_lq2), vec(da_lk2))
    ln1g, ln1b, ln2g, ln2b = vec(ln1_g), vec(ln1_b), vec(ln2_g), vec(ln2_b)
    cda_k = cache_da_k.reshape(nbs, depth, past, DA_HEADS * LANE)
    c_ckv = cache_mla_ckv.reshape(nbs * depth * past, MLA_KV_RANK)
    c_kr = jnp.concatenate([cache_mla_krope, jnp.zeros((nbs, depth, past, LANE - MLA_ROPE), F32)],
                           axis=-1).reshape(nbs * depth * past, LANE)
    bias, na_pat = _na_bias_tables(na_rpb, rows_n)
    stacks = None
    tm_big = 1024 if (mp % 1024 == 0 and s_lat % 1024 == 0) else tm

    st = [[] for _ in range(6)]
    for l in range(depth):
        lam_init = 0.8 - 0.6 * math.exp(-0.3 * l)
        proj = _mm(h, w_in_p, l, tm_big, IN_PROJ_TN)

        o_p, *stacks = _ctx_attention(proj, l, depth, nbp, seq, gkv, wk, wv, gq, wuq_p, lams, gsub, lam_init,
                                      stacks)

        kcat_s, vb_s, qcat_s, qa_s, ka_s = _lat_prep(proj, l, mp, ms, s_lat, tm, gkv, wk, wv, gq, wuq_p,
                                                     rope_m, rope_d)
        kcat_c, vb_c = _mla_cache(c_ckv, c_kr, l, depth, nbs, past, wk, wv)
        tq = 512 if s_lat % 512 == 0 else s_lat
        oa_s = _da(qa_s, ka_s, proj, lams, gsub, l, lam_init, nb=nbs, sq=s_lat, sk=s_lat, tq=tq,
                   vrow0=mp, vcol0=C_DAV, kc=cda_k, vc=cache_da_v, skc=past)
        ob_s = _attn(qcat_s, kcat_s, vb_s, nb=nbs, sq=s_lat, sk=s_lat, tq=tq, heads=MLA_HEADS,
                     hp=MLA_HEADS_PER_STEP, dk=MLA_QK_PAD, dv=MLA_V, scale=MLA_SCALE, kc=kcat_c, vc=vb_c,
                     skc=past)
        oc_s = _na_latent(proj, l, mp, nbs, s_lat, cache_na_k, cache_na_v, bias, na_pat)

        i = l // 2
        moe = (l % 2 == 1)
        res = _mm_ln(tok, [o_p], w_out_b, l, y, mods, l, 2, ln1g, ln1b, alpha, tm, d, h_mod=(l, 3, 4),
                     w_router=wr if moe else None, router_layer=i, xs_tail=[oa_s, ob_s, oc_s],
                     name="out_proj_ln")
        y, h2 = res[0], res[1]

        nxt = (l + 1, 0, 1) if l + 1 < depth else None
        if not moe:
            a = _ffn_up(h2, ffn_w1, ffn_w3, i, tm_big, FFN_UP_TF)
            res = _mm_ln(tok, [a], ffn_w2_b, i, y, mods, l, 5, ln2g, ln2b, alpha, tm, FFN_DOWN_TK,
                         h_mod=nxt, name="ffn_down_ln")
        else:
            plan = _moe_plan(res[2], res[3], m, tm)
            xs, grow = _moe_gather(h2, plan, tm)
            a = _moe_up(xs, grow, moe_w1, moe_w3, i, plan, 256)
            ys = _moe_down(a, moe_w2, i, plan, 512)
            res = _moe_combine_ln(tok, ys, plan, y, mods, l, 5, ln2g, ln2b, alpha, tm, h_mod=nxt,
                                  split_rows=0 if nxt is not None else mp)
        y = res[0] if nxt is not None else tuple(res)
        if nxt is not None:
            h = res[1]

        pp = proj[:mp]
        st[0].append(pp[:, C_DAK:C_DAK + 512].reshape(nbp, seq, DA_HEADS, 2, DA_QK))
        st[3].append(pp[:, C_CKV + MLA_KV_RANK:C_CKV + MLA_KV_RANK + MLA_ROPE].reshape(nbp, seq, MLA_ROPE))

    if not isinstance(y, tuple):
        y = (y[:mp], y[mp:])
    new_ckv, new_da_v, new_na_k, new_na_v = stacks
    return (y[0].reshape(nbp, seq, d), y[1].reshape(nbs, s_lat, d), jnp.stack(st[0], axis=1), new_da_v,
            new_ckv, jnp.stack(st[3], axis=1), new_na_k, new_na_v)
```

```python
import functools
import math

import numpy as np
import jax
import jax.numpy as jnp
from jax import lax
from jax.experimental import pallas as pl
from jax.experimental.pallas import tpu as pltpu

F32 = jnp.float32
BF16 = jnp.bfloat16

GRID_W = 64
DA_QK = 64
DA_V = 128
DA_HEADS = 4
MLA_Q_RANK = 512
MLA_KV_RANK = 256
MLA_NOPE = 128
MLA_ROPE = 64
MLA_V = 128
MLA_HEADS = 8
MLA_SCALE = (MLA_NOPE + MLA_ROPE) ** -0.5
NA_DIM = 128
NA_HEADS = 4
NA_KH = 8
NA_KW = 16
N_EXPERTS = 8
ROPE_THETA = 10000.0
LN_EPS = 1e-5
RMS_EPS = 1e-6
NEG_INF = -1e30
LOG2E = 1.4426950408889634
LANE = 128
COND_ROWS = 8
VMEM_LIMIT = 56 * 1024 * 1024

C_DAQ, C_DAK, C_DAV, C_CQ, C_CKV, C_NAQ, C_NAK, C_NAV = 0, 512, 1024, 1536, 2048, 2560, 3072, 3584
P_COLS = 4096
MLA_QK_PAD = 256
FFN_DOWN_TK = 1408
IN_PROJ_TN = 2048
FFN_UP_TF = 512
NA_ROWS_PER_STEP = 4
MLA_HEADS_PER_STEP = 4


def _cparams(sem):
    return pltpu.CompilerParams(dimension_semantics=sem, vmem_limit_bytes=VMEM_LIMIT)


def _dot(a, b):
    return jnp.dot(a, b, preferred_element_type=F32)


def _dot_nt(a, b):
    return lax.dot_general(a, b, (((1,), (1,)), ((), ())), preferred_element_type=F32)


def _silu(x):
    return x * (1.0 / (1.0 + jnp.exp(-x)))


def _rms(x, g):
    return x * lax.rsqrt(jnp.mean(x * x, axis=-1, keepdims=True) + RMS_EPS) * g


def _rope(x, cos, sa, sb):
    return x * cos + pltpu.roll(x, LANE - 16, 1) * sa + pltpu.roll(x, 16, 1) * sb


def _ada_kernel(c_ref, w_ref, b_ref, o_ref):
    s = _silu(c_ref[...]).astype(BF16)
    o_ref[...] = _dot(s, w_ref[...].astype(BF16)) + b_ref[...]


def _ada(cond, w_ada, b_ada):
    n_layers, d, n = w_ada.shape
    tn = 1024
    return pl.pallas_call(
        _ada_kernel,
        out_shape=jax.ShapeDtypeStruct((n_layers, COND_ROWS, n), F32),
        grid=(n_layers, n // tn),
        in_specs=[pl.BlockSpec((COND_ROWS, d), lambda l, j: (0, 0)),
                  pl.BlockSpec((None, d, tn), lambda l, j: (l, 0, j)),
                  pl.BlockSpec((None, 1, tn), lambda l, j: (l, 0, j))],
        out_specs=pl.BlockSpec((None, COND_ROWS, tn), lambda l, j: (l, 0, j)),
        compiler_params=_cparams(("arbitrary", "arbitrary")), name="ada",
    )(cond, w_ada, b_ada.reshape(n_layers, 1, n))


class _Tok:
    def __init__(self, mp, s_lat, n_lat_batches):
        self.mp, self.s_lat, self.nb = mp, s_lat, n_lat_batches
        self.m = mp + s_lat * n_lat_batches

    def rid(self, i, tm):
        r0 = i * tm
        return jnp.where(r0 < self.mp, 0, 1 + (r0 - self.mp) // self.s_lat)

    def mod_spec(self, layer, which, tm, d, grid_rank=1):
        if grid_rank == 1:
            return pl.BlockSpec((None, None, 1, d), lambda i: (layer, self.rid(i, tm), 0, which))
        return pl.BlockSpec((None, None, 1, d), lambda i, k: (layer, self.rid(i, tm), 0, which))


def _head_tail_specs(head, tail, tm, width, rank):
    ht = head.shape[0] // tm
    if rank == 1:
        return ht, [pl.BlockSpec((tm, width), lambda i: (jnp.minimum(i, ht - 1), 0)),
                    pl.BlockSpec((tm, width), lambda i: (jnp.maximum(i - ht, 0), 0))]
    return ht, [pl.BlockSpec((tm, width), lambda i, k: (jnp.minimum(i, ht - 1), 0)),
                pl.BlockSpec((tm, width), lambda i, k: (jnp.maximum(i - ht, 0), 0))]


def _modulate_kernel(xp_ref, xs_ref, sh_ref, sc_ref, o_ref, *, head_tiles):
    x = jnp.where(pl.program_id(0) < head_tiles, xp_ref[...], xs_ref[...])
    o_ref[...] = (x * (1.0 + sc_ref[...]) + sh_ref[...]).astype(BF16)


def _modulate(tok, x_p, x_s, mods, layer, tm):
    d = x_p.shape[1]
    ht, xspecs = _head_tail_specs(x_p, x_s, tm, d, 1)
    return pl.pallas_call(
        functools.partial(_modulate_kernel, head_tiles=ht),
        out_shape=jax.ShapeDtypeStruct((tok.m, d), BF16),
        grid=(tok.m // tm,),
        in_specs=xspecs + [tok.mod_spec(layer, 0, tm, d), tok.mod_spec(layer, 1, tm, d)],
        out_specs=pl.BlockSpec((tm, d), lambda i: (i, 0)),
        compiler_params=_cparams(("arbitrary",)), name="modulate",
    )(x_p, x_s, mods, mods)


def _mm_kernel(x_ref, w_ref, o_ref):
    o_ref[...] = _dot(x_ref[...], w_ref[...])


def _layer_spec(arr, layer):
    zeros = (0,) * (arr.ndim - 1)
    return pl.BlockSpec((None,) + arr.shape[1:], lambda *_: (layer,) + zeros)


def _mm(x, w, layer, tm, tn):
    m, k = x.shape
    n = w.shape[2]
    return pl.pallas_call(
        _mm_kernel,
        out_shape=jax.ShapeDtypeStruct((m, n), F32),
        grid=(n // tn, m // tm),
        in_specs=[pl.BlockSpec((tm, k), lambda j, i: (i, 0)),
                  pl.BlockSpec((None, k, tn), lambda j, i: (layer, 0, j))],
        out_specs=pl.BlockSpec((tm, tn), lambda j, i: (i, j)),
        compiler_params=_cparams(("arbitrary", "arbitrary")), name="in_proj",
    )(x, w)


def _lat_prep_kernel(ckv_ref, cq_ref, daq_ref, dak_ref, gkv_ref, wk_ref, wv_ref, gq_ref, wuq_ref,
                     cm_ref, sam_ref, sbm_ref, cd_ref, sad_ref, sbd_ref,
                     kcat_ref, vb_ref, qcat_ref, qa_ref, ka_ref):
    cb = _rms(ckv_ref[:, :MLA_KV_RANK], gkv_ref[...]).astype(BF16)
    kr = ckv_ref[:, MLA_KV_RANK:MLA_KV_RANK + LANE]
    kr = _rope(kr, cm_ref[...], sam_ref[...], sbm_ref[...]).astype(BF16)
    kn = _dot(cb, wk_ref[...]).astype(BF16)
    vb_ref[...] = _dot(cb, wv_ref[...]).astype(BF16)
    q = _dot(_rms(cq_ref[...], gq_ref[...]).astype(BF16), wuq_ref[...])
    for h in range(MLA_HEADS):
        lo = h * MLA_QK_PAD
        kcat_ref[:, lo:lo + MLA_NOPE] = kn[:, h * MLA_NOPE:(h + 1) * MLA_NOPE]
        kcat_ref[:, lo + MLA_NOPE:lo + MLA_QK_PAD] = kr
        qcat_ref[:, lo:lo + MLA_NOPE] = q[:, lo:lo + MLA_NOPE].astype(BF16)
        qr = _rope(q[:, lo + MLA_NOPE:lo + MLA_QK_PAD], cm_ref[...], sam_ref[...], sbm_ref[...])
        qcat_ref[:, lo + MLA_NOPE:lo + MLA_QK_PAD] = qr.astype(BF16)
    for h in range(DA_HEADS):
        sl = slice(h * LANE, (h + 1) * LANE)
        qa_ref[:, sl] = _rope(daq_ref[:, sl], cd_ref[...], sad_ref[...], sbd_ref[...]).astype(BF16)
        ka_ref[:, sl] = _rope(dak_ref[:, sl], cd_ref[...], sad_ref[...], sbd_ref[...]).astype(BF16)


def _lat_prep(proj, layer, row0, rows, s_lat, tm, gkv, wk, wv, gq, wuq, rope_m, rope_d):
    b0 = row0 // tm
    nt = s_lat // tm
    blk = lambda c: pl.BlockSpec((tm, 512), lambda i: (b0 + i, c // 512))
    table = pl.BlockSpec((tm, LANE), lambda i: (i % nt, 0))
    params = [gkv, wk, wv, gq, wuq]
    widths = [MLA_HEADS * MLA_QK_PAD, MLA_HEADS * MLA_V, MLA_HEADS * MLA_QK_PAD, DA_HEADS * LANE,
              DA_HEADS * LANE]
    return pl.pallas_call(
        _lat_prep_kernel,
        out_shape=[jax.ShapeDtypeStruct((rows, w), BF16) for w in widths],
        grid=(rows // tm,),
        in_specs=[blk(C_CKV), blk(C_CQ), blk(C_DAQ), blk(C_DAK)] + [_layer_spec(a, layer) for a in params]
        + [table] * 6,
        out_specs=[pl.BlockSpec((tm, w), lambda i: (i, 0)) for w in widths],
        compiler_params=_cparams(("arbitrary",)), name="lat_prep",
    )(proj, proj, proj, proj, *params, *rope_m, *rope_d)


def _mla_cache_kernel(ckv_ref, kr_ref, wk_ref, wv_ref, kcat_ref, vb_ref):
    cb = ckv_ref[...].astype(BF16)
    kn = _dot(cb, wk_ref[...]).astype(BF16)
    vb_ref[...] = _dot(cb, wv_ref[...]).astype(BF16)
    kr = kr_ref[...].astype(BF16)
    for h in range(MLA_HEADS):
        kcat_ref[:, h * MLA_QK_PAD:h * MLA_QK_PAD + MLA_NOPE] = kn[:, h * MLA_NOPE:(h + 1) * MLA_NOPE]
        kcat_ref[:, h * MLA_QK_PAD + MLA_NOPE:(h + 1) * MLA_QK_PAD] = kr


def _mla_cache(ckv, kr, layer, depth, nb, past, wk, wv):
    rows = lambda w: pl.BlockSpec((past, w), lambda b: (b * depth + layer, 0))
    out = lambda w: pl.BlockSpec((past, w), lambda b: (b, 0))
    return pl.pallas_call(
        _mla_cache_kernel,
        out_shape=[jax.ShapeDtypeStruct((nb * past, MLA_HEADS * MLA_QK_PAD), BF16),
                   jax.ShapeDtypeStruct((nb * past, MLA_HEADS * MLA_V), BF16)],
        grid=(nb,),
        in_specs=[rows(MLA_KV_RANK), rows(LANE), _layer_spec(wk, layer), _layer_spec(wv, layer)],
        out_specs=[out(MLA_HEADS * MLA_QK_PAD), out(MLA_HEADS * MLA_V)],
        compiler_params=_cparams(("arbitrary",)), name="mla_cache",
    )(ckv, kr, wk, wv)


def _softmax_parts(s, s2, scale):
    c = scale * LOG2E
    m = jnp.max(s, axis=-1, keepdims=True)
    if s2 is not None:
        m = jnp.maximum(m, jnp.max(s2, axis=-1, keepdims=True))
    e = jnp.exp2((s - m) * c)
    den = jnp.sum(e, axis=-1, keepdims=True)
    e2 = None
    if s2 is not None:
        e2 = jnp.exp2((s2 - m) * c)
        den = den + jnp.sum(e2, axis=-1, keepdims=True)
    return e, e2, 1.0 / den


def _attn_head(q, k, v, scale, kc=None, vc=None):
    s = _dot_nt(q, k)
    s2 = _dot_nt(q, kc) if kc is not None else None
    e, e2, inv = _softmax_parts(s, s2, scale)
    o = _dot(e.astype(BF16), v)
    if kc is not None:
        o = o + _dot(e2.astype(BF16), vc)
    return o * inv


def _attn_kernel(*refs, heads, dk, dv, scale, has_ctx):
    if has_ctx:
        q_ref, k_ref, v_ref, kc_ref, vc_ref, o_ref = refs
    else:
        q_ref, k_ref, v_ref, o_ref = refs
    for h in range(heads):
        ks, vs = slice(h * dk, (h + 1) * dk), slice(h * dv, (h + 1) * dv)
        o = _attn_head(q_ref[:, ks].astype(BF16), k_ref[:, ks].astype(BF16), v_ref[:, vs].astype(BF16), scale,
                       kc_ref[:, ks].astype(BF16) if has_ctx else None,
                       vc_ref[:, vs].astype(BF16) if has_ctx else None)
        o_ref[:, vs] = o.astype(BF16)


def _attn(q, k, v, *, nb, sq, sk, tq, heads, hp, dk, dv, scale, qrow0=0, krow0=0, qcol0=0, kcol0=0, vcol0=0,
          kc=None, vc=None, skc=0):
    nq = sq // tq
    ng = heads // hp
    wq, wv = hp * dk, hp * dv
    has_ctx = kc is not None
    ins = [q, k, v]
    specs = [pl.BlockSpec((tq, wq), lambda b, g, i: (qrow0 // tq + b * nq + i, qcol0 // wq + g)),
             pl.BlockSpec((sk, wq), lambda b, g, i: (krow0 // sk + b, kcol0 // wq + g)),
             pl.BlockSpec((sk, wv), lambda b, g, i: (krow0 // sk + b, vcol0 // wv + g))]
    if has_ctx:
        ins += [kc, vc]
        specs += [pl.BlockSpec((skc, wq), lambda b, g, i: (b, g)),
                  pl.BlockSpec((skc, wv), lambda b, g, i: (b, g))]
    return pl.pallas_call(
        functools.partial(_attn_kernel, heads=hp, dk=dk, dv=dv, scale=scale, has_ctx=has_ctx),
        out_shape=jax.ShapeDtypeStruct((nb * sq, heads * dv), BF16),
        grid=(nb, ng, nq), in_specs=specs,
        out_specs=pl.BlockSpec((tq, wv), lambda b, g, i: (b * nq + i, g)),
        compiler_params=_cparams(("arbitrary", "arbitrary", "arbitrary")), name="attn_h%d" % heads,
    )(*ins)


def _da_lambda(lq1, lk1, lq2, lk2, lam_init):
    return (jnp.exp(jnp.sum(lq1[...] * lk1[...], axis=-1, keepdims=True))
            - jnp.exp(jnp.sum(lq2[...] * lk2[...], axis=-1, keepdims=True)) + lam_init)


def _da_head(q, k, v, lam, g, lam_init, kc=None, vc=None):
    first = lax.broadcasted_iota(jnp.int32, (1, LANE), 1) < DA_QK
    q = q.astype(F32) * (DA_QK ** -0.5)
    qs = (jnp.where(first, q, 0.0).astype(BF16), jnp.where(first, 0.0, q).astype(BF16))
    o = _attn_head(qs[0], k, v, 1.0, kc, vc) - lam * _attn_head(qs[1], k, v, 1.0, kc, vc)
    return _rms(o, g) * (1.0 - lam_init)


def _da_kernel(*refs, has_ctx, lam_init):
    if has_ctx:
        q_ref, k_ref, v_ref, kc_ref, vc_ref, lq1, lk1, lq2, lk2, g_ref, o_ref = refs
    else:
        q_ref, k_ref, v_ref, lq1, lk1, lq2, lk2, g_ref, o_ref = refs
    lam = _da_lambda(lq1, lk1, lq2, lk2, lam_init)
    for h in range(DA_HEADS):
        sl = slice(h * LANE, (h + 1) * LANE)
        o = _da_head(q_ref[:, sl], k_ref[:, sl].astype(BF16), v_ref[:, sl].astype(BF16), lam, g_ref[...],
                     lam_init, kc_ref[:, sl].astype(BF16) if has_ctx else None,
                     vc_ref[:, h, :].astype(BF16) if has_ctx else None)
        o_ref[:, sl] = o.astype(BF16)


def _ctx_kernel(p_ref, gkv_ref, wk_ref, wv_ref, gq_ref, wuq_ref, lq1, lk1, lq2, lk2, g_ref, *rest, lam_init):
    o_ref, ckvn_ref, dav_ref, nak_ref, nav_ref = rest[-5:]
    for h in range(DA_HEADS):
        dav_ref[:, h, :] = p_ref[:, C_DAV + h * DA_V:C_DAV + (h + 1) * DA_V]
    for h in range(NA_HEADS):
        nak_ref[:, h, :] = p_ref[:, C_NAK + h * NA_DIM:C_NAK + (h + 1) * NA_DIM]
        nav_ref[:, h, :] = p_ref[:, C_NAV + h * NA_DIM:C_NAV + (h + 1) * NA_DIM]
    col = lambda c0, h, w=LANE: slice(c0 + h * w, c0 + (h + 1) * w)
    lam = _da_lambda(lq1, lk1, lq2, lk2, lam_init)
    for h in range(DA_HEADS):
        o = _da_head(p_ref[:, col(C_DAQ, h)], p_ref[:, col(C_DAK, h)].astype(BF16),
                     p_ref[:, col(C_DAV, h)].astype(BF16), lam, g_ref[...], lam_init)
        o_ref[:, col(0, h)] = o.astype(BF16)
    ckvn = _rms(p_ref[:, C_CKV:C_CKV + MLA_KV_RANK], gkv_ref[...])
    ckvn_ref[...] = ckvn
    cb = ckvn.astype(BF16)
    kn = _dot(cb, wk_ref[...]).astype(BF16)
    vb = _dot(cb, wv_ref[...]).astype(BF16)
    kr = p_ref[:, C_CKV + MLA_KV_RANK:C_CKV + MLA_KV_RANK + LANE].astype(BF16)
    q = _dot(_rms(p_ref[:, C_CQ:C_CQ + MLA_Q_RANK], gq_ref[...]).astype(BF16), wuq_ref[...]).astype(BF16)
    ob0 = DA_HEADS * DA_V
    for h in range(MLA_HEADS):
        kh = jnp.concatenate([kn[:, col(0, h)], kr], axis=1)
        o = _attn_head(q[:, col(0, h, MLA_QK_PAD)], kh, vb[:, col(0, h)], MLA_SCALE)
        o_ref[:, col(ob0, h)] = o.astype(BF16)
    oc0 = ob0 + MLA_HEADS * MLA_V
    for h in range(NA_HEADS):
        o = _attn_head(p_ref[:, col(C_NAQ, h)].astype(BF16), p_ref[:, col(C_NAK, h)].astype(BF16),
                       p_ref[:, col(C_NAV, h)].astype(BF16), NA_DIM ** -0.5)
        o_ref[:, col(oc0, h)] = o.astype(BF16)


def _ctx_attention(proj, layer, depth, nb, seq, gkv, wk, wv, gq, wuq, lams, g, lam_init, stacks):
    d_out = DA_HEADS * DA_V + MLA_HEADS * MLA_V + NA_HEADS * NA_DIM
    params = [gkv, wk, wv, gq, wuq] + list(lams) + [g]
    stack_shapes = [(nb, depth, seq, MLA_KV_RANK), (nb, depth, seq, DA_HEADS, DA_V),
                    (nb, depth, seq, NA_HEADS, NA_DIM), (nb, depth, seq, NA_HEADS, NA_DIM)]
    stack_specs = [pl.BlockSpec((None, None) + s[2:], lambda b, n=len(s): (b, layer) + (0,) * (n - 2))
                   for s in stack_shapes]
    prev = list(stacks) if stacks is not None else []
    n_in = 1 + len(params)
    return pl.pallas_call(
        functools.partial(_ctx_kernel, lam_init=lam_init),
        out_shape=[jax.ShapeDtypeStruct((nb * seq, d_out), BF16)]
        + [jax.ShapeDtypeStruct(s, F32) for s in stack_shapes],
        grid=(nb,),
        in_specs=[pl.BlockSpec((seq, P_COLS), lambda b: (b, 0))] + [_layer_spec(a, layer) for a in params]
        + [pl.BlockSpec(memory_space=pl.ANY)] * len(prev),
        out_specs=[pl.BlockSpec((seq, d_out), lambda b: (b, 0))] + stack_specs,
        input_output_aliases={n_in + k: 1 + k for k in range(len(prev))},
        compiler_params=_cparams(("arbitrary",)), name="ctx_attention",
    )(proj, *params, *prev)


def _da(q, k, v, lams, g, layer, lam_init, *, nb, sq, sk, tq, qrow0=0, krow0=0, qcol0=0, kcol0=0, vcol0=0,
        vrow0=None, kc=None, vc=None, skc=0):
    nq = sq // tq
    w = DA_HEADS * LANE
    vrow0 = krow0 if vrow0 is None else vrow0
    has_ctx = kc is not None
    ins = [q, k, v]
    specs = [pl.BlockSpec((tq, w), lambda b, i: (qrow0 // tq + b * nq + i, qcol0 // w)),
             pl.BlockSpec((sk, w), lambda b, i: (krow0 // sk + b, kcol0 // w)),
             pl.BlockSpec((sk, w), lambda b, i: (vrow0 // sk + b, vcol0 // w))]
    if has_ctx:
        ins += [kc, vc]
        specs += [pl.BlockSpec((None, None, skc, w), lambda b, i: (b, layer, 0, 0)),
                  pl.BlockSpec((None, None, skc, DA_HEADS, DA_V), lambda b, i: (b, layer, 0, 0, 0))]
    ins += list(lams) + [g]
    specs += [_layer_spec(a, layer) for a in ins[-5:]]
    return pl.pallas_call(
        functools.partial(_da_kernel, has_ctx=has_ctx, lam_init=lam_init),
        out_shape=jax.ShapeDtypeStruct((nb * sq, w), BF16),
        grid=(nb, nq), in_specs=specs,
        out_specs=pl.BlockSpec((tq, w), lambda b, i: (b * nq + i, 0)),
        compiler_params=_cparams(("arbitrary", "arbitrary")), name="diff_attn",
    )(*ins)


def _na_geometry(rows_n):
    kh = min(NA_KH, rows_n)
    rb = NA_ROWS_PER_STEP if rows_n % NA_ROWS_PER_STEP == 0 else 1
    uw = min(rows_n, kh + rb - 1)
    return kh, rb, uw


def _na_kernel(pat, q_ref, k_ref, v_ref, kc_ref, vc_ref, bias_ref, o_ref, *, rows_n, kh, rb, uw):
    k0 = jnp.clip(pl.program_id(1) * rb - kh // 2, 0, rows_n - uw)
    start = pl.multiple_of(k0 * GRID_W, GRID_W)
    nwin = uw * GRID_W
    scale = NA_DIM ** -0.5
    for h in range(NA_HEADS):
        sl = slice(h * NA_DIM, (h + 1) * NA_DIM)
        q = q_ref[:, sl].astype(BF16)
        kw = k_ref[pl.ds(start, nwin), sl].astype(BF16)
        vw = v_ref[pl.ds(start, nwin), sl].astype(BF16)
        s = _dot_nt(q, kw) * scale + bias_ref[h]
        s2 = _dot_nt(q, kc_ref[:, h, :].astype(BF16)) * scale
        e, e2, inv = _softmax_parts(s, s2, 1.0)
        o = _dot(e.astype(BF16), vw) + _dot(e2.astype(BF16), vc_ref[:, h, :].astype(BF16))
        o_ref[:, sl] = (o * inv).astype(BF16)


def _na_latent(proj, layer, row0, nb, s_lat, kc, vc, bias, pattern_of_step):
    rows_n = s_lat // GRID_W
    kh, rb, uw = _na_geometry(rows_n)
    w = NA_HEADS * NA_DIM
    skc = kc.shape[2]
    tq = rb * GRID_W
    steps = rows_n // rb
    return pl.pallas_call(
        functools.partial(_na_kernel, rows_n=rows_n, kh=kh, rb=rb, uw=uw),
        out_shape=jax.ShapeDtypeStruct((nb * s_lat, w), BF16),
        grid_spec=pltpu.PrefetchScalarGridSpec(
            num_scalar_prefetch=1, grid=(nb, steps),
            in_specs=[pl.BlockSpec((tq, w), lambda b, r, pat: (row0 // tq + b * steps + r, C_NAQ // w)),
                      pl.BlockSpec((s_lat, w), lambda b, r, pat: (row0 // s_lat + b, C_NAK // w)),
                      pl.BlockSpec((s_lat, w), lambda b, r, pat: (row0 // s_lat + b, C_NAV // w)),
                      pl.BlockSpec((None, None, skc, NA_HEADS, NA_DIM), lambda b, r, pat: (b, layer, 0, 0, 0)),
                      pl.BlockSpec((None, None, skc, NA_HEADS, NA_DIM), lambda b, r, pat: (b, layer, 0, 0, 0)),
                      pl.BlockSpec((None, NA_HEADS, None, tq, uw * GRID_W),
                                   lambda b, r, pat: (layer, 0, pat[r], 0, 0))],
            out_specs=pl.BlockSpec((tq, w), lambda b, r, pat: (b * steps + r, 0))),
        compiler_params=_cparams(("arbitrary", "arbitrary")), name="na_latent",
    )(pattern_of_step, proj, proj, proj, kc, vc, bias)


def _na_bias_tables(rpb, rows_n):
    kh = min(NA_KH, rows_n)
    qc = np.arange(GRID_W)[:, None]
    kc = np.arange(GRID_W)[None, :]
    col_start = np.clip(qc - NA_KW // 2, 0, GRID_W - NA_KW)
    valid = (kc >= col_start) & (kc < col_start + NA_KW)
    coff = np.clip(kc - qc, -(NA_KW - 1), NA_KW - 1) + (NA_KW - 1)
    onehot = (coff.reshape(-1)[None, :] == np.arange(2 * NA_KW - 1)[:, None]).astype(np.float32)
    n_l = rpb.shape[0]
    t = jnp.einsum('lhrc,cx->lhrx', rpb.astype(F32), jnp.asarray(onehot), precision=lax.Precision.HIGHEST)
    t = t.reshape(n_l, NA_HEADS, 2 * NA_KH - 1, GRID_W, GRID_W)
    t = jnp.where(jnp.asarray(valid)[None, None, None], t, NEG_INF)
    strips = []
    for v in range(kh):
        lo = NA_KH - 1 - v
        s = t[:, :, lo:lo + kh]
        strips.append(jnp.transpose(s, (0, 1, 3, 2, 4)).reshape(n_l, NA_HEADS, GRID_W, kh * GRID_W))
    _, rb, uw = _na_geometry(rows_n)
    patterns, pattern_of_step = [], []
    for t0 in range(rows_n // rb):
        k0 = int(np.clip(t0 * rb - kh // 2, 0, rows_n - uw))
        key = []
        for i in range(rb):
            r = t0 * rb + i
            rs = int(np.clip(r - kh // 2, 0, rows_n - kh))
            key.append((rs - k0, r - rs))
        key = tuple(key)
        if key not in patterns:
            patterns.append(key)
        pattern_of_step.append(patterns.index(key))
    blocks = []
    for key in patterns:
        rows = [jnp.pad(strips[v], ((0, 0), (0, 0), (0, 0), (off * GRID_W, (uw - kh - off) * GRID_W)),
                        constant_values=NEG_INF) for off, v in key]
        blocks.append(jnp.concatenate(rows, axis=2))
    return jnp.stack(blocks, axis=2), jnp.asarray(pattern_of_step, jnp.int32)


def _ln_epilogue(z, g_ref, b_ref):
    mu = jnp.mean(z, axis=-1, keepdims=True)
    zc = z - mu
    var = jnp.mean(zc * zc, axis=-1, keepdims=True)
    return zc * lax.rsqrt(var + LN_EPS) * g_ref[...] + b_ref[...]


def _router_info(h, wr_ref, cnt_ref):
    n = h.shape[0]
    logits = _dot(h, wr_ref[...])
    lane = lax.broadcasted_iota(jnp.int32, logits.shape, 1)
    lg = jnp.where(lane < N_EXPERTS, logits, -jnp.inf)
    m1 = jnp.max(lg, axis=-1, keepdims=True)
    i1 = jnp.min(jnp.where(lg == m1, lane, LANE), axis=-1, keepdims=True)
    lg2 = jnp.where(lane == i1, -jnp.inf, lg)
    m2 = jnp.max(lg2, axis=-1, keepdims=True)
    i2 = jnp.min(jnp.where(lg2 == m2, lane, LANE), axis=-1, keepdims=True)
    e2 = jnp.exp(m2 - m1)
    inv = 1.0 / (1.0 + e2)
    oh1, oh2 = lane == i1, lane == i2
    o1, o2 = jnp.where(oh1, 1.0, 0.0), jnp.where(oh2, 1.0, 0.0)
    below = lax.broadcasted_iota(jnp.int32, (n, n), 1) < lax.broadcasted_iota(jnp.int32, (n, n), 0)
    tri = jnp.where(below, 1.0, 0.0).astype(BF16)
    p1 = _dot(tri, o1.astype(BF16))
    p2 = _dot(tri, o2.astype(BF16))
    tot1 = jnp.sum(o1, axis=0, keepdims=True)
    tot2 = jnp.sum(o2, axis=0, keepdims=True)
    cnt = cnt_ref[...]
    rank0 = jnp.sum(jnp.where(oh1, cnt + p1, 0.0), axis=-1, keepdims=True)
    rank1 = jnp.sum(jnp.where(oh2, cnt + tot1 + p2, 0.0), axis=-1, keepdims=True)
    cnt_ref[...] = cnt + tot1 + tot2
    cols = (i1.astype(F32), i2.astype(F32), inv, e2 * inv, rank0, rank1)
    info = jnp.zeros(logits.shape, F32)
    for c, v in enumerate(cols):
        info = jnp.where(lane == c, v, info)
    return info


def _mm_ln_kernel(*refs, n_x, n_tail, head_tiles, y_head_tiles, k_total, tk, alpha, with_h, with_router,
                  mask_k):
    it = iter(refs)
    x_refs = [next(it) for _ in range(n_x)]
    tail_refs = [next(it) for _ in range(n_tail)]
    w_ref, y_ref = next(it), next(it)
    ys_ref = next(it) if y_head_tiles else None
    gate_ref, lng_ref, lnb_ref = next(it), next(it), next(it)
    sh_ref = next(it) if with_h else None
    sc_ref = next(it) if with_h else None
    wr_ref = next(it) if with_router else None
    yo_ref = next(it)
    h_ref = next(it) if with_h else None
    go_ref = next(it) if with_router else None
    ca_ref = next(it) if with_router else None
    acc_ref = next(it)
    cnt_ref = next(it) if with_router else None
    k = pl.program_id(1)
    nk = pl.num_programs(1)
    if with_router:
        @pl.when((pl.program_id(0) == 0) & (k == 0))
        def _():
            cnt_ref[...] = jnp.zeros_like(cnt_ref)
    cat = lambda rs: rs[0][...] if len(rs) == 1 else jnp.concatenate([r[...] for r in rs], axis=1)
    x = cat(x_refs)
    if n_tail:
        x = jnp.where(pl.program_id(0) < head_tiles, x, cat(tail_refs))
    w = w_ref[...]
    if mask_k:
        lim = k_total - k * tk
        x = jnp.where(lax.broadcasted_iota(jnp.int32, x.shape, 1) < lim, x, jnp.zeros_like(x))
        w = jnp.where(lax.broadcasted_iota(jnp.int32, w.shape, 0) < lim, w, jnp.zeros_like(w))
    part = _dot(x, w.astype(BF16))

    @pl.when(k == 0)
    def _():
        acc_ref[...] = part

    @pl.when(k > 0)
    def _():
        acc_ref[...] += part

    @pl.when(k == nk - 1)
    def _():
        y_in = y_ref[...]
        if y_head_tiles:
            y_in = jnp.where(pl.program_id(0) < y_head_tiles, y_in, ys_ref[...])
        z = alpha * y_in + gate_ref[...] * acc_ref[...]
        y = _ln_epilogue(z, lng_ref, lnb_ref)
        yo_ref[...] = y
        if with_h:
            h = (y * (1.0 + sc_ref[...]) + sh_ref[...]).astype(BF16)
            h_ref[...] = h
            if with_router:
                go_ref[...] = _router_info(h, wr_ref, cnt_ref)
                ca_ref[...] = cnt_ref[...]


def _mm_ln(tok, xs, w, w_layer, y, mods, layer, gate_which, ln_g, ln_b, alpha, tm, tk, h_mod=None,
           w_router=None, router_layer=0, xs_tail=None, name="mm_ln"):
    m, d = tok.m, w.shape[2]
    k_total = w.shape[1]
    nk = pl.cdiv(k_total, tk)
    with_h = h_mod is not None
    with_router = w_router is not None
    ins, specs = [], []
    xs_tail = xs_tail or []
    head_tiles = xs[0].shape[0] // tm if xs_tail else 0
    for x in xs:
        wx = x.shape[1] if (len(xs) > 1 or xs_tail) else tk
        ins.append(x)
        if xs_tail:
            specs.append(pl.BlockSpec((tm, wx), lambda i, k: (jnp.minimum(i, head_tiles - 1), k)))
        else:
            specs.append(pl.BlockSpec((tm, wx), lambda i, k: (i, k)))
    for x in xs_tail:
        ins.append(x)
        specs.append(pl.BlockSpec((tm, x.shape[1]), lambda i, k: (jnp.maximum(i - head_tiles, 0), k)))
    ins.append(w)
    specs.append(pl.BlockSpec((None, tk, d), lambda i, k: (w_layer, k, 0)))
    y_head_tiles = 0
    if isinstance(y, tuple):
        y_head_tiles, yspecs = _head_tail_specs(y[0], y[1], tm, d, 2)
        ins += list(y)
        specs += yspecs
    else:
        ins.append(y)
        specs.append(pl.BlockSpec((tm, d), lambda i, k: (i, 0)))
    ins += [mods, ln_g, ln_b]
    specs += [tok.mod_spec(layer, gate_which, tm, d, 2),
              _layer_spec(ln_g, layer), _layer_spec(ln_b, layer)]
    outs = [jax.ShapeDtypeStruct((m, d), F32)]
    ospecs = [pl.BlockSpec((tm, d), lambda i, k: (i, 0))]
    if with_h:
        hl, hsh, hsc = h_mod
        ins += [mods, mods]
        specs += [tok.mod_spec(hl, hsh, tm, d, 2), tok.mod_spec(hl, hsc, tm, d, 2)]
        outs.append(jax.ShapeDtypeStruct((m, d), BF16))
        ospecs.append(pl.BlockSpec((tm, d), lambda i, k: (i, 0)))
    if with_router:
        ins.append(w_router)
        specs.append(_layer_spec(w_router, router_layer))
        outs += [jax.ShapeDtypeStruct((m, LANE), F32), jax.ShapeDtypeStruct((m // tm, 1, LANE), F32)]
        ospecs += [pl.BlockSpec((tm, LANE), lambda i, k: (i, 0)),
                   pl.BlockSpec((None, 1, LANE), lambda i, k: (i, 0, 0))]
    scratch = [pltpu.VMEM((tm, d), F32)]
    if with_router:
        scratch.append(pltpu.VMEM((1, LANE), F32))
    res = pl.pallas_call(
        functools.partial(_mm_ln_kernel, n_x=len(xs), n_tail=len(xs_tail), head_tiles=head_tiles,
                          y_head_tiles=y_head_tiles, k_total=k_total, tk=tk, alpha=alpha, with_h=with_h,
                          with_router=with_router, mask_k=(k_total % tk != 0)),
        out_shape=outs, grid=(m // tm, nk), in_specs=specs, out_specs=ospecs,
        scratch_shapes=scratch,
        compiler_params=_cparams(("arbitrary", "arbitrary")), name=name,
    )(*ins)
    return res


def _ffn_up_kernel(x_ref, w1_ref, w3_ref, o_ref):
    x = x_ref[...]
    a = _silu(_dot(x, w1_ref[...].astype(BF16))) * _dot(x, w3_ref[...].astype(BF16))
    o_ref[...] = a.astype(BF16)


def _ffn_up(x, w1, w3, layer, tm, tf):
    m, d = x.shape
    f = w1.shape[2]
    wspec = pl.BlockSpec((None, d, tf), lambda j, i: (layer, 0, j))
    return pl.pallas_call(
        _ffn_up_kernel,
        out_shape=jax.ShapeDtypeStruct((m, f), BF16),
        grid=(pl.cdiv(f, tf), m // tm),
        in_specs=[pl.BlockSpec((tm, d), lambda j, i: (i, 0)), wspec, wspec],
        out_specs=pl.BlockSpec((tm, tf), lambda j, i: (i, j)),
        compiler_params=_cparams(("arbitrary", "arbitrary")), name="ffn_up",
    )(x, w1, w3)


MOE_ROWS = 1024
MOE_SEL_ROWS = 256


def _moe_plan(info, c_after, m, mc):
    tr, ts = MOE_ROWS, MOE_SEL_ROWS
    e_n = N_EXPERTS
    n_chunks = m // mc
    n_tiles = (2 * m) // tr + e_n
    n_blocks = n_tiles * (tr // ts)
    maxp = n_blocks + e_n * n_chunks
    i32 = jnp.int32
    i1, i2 = info[:, 0].astype(i32), info[:, 1].astype(i32)
    r0, r1 = info[:, 4].astype(i32), info[:, 5].astype(i32)
    ca = c_after[:, 0, :e_n].astype(i32)
    cb = jnp.concatenate([jnp.zeros((1, e_n), i32), ca[:-1]], axis=0)
    counts = ca[-1]
    padded = ((counts + tr - 1) // tr) * tr
    start = jnp.cumsum(padded) - padded
    eid = jnp.arange(e_n, dtype=i32)

    def pick(idx, table):
        return jnp.sum(jnp.where(idx[:, None] == eid[None, :], table[None, :], 0), axis=1)

    pos0 = pick(i1, start) + r0
    pos1 = pick(i2, start) + r1
    def expert_of(row0):
        return jnp.minimum(jnp.sum((row0[:, None] >= (start + padded)[None, :]).astype(i32), axis=1), e_n - 1)

    trow0 = jnp.arange(n_tiles, dtype=i32) * tr
    te = expert_of(trow0)
    tv = trow0 < jnp.sum(padded)
    row0 = jnp.arange(n_blocks, dtype=i32) * ts
    be = expert_of(row0)
    k0 = row0 - pick(be, start)
    k1 = jnp.minimum(k0 + ts, pick(be, counts))
    sel = (be[:, None] == eid[None, :])
    cb_t = jnp.sum(jnp.where(sel[:, None, :], cb[None], 0), axis=2)
    ca_t = jnp.sum(jnp.where(sel[:, None, :], ca[None], 0), axis=2)
    ov = (row0 < jnp.sum(padded))[:, None] & (cb_t < k1[:, None]) & (ca_t > k0[:, None])
    first_chunk = (jnp.arange(n_chunks) == 0)[None, :]
    ov_g = ov | (~jnp.any(ov, axis=1, keepdims=True) & first_chunk)

    def pairs(mask2d, inner):
        flat = mask2d.reshape(-1)
        n = jnp.sum(flat.astype(i32))
        idx = jnp.nonzero(flat, size=maxp, fill_value=0)[0].astype(i32)
        p = jnp.arange(maxp, dtype=i32)
        valid = p < n
        idx = jnp.where(valid, idx, jnp.max(jnp.where(valid, idx, 0)))
        outer, inn = idx // inner, idx % inner
        prev = jnp.concatenate([jnp.full((1,), -1, i32), outer[:-1]])
        nxt = jnp.concatenate([outer[1:], jnp.full((1,), -1, i32)])
        first = valid & (outer != prev)
        last = valid & ((outer != nxt) | (p == n - 1))
        return outer, inn, first.astype(i32), last.astype(i32), valid.astype(i32)

    g_tile, g_chunk, g_first, _, g_valid = pairs(ov_g, n_chunks)
    c_chunk, c_tile, c_first, c_last, c_valid = pairs(ov.T, n_blocks)
    ti = jnp.minimum(jnp.arange(n_tiles, dtype=i32), jnp.sum(tv.astype(i32)) - 1)
    return dict(pos0=pos0, pos1=pos1, g1=info[:, 2], g2=info[:, 3], te=te, tv=tv.astype(i32), ti=ti,
                gather=(g_tile, g_chunk, g_first, g_valid),
                combine=(c_tile, c_chunk, c_first, c_last, c_valid), n_tiles=n_tiles, maxp=maxp)


def _moe_gather_kernel(pt, pc, pf, pv, h_ref, p0_ref, p1_ref, g0_ref, g1_ref, xs_ref, gr_ref):
    p = pl.program_id(0)
    tr, mc = xs_ref.shape[0], h_ref.shape[0]

    @pl.when(pf[p] == 1)
    def _():
        xs_ref[...] = jnp.zeros_like(xs_ref)
        gr_ref[...] = jnp.zeros_like(gr_ref)

    @pl.when(pv[p] == 1)
    def _():
        rows = pt[p] * tr + lax.broadcasted_iota(jnp.int32, (tr, mc), 0)
        m0 = p0_ref[...] == rows
        m1 = p1_ref[...] == rows
        sel = jnp.where(m0 | m1, 1.0, 0.0).astype(BF16)
        xs_ref[...] = (xs_ref[...].astype(F32) + _dot(sel, h_ref[...])).astype(BF16)
        gr_ref[...] += jnp.sum(jnp.where(m0, g0_ref[...], 0.0) + jnp.where(m1, g1_ref[...], 0.0),
                               axis=1, keepdims=True)


def _moe_gather(h, plan, mc):
    m, d = h.shape
    tr = MOE_SEL_ROWS
    rows = plan["n_tiles"] * MOE_ROWS
    row = lambda a: a.reshape(1, m)
    tok_spec = lambda: pl.BlockSpec((1, mc), lambda p, pt, pc, pf, pv: (0, pc[p]))
    return pl.pallas_call(
        _moe_gather_kernel,
        out_shape=[jax.ShapeDtypeStruct((rows, d), BF16), jax.ShapeDtypeStruct((rows, 1), F32)],
        grid_spec=pltpu.PrefetchScalarGridSpec(
            num_scalar_prefetch=4, grid=(plan["maxp"],),
            in_specs=[pl.BlockSpec((mc, d), lambda p, pt, pc, pf, pv: (pc[p], 0)),
                      tok_spec(), tok_spec(), tok_spec(), tok_spec()],
            out_specs=[pl.BlockSpec((tr, d), lambda p, pt, pc, pf, pv: (pt[p], 0)),
                       pl.BlockSpec((tr, 1), lambda p, pt, pc, pf, pv: (pt[p], 0))]),
        compiler_params=_cparams(("arbitrary",)), name="moe_gather",
    )(*plan["gather"], h, row(plan["pos0"]), row(plan["pos1"]), row(plan["g1"]), row(plan["g2"]))


def _moe_up_kernel(te, tv, ti, x_ref, w1_ref, w3_ref, g_ref, o_ref):
    i = pl.program_id(1)

    @pl.when(tv[i] == 1)
    def _():
        x = x_ref[...]
        a = _silu(_dot(x, w1_ref[...].astype(BF16))) * _dot(x, w3_ref[...].astype(BF16))
        o_ref[...] = (a * g_ref[...]).astype(BF16)

    @pl.when(tv[i] == 0)
    def _():
        o_ref[...] = jnp.zeros_like(o_ref)


def _moe_up(xs, grow, w1, w3, layer, plan, tf):
    rows, d = xs.shape
    tr = MOE_ROWS
    fe = w1.shape[3]
    wspec = pl.BlockSpec((None, None, d, tf), lambda j, i, te, tv, ti: (layer, te[i], 0, j))
    return pl.pallas_call(
        _moe_up_kernel,
        out_shape=jax.ShapeDtypeStruct((rows, fe), BF16),
        grid_spec=pltpu.PrefetchScalarGridSpec(
            num_scalar_prefetch=3, grid=(fe // tf, rows // tr),
            in_specs=[pl.BlockSpec((tr, d), lambda j, i, te, tv, ti: (ti[i], 0)), wspec, wspec,
                      pl.BlockSpec((tr, 1), lambda j, i, te, tv, ti: (ti[i], 0))],
            out_specs=pl.BlockSpec((tr, tf), lambda j, i, te, tv, ti: (i, j))),
        compiler_params=_cparams(("arbitrary", "arbitrary")), name="moe_up",
    )(plan["te"], plan["tv"], plan["ti"], xs, w1, w3, grow)


def _moe_down_kernel(te, tv, ti, a_ref, w_ref, o_ref):
    i = pl.program_id(1)

    @pl.when(tv[i] == 1)
    def _():
        o_ref[...] = _dot(a_ref[...], w_ref[...].astype(BF16)).astype(BF16)

    @pl.when(tv[i] == 0)
    def _():
        o_ref[...] = jnp.zeros_like(o_ref)


def _moe_down(a, w2, layer, plan, tn):
    rows, fe = a.shape
    tr = MOE_ROWS
    d = w2.shape[3]
    return pl.pallas_call(
        _moe_down_kernel,
        out_shape=jax.ShapeDtypeStruct((rows, d), BF16),
        grid_spec=pltpu.PrefetchScalarGridSpec(
            num_scalar_prefetch=3, grid=(d // tn, rows // tr),
            in_specs=[pl.BlockSpec((tr, fe), lambda n, i, te, tv, ti: (ti[i], 0)),
                      pl.BlockSpec((None, None, fe, tn), lambda n, i, te, tv, ti: (layer, te[i], 0, n))],
            out_specs=pl.BlockSpec((tr, tn), lambda n, i, te, tv, ti: (i, n))),
        compiler_params=_cparams(("arbitrary", "arbitrary")), name="moe_down",
    )(plan["te"], plan["tv"], plan["ti"], a, w2)


def _moe_combine_ln_kernel(ct, cc, cf, cl, cv, ys_ref, p0_ref, p1_ref, y_ref, gate_ref, lng_ref, lnb_ref,
                           *rest, alpha, with_h, split_chunks):
    yos_ref = None
    if with_h:
        sh_ref, sc_ref, yo_ref, h_ref, acc_ref = rest
    elif split_chunks:
        yo_ref, yos_ref, acc_ref = rest
    else:
        yo_ref, acc_ref = rest
    p = pl.program_id(0)
    tr, mc = ys_ref.shape[0], y_ref.shape[0]

    @pl.when(cf[p] == 1)
    def _():
        acc_ref[...] = jnp.zeros_like(acc_ref)

    @pl.when(cv[p] == 1)
    def _():
        cols = ct[p] * tr + lax.broadcasted_iota(jnp.int32, (mc, tr), 1)
        sel = jnp.where((p0_ref[...] == cols) | (p1_ref[...] == cols), 1.0, 0.0).astype(BF16)
        acc_ref[...] += _dot(sel, ys_ref[...])

    @pl.when(cl[p] == 1)
    def _():
        z = alpha * y_ref[...] + gate_ref[...] * acc_ref[...]
        y = _ln_epilogue(z, lng_ref, lnb_ref)
        if yos_ref is None:
            yo_ref[...] = y
        else:
            @pl.when(cc[p] < split_chunks)
            def _():
                yo_ref[...] = y

            @pl.when(cc[p] >= split_chunks)
            def _():
                yos_ref[...] = y
        if with_h:
            h_ref[...] = (y * (1.0 + sc_ref[...]) + sh_ref[...]).astype(BF16)


def _moe_combine_ln(tok, ys, plan, y, mods, layer, gate_which, ln_g, ln_b, alpha, mc, h_mod=None,
                    split_rows=0):
    m, d = y.shape
    tr = MOE_SEL_ROWS
    with_h = h_mod is not None
    col = lambda a: a.reshape(m, 1)

    def mod(l, which):
        return pl.BlockSpec((None, None, 1, d),
                            lambda p, ct, cc, cf, cl, cv: (l, tok.rid(cc[p], mc), 0, which))

    chunk = lambda w: pl.BlockSpec((mc, w), lambda p, ct, cc, cf, cl, cv: (cc[p], 0))
    ins = [ys, col(plan["pos0"]), col(plan["pos1"]), y, mods, ln_g, ln_b]
    specs = [pl.BlockSpec((tr, d), lambda p, ct, cc, cf, cl, cv: (ct[p], 0)), chunk(1), chunk(1), chunk(d),
             mod(layer, gate_which), _layer_spec(ln_g, layer), _layer_spec(ln_b, layer)]
    outs = [jax.ShapeDtypeStruct((m, d), F32)]
    ospecs = [chunk(d)]
    sc = split_rows // mc
    if sc:
        assert not with_h
        outs = [jax.ShapeDtypeStruct((split_rows, d), F32), jax.ShapeDtypeStruct((m - split_rows, d), F32)]
        ospecs = [pl.BlockSpec((mc, d), lambda p, ct, cc, cf, cl, cv: (jnp.minimum(cc[p], sc - 1), 0)),
                  pl.BlockSpec((mc, d), lambda p, ct, cc, cf, cl, cv: (jnp.maximum(cc[p] - sc, 0), 0))]
    if with_h:
        hl, hsh, hsc = h_mod
        ins += [mods, mods]
        specs += [mod(hl, hsh), mod(hl, hsc)]
        outs.append(jax.ShapeDtypeStruct((m, d), BF16))
        ospecs.append(chunk(d))
    return pl.pallas_call(
        functools.partial(_moe_combine_ln_kernel, alpha=alpha, with_h=with_h, split_chunks=sc),
        out_shape=outs,
        grid_spec=pltpu.PrefetchScalarGridSpec(
            num_scalar_prefetch=5, grid=(plan["maxp"],), in_specs=specs, out_specs=ospecs,
            scratch_shapes=[pltpu.VMEM((mc, d), F32)]),
        compiler_params=_cparams(("arbitrary",)), name="moe_combine_ln",
    )(*plan["combine"], *ins)


def _rope_tables(n_tokens, dim, pad_to):
    t = jnp.arange(n_tokens)
    row = (t // GRID_W).astype(F32)
    col = (t % GRID_W).astype(F32)
    half = dim // 2
    inv_freq = ROPE_THETA ** (-jnp.arange(0, half, 2, dtype=F32) / half)
    ar = row[:, None] * inv_freq[None, :]
    ac = col[:, None] * inv_freq[None, :]
    ang = jnp.concatenate([ar, ar, ac, ac], axis=-1)
    cos, sin = jnp.cos(ang), jnp.sin(ang)
    lo = (np.arange(dim) % (dim // 2)) < dim // 4
    sa = jnp.where(lo[None, :], -sin, 0.0)
    sb = jnp.where(lo[None, :], 0.0, sin)
    return cos, sa, sb


def _pad_lanes(x, width, fill):
    return jnp.concatenate([x, jnp.full((x.shape[0], width - x.shape[1]), fill, x.dtype)], axis=1)


def kernel(x_prompt, x_sample, cache_da_k, cache_da_v, cache_mla_ckv, cache_mla_krope, cache_na_k, cache_na_v, c, c_ctx, w_ada, b_ada, w_in, da_lq1, da_lk1, da_lq2, da_lk2, da_subln, mla_gq, mla_gkv, mla_wuq, mla_wukv, na_rpb, w_out, ln1_g, ln1_b, ln2_g, ln2_b, ffn_w1, ffn_w3, ffn_w2, moe_router, moe_w1, moe_w3, moe_w2):
    nbp, seq, d = x_prompt.shape
    nbs, s_lat, _ = x_sample.shape
    depth = w_in.shape[0]
    past = cache_da_k.shape[2]
    mp, ms = nbp * seq, nbs * s_lat
    m = mp + ms
    tok = _Tok(mp, s_lat, nbs)
    tm = 512
    assert mp % s_lat == 0 and s_lat % tm == 0 and mp % tm == 0 and seq % LANE == 0 and nbs + 1 <= COND_ROWS
    alpha = (2.0 * depth) ** 0.25
    rows_n = s_lat // GRID_W

    cond = jnp.concatenate([c_ctx[None], c, jnp.zeros((COND_ROWS - 1 - nbs, d), F32)], axis=0)
    mods = _ada(cond, w_ada, b_ada).reshape(depth, COND_ROWS, 1, 6 * d)

    cos, sa, sb = _rope_tables(s_lat, DA_QK, LANE)
    rope_d = tuple(jnp.tile(t, (1, 2)) for t in (cos, sa, sb))
    rope_m = (_pad_lanes(cos, LANE, 1.0), _pad_lanes(sa, LANE, 0.0), _pad_lanes(sb, LANE, 0.0))

    y = (x_prompt.reshape(mp, d), x_sample.reshape(ms, d))
    h = _modulate(tok, y[0], y[1], mods, 0, tm)

    split = C_CKV + MLA_KV_RANK + MLA_ROPE
    w_in_b = w_in.astype(BF16)
    w_in_p = jnp.concatenate([w_in_b[:, :, :split], jnp.zeros((depth, d, C_NAQ - split), BF16),
                              w_in_b[:, :, split:]], axis=2)
    wuq = mla_wuq.reshape(depth, MLA_Q_RANK, MLA_HEADS, MLA_NOPE + MLA_ROPE)
    wuq_p = jnp.concatenate(
        [wuq, jnp.zeros((depth, MLA_Q_RANK, MLA_HEADS, MLA_QK_PAD - MLA_NOPE - MLA_ROPE), F32)],
        axis=3).reshape(depth, MLA_Q_RANK, MLA_HEADS * MLA_QK_PAD).astype(BF16)
    wukv = mla_wukv.reshape(depth, MLA_KV_RANK, MLA_HEADS, MLA_NOPE + MLA_V)
    wk = wukv[..., :MLA_NOPE].reshape(depth, MLA_KV_RANK, MLA_HEADS * MLA_NOPE).astype(BF16)
    wv = wukv[..., MLA_NOPE:].reshape(depth, MLA_KV_RANK, MLA_HEADS * MLA_V).astype(BF16)
    w_out_b = w_out.astype(BF16)
    ffn_w2_b = ffn_w2.astype(BF16)
    n_moe = moe_router.shape[0]
    wr = jnp.concatenate([moe_router, jnp.zeros((n_moe, d, LANE - N_EXPERTS), F32)], axis=2).astype(BF16)
    vec = lambda a: a.reshape(a.shape[0], 1, a.shape[1])
    gq, gkv, gsub = vec(mla_gq), vec(mla_gkv), vec(da_subln)
    lams = (vec(da_lq1), vec(da_lk1), vec(da_lq2), vec(da_lk2))
    ln1g, ln1b, ln2g, ln2b = vec(ln1_g), vec(ln1_b), vec(ln2_g), vec(ln2_b)
    cda_k = cache_da_k.reshape(nbs, depth, past, DA_HEADS * LANE)
    c_ckv = cache_mla_ckv.reshape(nbs * depth * past, MLA_KV_RANK)
    c_kr = jnp.concatenate([cache_mla_krope, jnp.zeros((nbs, depth, past, LANE - MLA_ROPE), F32)],
                           axis=-1).reshape(nbs * depth * past, LANE)
    bias, na_pat = _na_bias_tables(na_rpb, rows_n)
    stacks = None
    tm_big = 1024 if (mp % 1024 == 0 and s_lat % 1024 == 0) else tm

    st = [[] for _ in range(6)]
    for l in range(depth):
        lam_init = 0.8 - 0.6 * math.exp(-0.3 * l)
        proj = _mm(h, w_in_p, l, tm_big, IN_PROJ_TN)

        o_p, *stacks = _ctx_attention(proj, l, depth, nbp, seq, gkv, wk, wv, gq, wuq_p, lams, gsub, lam_init,
                                      stacks)

        kcat_s, vb_s, qcat_s, qa_s, ka_s = _lat_prep(proj, l, mp, ms, s_lat, tm, gkv, wk, wv, gq, wuq_p,
                                                     rope_m, rope_d)
        kcat_c, vb_c = _mla_cache(c_ckv, c_kr, l, depth, nbs, past, wk, wv)
        tq = 512 if s_lat % 512 == 0 else s_lat
        oa_s = _da(qa_s, ka_s, proj, lams, gsub, l, lam_init, nb=nbs, sq=s_lat, sk=s_lat, tq=tq,
                   vrow0=mp, vcol0=C_DAV, kc=cda_k, vc=cache_da_v, skc=past)
        ob_s = _attn(qcat_s, kcat_s, vb_s, nb=nbs, sq=s_lat, sk=s_lat, tq=tq, heads=MLA_HEADS,
                     hp=MLA_HEADS_PER_STEP, dk=MLA_QK_PAD, dv=MLA_V, scale=MLA_SCALE, kc=kcat_c, vc=vb_c,
                     skc=past)
        oc_s = _na_latent(proj, l, mp, nbs, s_lat, cache_na_k, cache_na_v, bias, na_pat)

        i = l // 2
        moe = (l % 2 == 1)
        res = _mm_ln(tok, [o_p], w_out_b, l, y, mods, l, 2, ln1g, ln1b, alpha, tm, d, h_mod=(l, 3, 4),
                     w_router=wr if moe else None, router_layer=i, xs_tail=[oa_s, ob_s, oc_s],
                     name="out_proj_ln")
        y, h2 = res[0], res[1]

        nxt = (l + 1, 0, 1) if l + 1 < depth else None
        if not moe:
            a = _ffn_up(h2, ffn_w1, ffn_w3, i, tm_big, FFN_UP_TF)
            res = _mm_ln(tok, [a], ffn_w2_b, i, y, mods, l, 5, ln2g, ln2b, alpha, tm, FFN_DOWN_TK,
                         h_mod=nxt, name="ffn_down_ln")
        else:
            plan = _moe_plan(res[2], res[3], m, tm)
            xs, grow = _moe_gather(h2, plan, tm)
            a = _moe_up(xs, grow, moe_w1, moe_w3, i, plan, 256)
            ys = _moe_down(a, moe_w2, i, plan, 512)
            res = _moe_combine_ln(tok, ys, plan, y, mods, l, 5, ln2g, ln2b, alpha, tm, h_mod=nxt,
                                  split_rows=0 if nxt is not None else mp)
        y = res[0] if nxt is not None else tuple(res)
        if nxt is not None:
            h = res[1]

        pp = proj[:mp]
        st[0].append(pp[:, C_DAK:C_DAK + 512].reshape(nbp, seq, DA_HEADS, 2, DA_QK))
        st[3].append(pp[:, C_CKV + MLA_KV_RANK:C_CKV + MLA_KV_RANK + MLA_ROPE].reshape(nbp, seq, MLA_ROPE))

    if not isinstance(y, tuple):
        y = (y[:mp], y[mp:])
    new_ckv, new_da_v, new_na_k, new_na_v = stacks
    return (y[0].reshape(nbp, seq, d), y[1].reshape(nbs, s_lat, d), jnp.stack(st[0], axis=1), new_da_v,
            new_ckv, jnp.stack(st[3], axis=1), new_na_k, new_na_v)
```
